```python
import math
import jax, jax.numpy as jnp
from jax import lax
import numpy as np

D_MODEL = 1024
BATCH = 8
SEQ = 2048
DEPTH = 2
DEC_BATCH = 128
DEC_SEQ = 8
PAST_LEN = 16384
PAGE_SIZE = 128

N_MIXERS = 2
N_POOL_LAYERS = (DEPTH + 1) // 2
N_SSM_LAYERS = DEPTH // 2
POOL_WINDOWS = (2, 4, 8, 16)
N_POOL_GROUPS = len(POOL_WINDOWS)
POOL_GROUP_DIM = D_MODEL // N_POOL_GROUPS
POOL_BUF = max(POOL_WINDOWS) - 1
SSM_GROUP_DIM = 16
SSM_GROUPS = D_MODEL // SSM_GROUP_DIM
SSM_STATE = 64
SCAN_BLOCK = 128
D_FF = -(-(8 * D_MODEL) // (3 * 256)) * 256
PLE_DIM = 256
EPS = 1e-6

kernel_name = "pool_s5_hybrid_decode_step"


def rmsnorm(x, g):
    xf = x.astype(jnp.float32)
    y = xf * lax.rsqrt(jnp.mean(xf * xf, axis=-1, keepdims=True) + EPS) * g.astype(jnp.float32)
    return y.astype(x.dtype)


def pool_mixer(h, prev, start, w_grp, scale):
    bt, t_len, _ = h.shape
    rows = jnp.concatenate([prev.astype(jnp.float32), h.astype(jnp.float32)], axis=1)
    cs = jnp.concatenate([jnp.zeros((bt, 1, D_MODEL), jnp.float32), jnp.cumsum(rows, axis=1)], axis=1)
    end = cs[:, POOL_BUF + 1:]
    pos = start + jnp.arange(t_len)
    outs = []
    for gi, w in enumerate(POOL_WINDOWS):
        c0, c1 = gi * POOL_GROUP_DIM, (gi + 1) * POOL_GROUP_DIM
        beg = cs[:, POOL_BUF + 1 - w: POOL_BUF + 1 - w + t_len, c0:c1]
        cnt = jnp.minimum(w, pos + 1).astype(jnp.float32)[None, :, None]
        diff = (end[..., c0:c1] - beg) / cnt - rows[:, POOL_BUF:, c0:c1]
        outs.append(jnp.einsum('btc,cd->btd', diff, w_grp[gi].astype(jnp.float32)))
    out = jnp.concatenate(outs, axis=-1) * scale.astype(jnp.float32)
    return out.astype(h.dtype), rows[:, -POOL_BUF:].astype(prev.dtype)


def _complex_affine_combine(e1, e2):
    a1r, a1i, b1r, b1i = e1
    a2r, a2i, b2r, b2i = e2
    return (a2r * a1r - a2i * a1i,
            a2r * a1i + a2i * a1r,
            a2r * b1r - a2i * b1i + b2r,
            a2r * b1i + a2i * b1r + b2i)


def ssm_mixer(h, h0_re, h0_im, lam_re, lam_im, log_dt, b_re, b_im, c_re, c_im, d, w_glu):
    bt, t_len, _ = h.shape
    f32 = jnp.float32
    lam_re = lam_re.astype(f32); lam_im = lam_im.astype(f32)
    dt = jnp.exp(log_dt.astype(f32))[:, None]
    mag = jnp.exp(lam_re * dt)
    lb_re = mag * jnp.cos(lam_im * dt)
    lb_im = mag * jnp.sin(lam_im * dt)
    den = lam_re * lam_re + lam_im * lam_im
    f_re = ((lb_re - 1.0) * lam_re + lb_im * lam_im) / den
    f_im = (lb_im * lam_re - (lb_re - 1.0) * lam_im) / den
    b_re = b_re.astype(f32); b_im = b_im.astype(f32)
    bb_re = f_re[..., None] * b_re - f_im[..., None] * b_im
    bb_im = f_re[..., None] * b_im + f_im[..., None] * b_re
    c_re = c_re.astype(f32); c_im = c_im.astype(f32)

    u = h.astype(f32)
    blk = SCAN_BLOCK if t_len % SCAN_BLOCK == 0 else t_len
    nb = t_len // blk
    ub = u.reshape(bt, nb, blk, SSM_GROUPS, SSM_GROUP_DIM).transpose(1, 0, 2, 3, 4)

    def step(carry, u_blk):
        hr, hi = carry
        br = jnp.einsum('blgh,gph->blgp', u_blk, bb_re)
        bi = jnp.einsum('blgh,gph->blgp', u_blk, bb_im)
        br = br.at[:, 0].add(lb_re * hr - lb_im * hi)
        bi = bi.at[:, 0].add(lb_re * hi + lb_im * hr)
        ar = jnp.broadcast_to(lb_re, br.shape)
        ai = jnp.broadcast_to(lb_im, br.shape)
        _, _, sr, si = lax.associative_scan(_complex_affine_combine, (ar, ai, br, bi), axis=1)
        y = jnp.einsum('blgp,ghp->blgh', sr, c_re) - jnp.einsum('blgp,ghp->blgh', si, c_im)
        return (sr[:, -1], si[:, -1]), y

    (hr, hi), ys = lax.scan(step, (h0_re.astype(f32), h0_im.astype(f32)), ub)
    y = ys.transpose(1, 0, 2, 3, 4).reshape(bt, t_len, D_MODEL) + d.astype(f32) * u
    z = jax.nn.gelu(y)
    a, g = jnp.split(jnp.einsum('btd,de->bte', z, w_glu.astype(f32)), 2, axis=-1)
    out = a * jax.nn.sigmoid(g)
    return out.astype(h.dtype), hr.astype(h0_re.dtype), hi.astype(h0_im.dtype)


def swiglu(h, w_gate, w_up, w_down):
    return (jax.nn.silu(h @ w_gate) * (h @ w_up)) @ w_down


def per_layer_embed(x, p_i, g, w_in, w_gate):
    gate = jax.nn.sigmoid(rmsnorm(x, g) @ w_gate)
    return (p_i @ w_in) * gate


def run_trunk(x, p, start, pool_state, ssm_re, ssm_im, prm):
    pool_new, re_new, im_new = [], [], []
    for i in range(DEPTH):
        j = i // N_MIXERS
        h = rmsnorm(x, prm['g_mix'][i])
        if i % N_MIXERS == 0:
            out, st = pool_mixer(h, pool_state[j], start, prm['pool_w'][j], prm['pool_scale'][j])
            pool_new.append(st)
        else:
            out, sr, si = ssm_mixer(h, ssm_re[j], ssm_im[j], prm['ssm_lambda_re'][j], prm['ssm_lambda_im'][j],
                                    prm['ssm_log_dt'][j], prm['ssm_b_re'][j], prm['ssm_b_im'][j],
                                    prm['ssm_c_re'][j], prm['ssm_c_im'][j], prm['ssm_d'][j], prm['ssm_w_glu'][j])
            re_new.append(sr)
            im_new.append(si)
        x = x + out
        x = x + swiglu(rmsnorm(x, prm['g_ffn'][i]), prm['ffn_w_gate'][i], prm['ffn_w_up'][i], prm['ffn_w_down'][i])
        x = x + per_layer_embed(x, p[i], prm['g_ple'][i], prm['ple_w_in'][i], prm['ple_w_gate'][i])
    return rmsnorm(x, prm['g_final']), jnp.stack(pool_new), jnp.stack(re_new), jnp.stack(im_new)


def setup_inputs(seed: int = 0) -> dict:
    key = jax.random.key(seed)
    ks = jax.random.split(key, 32)
    f32 = jnp.float32

    def nrm(k, shape, s):
        return jax.random.normal(k, shape, f32) * s

    lam_im_base = jnp.pi * jnp.arange(SSM_STATE, dtype=f32)
    return {
        'x_prompt': nrm(ks[0], (BATCH, SEQ, D_MODEL), 1.0),
        'x_sample': nrm(ks[1], (DEC_BATCH, DEC_SEQ, D_MODEL), 1.0),
        'state_pool': nrm(ks[2], (N_POOL_LAYERS, DEC_BATCH, POOL_BUF, D_MODEL), 1.0),
        'state_ssm_re': nrm(ks[3], (N_SSM_LAYERS, DEC_BATCH, SSM_GROUPS, SSM_STATE), 0.1),
        'state_ssm_im': nrm(ks[4], (N_SSM_LAYERS, DEC_BATCH, SSM_GROUPS, SSM_STATE), 0.1),
        'p_prompt': nrm(ks[5], (DEPTH, BATCH, SEQ, PLE_DIM), 1.0),
        'p_sample': nrm(ks[6], (DEPTH, DEC_BATCH, DEC_SEQ, PLE_DIM), 1.0),
        'g_mix': 1.0 + nrm(ks[7], (DEPTH, D_MODEL), 0.05),
        'g_ffn': 1.0 + nrm(ks[8], (DEPTH, D_MODEL), 0.05),
        'g_ple': 1.0 + nrm(ks[9], (DEPTH, D_MODEL), 0.05),
        'g_final': 1.0 + nrm(ks[10], (D_MODEL,), 0.05),
        'pool_w': nrm(ks[11], (N_POOL_LAYERS, N_POOL_GROUPS, POOL_GROUP_DIM, POOL_GROUP_DIM), POOL_GROUP_DIM ** -0.5),
        'pool_scale': 1.0 + nrm(ks[12], (N_POOL_LAYERS, D_MODEL), 0.1),
        'ssm_lambda_re': -0.5 + nrm(ks[13], (N_SSM_LAYERS, SSM_GROUPS, SSM_STATE), 0.01),
        'ssm_lambda_im': lam_im_base + nrm(ks[14], (N_SSM_LAYERS, SSM_GROUPS, SSM_STATE), 0.01),
        'ssm_log_dt': jax.random.uniform(ks[15], (N_SSM_LAYERS, SSM_GROUPS), f32, math.log(1e-3), math.log(1e-1)),
        'ssm_b_re': nrm(ks[16], (N_SSM_LAYERS, SSM_GROUPS, SSM_STATE, SSM_GROUP_DIM), (2 * SSM_GROUP_DIM) ** -0.5),
        'ssm_b_im': nrm(ks[17], (N_SSM_LAYERS, SSM_GROUPS, SSM_STATE, SSM_GROUP_DIM), (2 * SSM_GROUP_DIM) ** -0.5),
        'ssm_c_re': nrm(ks[18], (N_SSM_LAYERS, SSM_GROUPS, SSM_GROUP_DIM, SSM_STATE), SSM_STATE ** -0.5),
        'ssm_c_im': nrm(ks[19], (N_SSM_LAYERS, SSM_GROUPS, SSM_GROUP_DIM, SSM_STATE), SSM_STATE ** -0.5),
        'ssm_d': 1.0 + nrm(ks[20], (N_SSM_LAYERS, D_MODEL), 0.1),
        'ssm_w_glu': nrm(ks[21], (N_SSM_LAYERS, D_MODEL, 2 * D_MODEL), D_MODEL ** -0.5),
        'ffn_w_gate': nrm(ks[22], (DEPTH, D_MODEL, D_FF), D_MODEL ** -0.5),
        'ffn_w_up': nrm(ks[23], (DEPTH, D_MODEL, D_FF), D_MODEL ** -0.5),
        'ffn_w_down': nrm(ks[24], (DEPTH, D_FF, D_MODEL), D_FF ** -0.5),
        'ple_w_in': nrm(ks[25], (DEPTH, PLE_DIM, D_MODEL), PLE_DIM ** -0.5),
        'ple_w_gate': nrm(ks[26], (DEPTH, D_MODEL, D_MODEL), D_MODEL ** -0.5),
    }


def reference(x_prompt, x_sample, state_pool, state_ssm_re, state_ssm_im, p_prompt, p_sample,
              g_mix, g_ffn, g_ple, g_final, pool_w, pool_scale,
              ssm_lambda_re, ssm_lambda_im, ssm_log_dt, ssm_b_re, ssm_b_im, ssm_c_re, ssm_c_im,
              ssm_d, ssm_w_glu, ffn_w_gate, ffn_w_up, ffn_w_down, ple_w_in, ple_w_gate):
    prm = dict(g_mix=g_mix, g_ffn=g_ffn, g_ple=g_ple, g_final=g_final, pool_w=pool_w, pool_scale=pool_scale,
               ssm_lambda_re=ssm_lambda_re, ssm_lambda_im=ssm_lambda_im, ssm_log_dt=ssm_log_dt,
               ssm_b_re=ssm_b_re, ssm_b_im=ssm_b_im, ssm_c_re=ssm_c_re, ssm_c_im=ssm_c_im,
               ssm_d=ssm_d, ssm_w_glu=ssm_w_glu, ffn_w_gate=ffn_w_gate, ffn_w_up=ffn_w_up,
               ffn_w_down=ffn_w_down, ple_w_in=ple_w_in, ple_w_gate=ple_w_gate)
    pool0 = jnp.zeros((N_POOL_LAYERS, x_prompt.shape[0], POOL_BUF, D_MODEL), x_prompt.dtype)
    ssm0 = jnp.zeros((N_SSM_LAYERS, x_prompt.shape[0], SSM_GROUPS, SSM_STATE), state_ssm_re.dtype)
    y_prompt, pool_prompt, ssm_re_prompt, ssm_im_prompt = run_trunk(
        x_prompt, p_prompt, 0, pool0, ssm0, ssm0, prm)
    y_sample, pool_sample, ssm_re_sample, ssm_im_sample = run_trunk(
        x_sample, p_sample, PAST_LEN, state_pool, state_ssm_re, state_ssm_im, prm)
    return (y_prompt, y_sample, pool_prompt, pool_sample, ssm_re_prompt, ssm_im_prompt, ssm_re_sample, ssm_im_sample)
```

```python
import functools
import math

import jax
import jax.numpy as jnp
from jax import lax
from jax.experimental import pallas as pl
from jax.experimental.pallas import tpu as pltpu

D_MODEL = 1024
POOL_WINDOWS = (2, 4, 8, 16)
POOL_GROUP_DIM = D_MODEL // len(POOL_WINDOWS)
POOL_BUF = max(POOL_WINDOWS) - 1
SSM_GROUP_DIM = 16
SSM_GROUPS = D_MODEL // SSM_GROUP_DIM
SSM_STATE = 64
SSM_FLAT = SSM_GROUPS * SSM_STATE
EPS = 1e-6

V7X_SUBLANES = 8
V7X_MXU_DIM = 256
V7X_VMEM_LIMIT_BYTES = 56 * 1024 * 1024

SSM_COL_BLOCK = V7X_MXU_DIM
SSM_GROUPS_PER_BLOCK = SSM_COL_BLOCK // SSM_GROUP_DIM
SSM_STATE_BLOCK = SSM_GROUPS_PER_BLOCK * SSM_STATE
SSM_N_BLOCKS = D_MODEL // SSM_COL_BLOCK
SCAN_COLS = 512

BF16 = jnp.bfloat16
F32 = jnp.float32


def _rmsnorm(x, g):
    return x * lax.rsqrt(jnp.mean(x * x, axis=-1, keepdims=True) + EPS) * g


def _dot(a, b):
    return jnp.dot(a, b, preferred_element_type=F32)


def _const_spec(shape):
    zeros = (0,) * len(shape)
    return pl.BlockSpec(shape, lambda *_: zeros)


def _params(semantics):
    return pltpu.CompilerParams(dimension_semantics=semantics,
                                vmem_limit_bytes=V7X_VMEM_LIMIT_BYTES)


def _ssm_prep_kernel(lam_re_ref, lam_im_ref, log_dt_ref, b_re_ref, b_im_ref,
                     lb_re_ref, lb_im_ref, bb_re_ref, bb_im_ref):
    lam_re = lam_re_ref[...]
    lam_im = lam_im_ref[...]
    dt = jnp.exp(log_dt_ref[...])
    mag = jnp.exp(lam_re * dt)
    lb_re = mag * jnp.cos(lam_im * dt)
    lb_im = mag * jnp.sin(lam_im * dt)
    den = lam_re * lam_re + lam_im * lam_im
    f_re = ((lb_re - 1.0) * lam_re + lb_im * lam_im) / den
    f_im = (lb_im * lam_re - (lb_re - 1.0) * lam_im) / den
    lb_re_ref[...] = lb_re
    lb_im_ref[...] = lb_im
    b_re = b_re_ref[...]
    b_im = b_im_ref[...]
    fr = f_re[:, None, :]
    fi = f_im[:, None, :]
    bb_re_ref[...] = fr * b_re - fi * b_im
    bb_im_ref[...] = fr * b_im + fi * b_re


def _ssm_prep(lam_re, lam_im, log_dt, b_re, b_im):
    g, p, h = b_re.shape
    gp = jax.ShapeDtypeStruct((g, p), F32)
    ghp = jax.ShapeDtypeStruct((g, h, p), F32)
    return pl.pallas_call(
        _ssm_prep_kernel,
        out_shape=(gp, gp, ghp, ghp),
        name="ssm_prep",
    )(lam_re, lam_im, log_dt.reshape(g, 1),
      jnp.transpose(b_re, (0, 2, 1)), jnp.transpose(b_im, (0, 2, 1)))


def _block_diag_in(w_ghp):
    w = w_ghp.reshape(SSM_N_BLOCKS, SSM_GROUPS_PER_BLOCK, SSM_GROUP_DIM, SSM_STATE)
    eye = jnp.eye(SSM_GROUPS_PER_BLOCK, dtype=F32)
    out = jnp.einsum('cghp,gk->cghkp', w, eye)
    return out.reshape(SSM_N_BLOCKS, SSM_COL_BLOCK, SSM_STATE_BLOCK).astype(BF16)


def _block_diag_out(w_ghp):
    w = w_ghp.reshape(SSM_N_BLOCKS, SSM_GROUPS_PER_BLOCK, SSM_GROUP_DIM, SSM_STATE)
    eye = jnp.eye(SSM_GROUPS_PER_BLOCK, dtype=F32)
    out = jnp.einsum('cghp,gk->cgpkh', w, eye)
    return out.reshape(SSM_N_BLOCKS, SSM_STATE_BLOCK, SSM_COL_BLOCK).astype(BF16)


def _pool_kernel(x_ref, prev_ref, g_ref, w_ref, scale_ref, o_ref, state_ref, ext_ref,
                 *, tt, bb, start):
    tm = tt * bb
    halo = POOL_BUF * bb
    it = pl.program_id(1)

    @pl.when(it == 0)
    def _():
        ext_ref[0:halo, :] = prev_ref[...].reshape(halo, D_MODEL)

    @pl.when(it > 0)
    def _():
        ext_ref[0:halo, :] = ext_ref[tm:tm + halo, :]

    x = x_ref[...].reshape(tm, D_MODEL)
    h = _rmsnorm(x, g_ref[...])
    ext_ref[halo:halo + tm, :] = h

    row = lax.broadcasted_iota(jnp.int32, (tm, 1), 0)
    pos = start + it * tt + lax.shift_right_logical(row, int(math.log2(bb)))
    for gi, w in enumerate(POOL_WINDOWS):
        c0, c1 = gi * POOL_GROUP_DIM, (gi + 1) * POOL_GROUP_DIM
        acc = ext_ref[halo:halo + tm, c0:c1]
        for k in range(1, w):
            acc = acc + ext_ref[halo - k * bb:halo - k * bb + tm, c0:c1]
        inv_cnt = 1.0 / jnp.minimum(w, pos + 1).astype(F32)
        diff = acc * inv_cnt - ext_ref[halo:halo + tm, c0:c1]
        mixed = _dot(diff.astype(BF16), w_ref[gi]) * scale_ref[:, c0:c1]
        o_ref[:, :, c0:c1] = (x[:, c0:c1] + mixed).reshape(tt, bb, POOL_GROUP_DIM)

    state_ref[...] = ext_ref[tm:tm + halo, :].reshape(POOL_BUF, bb, D_MODEL)


def _pool_layer(x, prev, g, w, scale, *, tt, bb, start):
    t_len, b_len, _ = x.shape
    assert t_len % tt == 0 and b_len % bb == 0 and bb % V7X_SUBLANES == 0
    assert tt >= POOL_BUF or t_len == tt
    grid = (b_len // bb, t_len // tt)
    return pl.pallas_call(
        functools.partial(_pool_kernel, tt=tt, bb=bb, start=start),
        grid=grid,
        in_specs=[
            pl.BlockSpec((tt, bb, D_MODEL), lambda j, i: (i, j, 0)),
            pl.BlockSpec((POOL_BUF, bb, D_MODEL), lambda j, i: (0, j, 0)),
            _const_spec((1, D_MODEL)),
            _const_spec(w.shape),
            _const_spec((1, D_MODEL)),
        ],
        out_specs=[
            pl.BlockSpec((tt, bb, D_MODEL), lambda j, i: (i, j, 0)),
            pl.BlockSpec((POOL_BUF, bb, D_MODEL), lambda j, i: (0, j, 0)),
        ],
        out_shape=[
            jax.ShapeDtypeStruct(x.shape, F32),
            jax.ShapeDtypeStruct((POOL_BUF, b_len, D_MODEL), F32),
        ],
        scratch_shapes=[pltpu.VMEM(((POOL_BUF + tt) * bb, D_MODEL), F32)],
        compiler_params=_params(("arbitrary", "arbitrary")),
        name="pool_mixer",
    )(x, prev, g.reshape(1, D_MODEL), w, scale.reshape(1, D_MODEL))


def _ssm_kernel(x_ref, h0_re_ref, h0_im_ref, g_ref, lb_re_ref, lb_im_ref,
                wb_re_ref, wb_im_ref, wc_re_ref, wc_im_ref, d_ref, wglu_ref,
                o_ref, s_re_ref, s_im_ref,
                st_re, st_im, traj_re, traj_im, z_ref, *, tt, bb):
    tm = tt * bb
    it = pl.program_id(1)

    @pl.when(it == 0)
    def _():
        st_re[...] = h0_re_ref[...]
        st_im[...] = h0_im_ref[...]

    x = x_ref[...].reshape(tm, D_MODEL)
    u = _rmsnorm(x, g_ref[...])
    ub = u.astype(BF16)

    for cb in range(SSM_N_BLOCKS):
        cols = slice(cb * SSM_COL_BLOCK, (cb + 1) * SSM_COL_BLOCK)
        states = slice(cb * SSM_STATE_BLOCK, (cb + 1) * SSM_STATE_BLOCK)
        traj_re[:, states] = _dot(ub[:, cols], wb_re_ref[cb])
        traj_im[:, states] = _dot(ub[:, cols], wb_im_ref[cb])

    n_row_blocks = bb // V7X_SUBLANES
    n_col_blocks = SSM_FLAT // SCAN_COLS

    def chain(ci, carry):
        rb = ci // n_col_blocks
        cc = ci % n_col_blocks
        r0 = pl.multiple_of(rb * V7X_SUBLANES, V7X_SUBLANES)
        c0 = pl.multiple_of(cc * SCAN_COLS, SCAN_COLS)
        lr = jnp.broadcast_to(lb_re_ref[:, pl.ds(c0, SCAN_COLS)], (V7X_SUBLANES, SCAN_COLS))
        li = jnp.broadcast_to(lb_im_ref[:, pl.ds(c0, SCAN_COLS)], (V7X_SUBLANES, SCAN_COLS))
        sr = st_re[pl.ds(r0, V7X_SUBLANES), pl.ds(c0, SCAN_COLS)]
        si = st_im[pl.ds(r0, V7X_SUBLANES), pl.ds(c0, SCAN_COLS)]

        def step(t, s):
            sr, si = s
            rt = pl.multiple_of(t * bb + r0, V7X_SUBLANES)
            rows = pl.ds(rt, V7X_SUBLANES)
            colsl = pl.ds(c0, SCAN_COLS)
            nr = lr * sr - li * si + traj_re[rows, colsl]
            ni = lr * si + li * sr + traj_im[rows, colsl]
            traj_re[rows, colsl] = nr
            traj_im[rows, colsl] = ni
            return nr, ni

        sr, si = lax.fori_loop(0, tt, step, (sr, si), unroll=min(tt, 8))
        st_re[pl.ds(r0, V7X_SUBLANES), pl.ds(c0, SCAN_COLS)] = sr
        st_im[pl.ds(r0, V7X_SUBLANES), pl.ds(c0, SCAN_COLS)] = si
        return carry

    lax.fori_loop(0, n_row_blocks * n_col_blocks, chain, 0)

    for cb in range(SSM_N_BLOCKS):
        cols = slice(cb * SSM_COL_BLOCK, (cb + 1) * SSM_COL_BLOCK)
        states = slice(cb * SSM_STATE_BLOCK, (cb + 1) * SSM_STATE_BLOCK)
        y = (_dot(traj_re[:, states].astype(BF16), wc_re_ref[cb])
             - _dot(traj_im[:, states].astype(BF16), wc_im_ref[cb]))
        y = y + d_ref[:, cols] * u[:, cols]
        z_ref[:, cols] = jax.nn.gelu(y).astype(BF16)

    ag = _dot(z_ref[...], wglu_ref[...])
    mixed = ag[:, :D_MODEL] * jax.nn.sigmoid(ag[:, D_MODEL:])
    o_ref[...] = (x + mixed).reshape(tt, bb, D_MODEL)
    s_re_ref[...] = st_re[...]
    s_im_ref[...] = st_im[...]


def _ssm_layer(x, h0_re, h0_im, g, lb_re, lb_im, wb_re, wb_im, wc_re, wc_im, d, wglu, *, tt, bb):
    t_len, b_len, _ = x.shape
    assert t_len % tt == 0 and b_len % bb == 0 and bb % V7X_SUBLANES == 0
    tm = tt * bb
    grid = (b_len // bb, t_len // tt)
    x_spec = pl.BlockSpec((tt, bb, D_MODEL), lambda j, i: (i, j, 0))
    st_spec = pl.BlockSpec((bb, SSM_FLAT), lambda j, i: (j, 0))
    return pl.pallas_call(
        functools.partial(_ssm_kernel, tt=tt, bb=bb),
        grid=grid,
        in_specs=[
            x_spec, st_spec, st_spec,
            _const_spec((1, D_MODEL)),
            _const_spec((1, SSM_FLAT)), _const_spec((1, SSM_FLAT)),
            _const_spec(wb_re.shape), _const_spec(wb_im.shape),
            _const_spec(wc_re.shape), _const_spec(wc_im.shape),
            _const_spec((1, D_MODEL)),
            _const_spec(wglu.shape),
        ],
        out_specs=[x_spec, st_spec, st_spec],
        out_shape=[
            jax.ShapeDtypeStruct(x.shape, F32),
            jax.ShapeDtypeStruct((b_len, SSM_FLAT), F32),
            jax.ShapeDtypeStruct((b_len, SSM_FLAT), F32),
        ],
        scratch_shapes=[
            pltpu.VMEM((bb, SSM_FLAT), F32), pltpu.VMEM((bb, SSM_FLAT), F32),
            pltpu.VMEM((tm, SSM_FLAT), F32), pltpu.VMEM((tm, SSM_FLAT), F32),
            pltpu.VMEM((tm, D_MODEL), BF16),
        ],
        compiler_params=_params(("arbitrary", "arbitrary")),
        name="ssm_mixer",
    )(x, h0_re, h0_im, g.reshape(1, D_MODEL), lb_re, lb_im, wb_re, wb_im, wc_re, wc_im,
      d.reshape(1, D_MODEL), wglu)


def _ffn_ple_kernel(x_ref, p_ref, g_ffn_ref, w_gate_ref, w_up_ref, w_down_ref,
                    g_ple_ref, ple_in_ref, ple_gate_ref, g_final_ref, o_ref, *, final_norm):
    x = x_ref[...]
    hb = _rmsnorm(x, g_ffn_ref[...]).astype(BF16)
    gate = _dot(hb, w_gate_ref[...])
    up = _dot(hb, w_up_ref[...])
    act = (jax.nn.silu(gate) * up).astype(BF16)
    x = x + _dot(act, w_down_ref[...])
    hp = _rmsnorm(x, g_ple_ref[...]).astype(BF16)
    ple_gate = jax.nn.sigmoid(_dot(hp, ple_gate_ref[...]))
    x = x + _dot(p_ref[...].astype(BF16), ple_in_ref[...]) * ple_gate
    if final_norm:
        x = _rmsnorm(x, g_final_ref[...])
    o_ref[...] = x


def _ffn_ple_layer(x, p, g_ffn, w_gate, w_up, w_down, g_ple, ple_in, ple_gate, g_final,
                   *, tm, final_norm):
    rows = x.shape[0]
    assert rows % tm == 0
    ple_dim = p.shape[1]
    return pl.pallas_call(
        functools.partial(_ffn_ple_kernel, final_norm=final_norm),
        grid=(rows // tm,),
        in_specs=[
            pl.BlockSpec((tm, D_MODEL), lambda i: (i, 0)),
            pl.BlockSpec((tm, ple_dim), lambda i: (i, 0)),
            _const_spec((1, D_MODEL)),
            _const_spec(w_gate.shape), _const_spec(w_up.shape), _const_spec(w_down.shape),
            _const_spec((1, D_MODEL)),
            _const_spec(ple_in.shape), _const_spec(ple_gate.shape),
            _const_spec((1, D_MODEL)),
        ],
        out_specs=pl.BlockSpec((tm, D_MODEL), lambda i: (i, 0)),
        out_shape=jax.ShapeDtypeStruct(x.shape, F32),
        compiler_params=_params(("arbitrary",)),
        name="ffn_ple",
    )(x, p, g_ffn.reshape(1, D_MODEL), w_gate, w_up, w_down, g_ple.reshape(1, D_MODEL),
      ple_in, ple_gate, g_final.reshape(1, D_MODEL))


def _run_trunk(x_btd, p_lbtd, start, pool_state, ssm_re, ssm_im, prm, *, tt, bb, ffn_tm):
    b_len, t_len, _ = x_btd.shape
    rows = b_len * t_len
    x = jnp.transpose(x_btd, (1, 0, 2))
    p = jnp.transpose(p_lbtd, (0, 2, 1, 3)).reshape(p_lbtd.shape[0], rows, -1)
    prev = jnp.transpose(pool_state, (1, 0, 2))

    def ffn(x3d, layer, final_norm):
        y = _ffn_ple_layer(x3d.reshape(rows, D_MODEL), p[layer], prm['g_ffn'][layer],
                           prm['ffn_w_gate'][layer], prm['ffn_w_up'][layer], prm['ffn_w_down'][layer],
                           prm['g_ple'][layer], prm['ple_w_in'][layer], prm['ple_w_gate'][layer],
                           prm['g_final'], tm=ffn_tm, final_norm=final_norm)
        return y.reshape(t_len, b_len, D_MODEL)

    x, pool_new = _pool_layer(x, prev, prm['g_mix'][0], prm['pool_w'], prm['pool_scale'],
                              tt=tt, bb=bb, start=start)
    x = ffn(x, 0, False)
    x, s_re, s_im = _ssm_layer(x, ssm_re.reshape(b_len, SSM_FLAT), ssm_im.reshape(b_len, SSM_FLAT),
                               prm['g_mix'][1], prm['lb_re'], prm['lb_im'],
                               prm['wb_re'], prm['wb_im'], prm['wc_re'], prm['wc_im'],
                               prm['ssm_d'], prm['ssm_w_glu'], tt=tt, bb=bb)
    x = ffn(x, 1, True)
    y = jnp.transpose(x, (1, 0, 2))
    pool_new = jnp.transpose(pool_new, (1, 0, 2))[None]
    s_re = s_re.reshape(1, b_len, SSM_GROUPS, SSM_STATE)
    s_im = s_im.reshape(1, b_len, SSM_GROUPS, SSM_STATE)
    return y, pool_new, s_re, s_im


def kernel(x_prompt, x_sample, state_pool, state_ssm_re, state_ssm_im, p_prompt, p_sample, g_mix, g_ffn, g_ple, g_final, pool_w, pool_scale, ssm_lambda_re, ssm_lambda_im, ssm_log_dt, ssm_b_re, ssm_b_im, ssm_c_re, ssm_c_im, ssm_d, ssm_w_glu, ffn_w_gate, ffn_w_up, ffn_w_down, ple_w_in, ple_w_gate):
    past_len = 16384
    lb_re, lb_im, bb_re, bb_im = _ssm_prep(ssm_lambda_re[0], ssm_lambda_im[0], ssm_log_dt[0],
                                           ssm_b_re[0], ssm_b_im[0])
    prm = dict(
        g_mix=g_mix, g_ffn=g_ffn, g_ple=g_ple, g_final=g_final,
        pool_w=pool_w[0].astype(BF16), pool_scale=pool_scale[0],
        lb_re=lb_re.reshape(1, SSM_FLAT), lb_im=lb_im.reshape(1, SSM_FLAT),
        wb_re=_block_diag_in(bb_re), wb_im=_block_diag_in(bb_im),
        wc_re=_block_diag_out(ssm_c_re[0]), wc_im=_block_diag_out(ssm_c_im[0]),
        ssm_d=ssm_d[0], ssm_w_glu=ssm_w_glu[0].astype(BF16),
        ffn_w_gate=ffn_w_gate.astype(BF16), ffn_w_up=ffn_w_up.astype(BF16),
        ffn_w_down=ffn_w_down.astype(BF16),
        ple_w_in=ple_w_in.astype(BF16), ple_w_gate=ple_w_gate.astype(BF16),
    )
    b_p = x_prompt.shape[0]
    zeros_pool = jnp.zeros((b_p, POOL_BUF, D_MODEL), F32)
    zeros_ssm = jnp.zeros((b_p, SSM_GROUPS, SSM_STATE), F32)
    y_p, pool_p, re_p, im_p = _run_trunk(x_prompt, p_prompt, 0, zeros_pool, zeros_ssm, zeros_ssm, prm,
                                         tt=32, bb=8, ffn_tm=256)
    y_s, pool_s, re_s, im_s = _run_trunk(x_sample, p_sample, past_len, state_pool[0],
                                         state_ssm_re[0], state_ssm_im[0], prm,
                                         tt=8, bb=32, ffn_tm=256)
    return (y_p, y_s, pool_p, pool_s, re_p, im_p, re_s, im_s)
```

```python
import functools
import math

import jax
import jax.numpy as jnp
from jax import lax
from jax.experimental import pallas as pl
from jax.experimental.pallas import tpu as pltpu

D_MODEL = 1024
POOL_WINDOWS = (2, 4, 8, 16)
POOL_GROUP_DIM = D_MODEL // len(POOL_WINDOWS)
POOL_BUF = max(POOL_WINDOWS) - 1
SSM_GROUP_DIM = 16
SSM_GROUPS = D_MODEL // SSM_GROUP_DIM
SSM_STATE = 64
SSM_FLAT = SSM_GROUPS * SSM_STATE
PAST_LEN = 16384
EPS = 1e-6

V7X_SUBLANES = 8
V7X_MXU_DIM = 256
V7X_VMEM_LIMIT_BYTES = 56 * 1024 * 1024

SSM_COL_BLOCK = V7X_MXU_DIM
SSM_GROUPS_PER_BLOCK = SSM_COL_BLOCK // SSM_GROUP_DIM
SSM_STATE_BLOCK = SSM_GROUPS_PER_BLOCK * SSM_STATE
SSM_N_BLOCKS = D_MODEL // SSM_COL_BLOCK
SCAN_COLS = 512

BF16 = jnp.bfloat16
F32 = jnp.float32


def _rmsnorm(x, g):
    return x * lax.rsqrt(jnp.mean(x * x, axis=-1, keepdims=True) + EPS) * g


def _dot(a, b):
    return jnp.dot(a, b, preferred_element_type=F32)


def _time_major(block):
    bb, tt, c = block.shape
    return jnp.transpose(block, (1, 0, 2)).reshape(tt * bb, c)


def _batch_major(rows, tt, bb):
    return jnp.transpose(rows.reshape(tt, bb, rows.shape[-1]), (1, 0, 2))


def _const_spec(shape):
    zeros = (0,) * len(shape)
    return pl.BlockSpec(shape, lambda *_: zeros)


def _layer_spec(shape, layer):
    tail = (0,) * len(shape)
    return pl.BlockSpec((None,) + tuple(shape), lambda *_: (layer,) + tail)


def _params(semantics):
    return pltpu.CompilerParams(dimension_semantics=semantics,
                                vmem_limit_bytes=V7X_VMEM_LIMIT_BYTES)


def _ssm_prep_kernel(lam_re_ref, lam_im_ref, log_dt_ref, b_re_ref, b_im_ref, c_re_ref, c_im_ref,
                     lb_re_ref, lb_im_ref, wb_re_ref, wb_im_ref, wc_re_ref, wc_im_ref):
    lam_re = lam_re_ref[...]
    lam_im = lam_im_ref[...]
    dt = jnp.exp(log_dt_ref[...])
    mag = jnp.exp(lam_re * dt)
    lb_re = mag * jnp.cos(lam_im * dt)
    lb_im = mag * jnp.sin(lam_im * dt)
    den = lam_re * lam_re + lam_im * lam_im
    f_re = ((lb_re - 1.0) * lam_re + lb_im * lam_im) / den
    f_im = (lb_im * lam_re - (lb_re - 1.0) * lam_im) / den
    lb_re_ref[...] = lb_re
    lb_im_ref[...] = lb_im
    b_re = b_re_ref[...]
    b_im = b_im_ref[...]
    fr = f_re[:, None, :]
    fi = f_im[:, None, :]
    bb_re = fr * b_re - fi * b_im
    bb_im = fr * b_im + fi * b_re
    c_re = c_re_ref[...]
    c_im_neg = -c_im_ref[...]

    for ref in (wb_re_ref, wb_im_ref, wc_re_ref, wc_im_ref):
        ref[...] = jnp.zeros(ref.shape, ref.dtype)
    for g in range(SSM_GROUPS):
        cb, gl = divmod(g, SSM_GROUPS_PER_BLOCK)
        cols = slice(gl * SSM_GROUP_DIM, (gl + 1) * SSM_GROUP_DIM)
        states = slice(gl * SSM_STATE, (gl + 1) * SSM_STATE)
        wb_re_ref[cb, cols, states] = bb_re[g]
        wb_im_ref[cb, cols, states] = bb_im[g]
        wc_re_ref[cb, states, cols] = c_re[g]
        wc_im_ref[cb, states, cols] = c_im_neg[g]


def _ssm_prep(lam_re, lam_im, log_dt, b_re, b_im, c_re, c_im):
    g, p, _ = b_re.shape
    gp = jax.ShapeDtypeStruct((g, p), F32)
    wb = jax.ShapeDtypeStruct((SSM_N_BLOCKS, SSM_COL_BLOCK, SSM_STATE_BLOCK), F32)
    wc = jax.ShapeDtypeStruct((SSM_N_BLOCKS, SSM_STATE_BLOCK, SSM_COL_BLOCK), F32)
    swap = lambda a: jnp.transpose(a, (0, 2, 1))
    return pl.pallas_call(
        _ssm_prep_kernel,
        out_shape=(gp, gp, wb, wb, wc, wc),
        compiler_params=pltpu.CompilerParams(vmem_limit_bytes=V7X_VMEM_LIMIT_BYTES),
        name="ssm_prep",
    )(lam_re, lam_im, log_dt.reshape(g, 1), swap(b_re), swap(b_im), swap(c_re), swap(c_im))


def _pool_kernel(x_ref, prev_ref, g_ref, w_ref, scale_ref, o_ref, state_ref, ext_ref,
                 *, tt, bb, start):
    tm = tt * bb
    halo = POOL_BUF * bb
    it = pl.program_id(1)

    @pl.when(it == 0)
    def _():
        ext_ref[0:halo, :] = _time_major(prev_ref[...])

    @pl.when(it > 0)
    def _():
        ext_ref[0:halo, :] = ext_ref[tm:tm + halo, :]

    x = _time_major(x_ref[...])
    h = _rmsnorm(x, g_ref[...])
    ext_ref[halo:halo + tm, :] = h

    row = lax.broadcasted_iota(jnp.int32, (tm, 1), 0)
    pos = start + it * tt + lax.shift_right_logical(row, int(math.log2(bb)))
    for gi, w in enumerate(POOL_WINDOWS):
        c0, c1 = gi * POOL_GROUP_DIM, (gi + 1) * POOL_GROUP_DIM
        acc = ext_ref[halo:halo + tm, c0:c1]
        for k in range(1, w):
            acc = acc + ext_ref[halo - k * bb:halo - k * bb + tm, c0:c1]
        inv_cnt = 1.0 / jnp.minimum(w, pos + 1).astype(F32)
        diff = acc * inv_cnt - ext_ref[halo:halo + tm, c0:c1]
        mixed = _dot(diff.astype(BF16), w_ref[gi]) * scale_ref[:, c0:c1]
        o_ref[:, :, c0:c1] = (x[:, c0:c1] + mixed).reshape(tt, bb, POOL_GROUP_DIM)

    state_ref[...] = _batch_major(ext_ref[tm:tm + halo, :], POOL_BUF, bb)


def _pool_layer(x, prev, g, w, scale, *, tt, bb, start):
    b_len, t_len, _ = x.shape
    assert t_len % tt == 0 and b_len % bb == 0 and bb % V7X_SUBLANES == 0
    assert tt >= POOL_BUF or t_len == tt
    state_spec = pl.BlockSpec((bb, POOL_BUF, D_MODEL), lambda j, i: (j, 0, 0))
    return pl.pallas_call(
        functools.partial(_pool_kernel, tt=tt, bb=bb, start=start),
        grid=(b_len // bb, t_len // tt),
        in_specs=[
            pl.BlockSpec((bb, tt, D_MODEL), lambda j, i: (j, i, 0)),
            state_spec,
            _layer_spec((1, D_MODEL), 0),
            _const_spec(w.shape),
            _const_spec((1, D_MODEL)),
        ],
        out_specs=[pl.BlockSpec((tt, bb, D_MODEL), lambda j, i: (i, j, 0)), state_spec],
        out_shape=[
            jax.ShapeDtypeStruct((t_len, b_len, D_MODEL), F32),
            jax.ShapeDtypeStruct((b_len, POOL_BUF, D_MODEL), F32),
        ],
        scratch_shapes=[pltpu.VMEM(((POOL_BUF + tt) * bb, D_MODEL), F32)],
        compiler_params=_params(("arbitrary", "arbitrary")),
        name="pool_mixer",
    )(x, prev, g, w, scale)


def _ssm_kernel(x_ref, h0_re_ref, h0_im_ref, g_ref, lb_re_ref, lb_im_ref,
                wb_re_ref, wb_im_ref, wc_re_ref, wc_im_ref, d_ref, wglu_ref,
                o_ref, s_re_ref, s_im_ref,
                st_re, st_im, traj_re, traj_im, z_ref, *, tt, bb):
    tm = tt * bb
    it = pl.program_id(1)

    @pl.when(it == 0)
    def _():
        st_re[...] = h0_re_ref[...]
        st_im[...] = h0_im_ref[...]

    x = x_ref[...].reshape(tm, D_MODEL)
    u = _rmsnorm(x, g_ref[...])
    ub = u.astype(BF16)

    for cb in range(SSM_N_BLOCKS):
        cols = slice(cb * SSM_COL_BLOCK, (cb + 1) * SSM_COL_BLOCK)
        states = slice(cb * SSM_STATE_BLOCK, (cb + 1) * SSM_STATE_BLOCK)
        traj_re[:, states] = _dot(ub[:, cols], wb_re_ref[cb])
        traj_im[:, states] = _dot(ub[:, cols], wb_im_ref[cb])

    n_row_blocks = bb // V7X_SUBLANES
    n_col_blocks = SSM_FLAT // SCAN_COLS

    def chain(ci, carry):
        rb = ci // n_col_blocks
        cc = ci % n_col_blocks
        r0 = pl.multiple_of(rb * V7X_SUBLANES, V7X_SUBLANES)
        c0 = pl.multiple_of(cc * SCAN_COLS, SCAN_COLS)
        lr = jnp.broadcast_to(lb_re_ref[:, pl.ds(c0, SCAN_COLS)], (V7X_SUBLANES, SCAN_COLS))
        li = jnp.broadcast_to(lb_im_ref[:, pl.ds(c0, SCAN_COLS)], (V7X_SUBLANES, SCAN_COLS))
        sr = st_re[pl.ds(r0, V7X_SUBLANES), pl.ds(c0, SCAN_COLS)]
        si = st_im[pl.ds(r0, V7X_SUBLANES), pl.ds(c0, SCAN_COLS)]

        def step(t, s):
            sr, si = s
            rt = pl.multiple_of(t * bb + r0, V7X_SUBLANES)
            rows = pl.ds(rt, V7X_SUBLANES)
            colsl = pl.ds(c0, SCAN_COLS)
            nr = lr * sr - li * si + traj_re[rows, colsl]
            ni = lr * si + li * sr + traj_im[rows, colsl]
            traj_re[rows, colsl] = nr
            traj_im[rows, colsl] = ni
            return nr, ni

        sr, si = lax.fori_loop(0, tt, step, (sr, si), unroll=min(tt, 8))
        st_re[pl.ds(r0, V7X_SUBLANES), pl.ds(c0, SCAN_COLS)] = sr
        st_im[pl.ds(r0, V7X_SUBLANES), pl.ds(c0, SCAN_COLS)] = si
        return carry

    lax.fori_loop(0, n_row_blocks * n_col_blocks, chain, 0)

    for cb in range(SSM_N_BLOCKS):
        cols = slice(cb * SSM_COL_BLOCK, (cb + 1) * SSM_COL_BLOCK)
        states = slice(cb * SSM_STATE_BLOCK, (cb + 1) * SSM_STATE_BLOCK)
        y = (_dot(traj_re[:, states].astype(BF16), wc_re_ref[cb])
             + _dot(traj_im[:, states].astype(BF16), wc_im_ref[cb]))
        y = y + d_ref[:, cols] * u[:, cols]
        z_ref[:, cols] = jax.nn.gelu(y).astype(BF16)

    ag = _dot(z_ref[...], wglu_ref[...])
    mixed = ag[:, :D_MODEL] * jax.nn.sigmoid(ag[:, D_MODEL:])
    o_ref[...] = (x + mixed).reshape(tt, bb, D_MODEL)
    s_re_ref[...] = st_re[...]
    s_im_ref[...] = st_im[...]


def _ssm_layer(x, h0_re, h0_im, g, lb_re, lb_im, wb_re, wb_im, wc_re, wc_im, d, wglu, *, tt, bb):
    t_len, b_len, _ = x.shape
    assert t_len % tt == 0 and b_len % bb == 0 and bb % V7X_SUBLANES == 0
    tm = tt * bb
    x_spec = pl.BlockSpec((tt, bb, D_MODEL), lambda j, i: (i, j, 0))
    st_spec = pl.BlockSpec((bb, SSM_FLAT), lambda j, i: (j, 0))
    return pl.pallas_call(
        functools.partial(_ssm_kernel, tt=tt, bb=bb),
        grid=(b_len // bb, t_len // tt),
        in_specs=[
            x_spec, st_spec, st_spec,
            _layer_spec((1, D_MODEL), 1),
            _const_spec((1, SSM_FLAT)), _const_spec((1, SSM_FLAT)),
            _const_spec(wb_re.shape), _const_spec(wb_im.shape),
            _const_spec(wc_re.shape), _const_spec(wc_im.shape),
            _const_spec((1, D_MODEL)),
            _const_spec(wglu.shape),
        ],
        out_specs=[x_spec, st_spec, st_spec],
        out_shape=[
            jax.ShapeDtypeStruct(x.shape, F32),
            jax.ShapeDtypeStruct((b_len, SSM_FLAT), F32),
            jax.ShapeDtypeStruct((b_len, SSM_FLAT), F32),
        ],
        scratch_shapes=[
            pltpu.VMEM((bb, SSM_FLAT), F32), pltpu.VMEM((bb, SSM_FLAT), F32),
            pltpu.VMEM((tm, SSM_FLAT), F32), pltpu.VMEM((tm, SSM_FLAT), F32),
            pltpu.VMEM((tm, D_MODEL), BF16),
        ],
        compiler_params=_params(("arbitrary", "arbitrary")),
        name="ssm_mixer",
    )(x, h0_re, h0_im, g, lb_re, lb_im, wb_re, wb_im, wc_re, wc_im, d, wglu)


def _ffn_ple_kernel(x_ref, p_ref, g_ffn_ref, w_gate_ref, w_up_ref, w_down_ref,
                    g_ple_ref, ple_in_ref, ple_gate_ref, g_final_ref, o_ref, *, tt, bb, last):
    tm = tt * bb
    x = x_ref[...].reshape(tm, D_MODEL)
    hb = _rmsnorm(x, g_ffn_ref[...]).astype(BF16)
    gate = _dot(hb, w_gate_ref[...])
    up = _dot(hb, w_up_ref[...])
    act = (jax.nn.silu(gate) * up).astype(BF16)
    x = x + _dot(act, w_down_ref[...])
    hp = _rmsnorm(x, g_ple_ref[...]).astype(BF16)
    ple_gate = jax.nn.sigmoid(_dot(hp, ple_gate_ref[...]))
    p = _time_major(p_ref[...]).astype(BF16)
    x = x + _dot(p, ple_in_ref[...]) * ple_gate
    if last:
        o_ref[...] = _batch_major(_rmsnorm(x, g_final_ref[...]), tt, bb)
    else:
        o_ref[...] = x.reshape(tt, bb, D_MODEL)


def _ffn_ple_layer(x, p, g_ffn, w_gate, w_up, w_down, g_ple, ple_in, ple_gate, g_final,
                   *, layer, tt, bb, last):
    t_len, b_len, _ = x.shape
    assert t_len % tt == 0 and b_len % bb == 0 and bb % V7X_SUBLANES == 0
    ple_dim = p.shape[-1]
    tm_spec = pl.BlockSpec((tt, bb, D_MODEL), lambda j, i: (i, j, 0))
    bm_spec = pl.BlockSpec((bb, tt, D_MODEL), lambda j, i: (j, i, 0))
    out_shape = (b_len, t_len, D_MODEL) if last else (t_len, b_len, D_MODEL)
    return pl.pallas_call(
        functools.partial(_ffn_ple_kernel, tt=tt, bb=bb, last=last),
        grid=(b_len // bb, t_len // tt),
        in_specs=[
            tm_spec,
            pl.BlockSpec((None, bb, tt, ple_dim), lambda j, i: (layer, j, i, 0)),
            _layer_spec((1, D_MODEL), layer),
            _layer_spec(w_gate.shape[1:], layer), _layer_spec(w_up.shape[1:], layer),
            _layer_spec(w_down.shape[1:], layer),
            _layer_spec((1, D_MODEL), layer),
            _layer_spec(ple_in.shape[1:], layer), _layer_spec(ple_gate.shape[1:], layer),
            _const_spec((1, D_MODEL)),
        ],
        out_specs=bm_spec if last else tm_spec,
        out_shape=jax.ShapeDtypeStruct(out_shape, F32),
        compiler_params=_params(("arbitrary", "arbitrary")),
        name="ffn_ple",
    )(x, p, g_ffn, w_gate, w_up, w_down, g_ple, ple_in, ple_gate, g_final)


def _run_trunk(x, p, start, pool_state, ssm_re, ssm_im, prm, *, tt, bb):
    b_len = x.shape[0]

    def ffn(xt, layer, last):
        return _ffn_ple_layer(xt, p, prm['g_ffn'], prm['ffn_w_gate'], prm['ffn_w_up'], prm['ffn_w_down'],
                              prm['g_ple'], prm['ple_w_in'], prm['ple_w_gate'], prm['g_final'],
                              layer=layer, tt=tt, bb=bb, last=last)

    xt, pool_new = _pool_layer(x, pool_state, prm['g_mix'], prm['pool_w'], prm['pool_scale'],
                               tt=tt, bb=bb, start=start)
    xt = ffn(xt, 0, False)
    xt, s_re, s_im = _ssm_layer(xt, ssm_re.reshape(b_len, SSM_FLAT), ssm_im.reshape(b_len, SSM_FLAT),
                                prm['g_mix'], prm['lb_re'], prm['lb_im'],
                                prm['wb_re'], prm['wb_im'], prm['wc_re'], prm['wc_im'],
                                prm['ssm_d'], prm['ssm_w_glu'], tt=tt, bb=bb)
    y = ffn(xt, 1, True)
    s_re = s_re.reshape(1, b_len, SSM_GROUPS, SSM_STATE)
    s_im = s_im.reshape(1, b_len, SSM_GROUPS, SSM_STATE)
    return y, pool_new[None], s_re, s_im


def kernel(x_prompt, x_sample, state_pool, state_ssm_re, state_ssm_im, p_prompt, p_sample, g_mix, g_ffn, g_ple, g_final, pool_w, pool_scale, ssm_lambda_re, ssm_lambda_im, ssm_log_dt, ssm_b_re, ssm_b_im, ssm_c_re, ssm_c_im, ssm_d, ssm_w_glu, ffn_w_gate, ffn_w_up, ffn_w_down, ple_w_in, ple_w_gate):
    depth = g_mix.shape[0]
    lb_re, lb_im, wb_re, wb_im, wc_re, wc_im = _ssm_prep(
        ssm_lambda_re[0], ssm_lambda_im[0], ssm_log_dt[0], ssm_b_re[0], ssm_b_im[0], ssm_c_re[0], ssm_c_im[0])
    per_layer_vec = lambda a: a.reshape(depth, 1, D_MODEL)
    prm = dict(
        g_mix=per_layer_vec(g_mix), g_ffn=per_layer_vec(g_ffn), g_ple=per_layer_vec(g_ple),
        g_final=g_final.reshape(1, D_MODEL),
        pool_w=pool_w[0].astype(BF16), pool_scale=pool_scale,
        lb_re=lb_re.reshape(1, SSM_FLAT), lb_im=lb_im.reshape(1, SSM_FLAT),
        wb_re=wb_re.astype(BF16), wb_im=wb_im.astype(BF16),
        wc_re=wc_re.astype(BF16), wc_im=wc_im.astype(BF16),
        ssm_d=ssm_d, ssm_w_glu=ssm_w_glu[0].astype(BF16),
        ffn_w_gate=ffn_w_gate.astype(BF16), ffn_w_up=ffn_w_up.astype(BF16),
        ffn_w_down=ffn_w_down.astype(BF16),
        ple_w_in=ple_w_in.astype(BF16), ple_w_gate=ple_w_gate.astype(BF16),
    )
    b_p = x_prompt.shape[0]
    zeros_pool = jnp.zeros((b_p, POOL_BUF, D_MODEL), F32)
    zeros_ssm = jnp.zeros((b_p, SSM_GROUPS, SSM_STATE), F32)
    y_p, pool_p, re_p, im_p = _run_trunk(x_prompt, p_prompt, 0, zeros_pool, zeros_ssm, zeros_ssm, prm,
                                         tt=32, bb=8)
    y_s, pool_s, re_s, im_s = _run_trunk(x_sample, p_sample, PAST_LEN, state_pool[0],
                                         state_ssm_re[0], state_ssm_im[0], prm, tt=8, bb=32)
    return (y_p, y_s, pool_p, pool_s, re_p, im_p, re_s, im_s)
```

```python
import functools
import math

import jax
import jax.numpy as jnp
from jax import lax
from jax.experimental import pallas as pl
from jax.experimental.pallas import tpu as pltpu

D_MODEL = 1024
POOL_WINDOWS = (2, 4, 8, 16)
POOL_GROUP_DIM = D_MODEL // len(POOL_WINDOWS)
POOL_BUF = max(POOL_WINDOWS) - 1
SSM_GROUP_DIM = 16
SSM_GROUPS = D_MODEL // SSM_GROUP_DIM
SSM_STATE = 64
SSM_FLAT = SSM_GROUPS * SSM_STATE
PAST_LEN = 16384
EPS = 1e-6

V7X_SUBLANES = 8
V7X_MXU_DIM = 256
V7X_VMEM_LIMIT_BYTES = 56 * 1024 * 1024

SSM_COL_BLOCK = V7X_MXU_DIM
SSM_GROUPS_PER_BLOCK = SSM_COL_BLOCK // SSM_GROUP_DIM
SSM_STATE_BLOCK = SSM_GROUPS_PER_BLOCK * SSM_STATE
SSM_N_BLOCKS = D_MODEL // SSM_COL_BLOCK
SCAN_COLS = 512

BF16 = jnp.bfloat16
F32 = jnp.float32


def _rmsnorm(x, g):
    return x * lax.rsqrt(jnp.mean(x * x, axis=-1, keepdims=True) + EPS) * g


def _dot(a, b):
    return jnp.dot(a, b, preferred_element_type=F32)


def _time_major(block):
    bb, tt, c = block.shape
    return jnp.transpose(block, (1, 0, 2)).reshape(tt * bb, c)


def _batch_major(rows, tt, bb):
    return jnp.transpose(rows.reshape(tt, bb, rows.shape[-1]), (1, 0, 2))


def _const_spec(shape):
    zeros = (0,) * len(shape)
    return pl.BlockSpec(shape, lambda *_: zeros)


def _layer_spec(shape, layer):
    tail = (0,) * len(shape)
    return pl.BlockSpec((None,) + tuple(shape), lambda *_: (layer,) + tail)


def _params(semantics):
    return pltpu.CompilerParams(dimension_semantics=semantics,
                                vmem_limit_bytes=V7X_VMEM_LIMIT_BYTES)


def _ssm_prep_kernel(lam_re_ref, lam_im_ref, log_dt_ref, b_re_ref, b_im_ref, c_re_ref, c_im_ref,
                     lb_re_ref, lb_im_ref, wb_re_ref, wb_im_ref, wc_re_ref, wc_im_ref):
    lam_re = lam_re_ref[...]
    lam_im = lam_im_ref[...]
    dt = jnp.exp(log_dt_ref[...])
    mag = jnp.exp(lam_re * dt)
    lb_re = mag * jnp.cos(lam_im * dt)
    lb_im = mag * jnp.sin(lam_im * dt)
    den = lam_re * lam_re + lam_im * lam_im
    f_re = ((lb_re - 1.0) * lam_re + lb_im * lam_im) / den
    f_im = (lb_im * lam_re - (lb_re - 1.0) * lam_im) / den
    lb_re_ref[...] = lb_re
    lb_im_ref[...] = lb_im
    b_re = b_re_ref[...]
    b_im = b_im_ref[...]
    fr = f_re[:, None, :]
    fi = f_im[:, None, :]
    bb_re = fr * b_re - fi * b_im
    bb_im = fr * b_im + fi * b_re
    c_re = c_re_ref[...]
    c_im_neg = -c_im_ref[...]

    for ref in (wb_re_ref, wb_im_ref, wc_re_ref, wc_im_ref):
        ref[...] = jnp.zeros(ref.shape, ref.dtype)
    for g in range(SSM_GROUPS):
        cb, gl = divmod(g, SSM_GROUPS_PER_BLOCK)
        cols = slice(gl * SSM_GROUP_DIM, (gl + 1) * SSM_GROUP_DIM)
        states = slice(gl * SSM_STATE, (gl + 1) * SSM_STATE)
        wb_re_ref[cb, cols, states] = bb_re[g]
        wb_im_ref[cb, cols, states] = bb_im[g]
        wc_re_ref[cb, states, cols] = c_re[g]
        wc_im_ref[cb, states, cols] = c_im_neg[g]


def _ssm_prep(lam_re, lam_im, log_dt, b_re, b_im, c_re, c_im):
    g, p, _ = b_re.shape
    gp = jax.ShapeDtypeStruct((g, p), F32)
    wb = jax.ShapeDtypeStruct((SSM_N_BLOCKS, SSM_COL_BLOCK, SSM_STATE_BLOCK), F32)
    wc = jax.ShapeDtypeStruct((SSM_N_BLOCKS, SSM_STATE_BLOCK, SSM_COL_BLOCK), F32)
    swap = lambda a: jnp.transpose(a, (0, 2, 1))
    return pl.pallas_call(
        _ssm_prep_kernel,
        out_shape=(gp, gp, wb, wb, wc, wc),
        compiler_params=pltpu.CompilerParams(vmem_limit_bytes=V7X_VMEM_LIMIT_BYTES),
        name="ssm_prep",
    )(lam_re, lam_im, log_dt.reshape(g, 1), swap(b_re), swap(b_im), swap(c_re), swap(c_im))


def _pool_kernel(x_ref, prev_ref, g_ref, w_ref, scale_ref, o_ref, state_ref, ext_ref,
                 *, tt, bb, start):
    tm = tt * bb
    halo = POOL_BUF * bb
    it = pl.program_id(1)

    @pl.when(it == 0)
    def _():
        ext_ref[0:halo, :] = _time_major(prev_ref[...])

    @pl.when(it > 0)
    def _():
        ext_ref[0:halo, :] = ext_ref[tm:tm + halo, :]

    x = _time_major(x_ref[...])
    h = _rmsnorm(x, g_ref[...])
    ext_ref[halo:halo + tm, :] = h

    row = lax.broadcasted_iota(jnp.int32, (tm, 1), 0)
    pos = start + it * tt + lax.shift_right_logical(row, int(math.log2(bb)))
    for gi, w in enumerate(POOL_WINDOWS):
        c0, c1 = gi * POOL_GROUP_DIM, (gi + 1) * POOL_GROUP_DIM
        acc = ext_ref[halo:halo + tm, c0:c1]
        for k in range(1, w):
            acc = acc + ext_ref[halo - k * bb:halo - k * bb + tm, c0:c1]
        inv_cnt = 1.0 / jnp.minimum(w, pos + 1).astype(F32)
        diff = acc * inv_cnt - ext_ref[halo:halo + tm, c0:c1]
        mixed = _dot(diff.astype(BF16), w_ref[gi]) * scale_ref[:, c0:c1]
        o_ref[:, :, c0:c1] = (x[:, c0:c1] + mixed).reshape(tt, bb, POOL_GROUP_DIM)

    state_ref[...] = _batch_major(ext_ref[tm:tm + halo, :], POOL_BUF, bb)


def _pool_layer(x, prev, g, w, scale, *, tt, bb, start):
    b_len, t_len, _ = x.shape
    assert t_len % tt == 0 and b_len % bb == 0 and bb % V7X_SUBLANES == 0
    assert tt >= POOL_BUF or t_len == tt
    state_spec = pl.BlockSpec((bb, POOL_BUF, D_MODEL), lambda j, i: (j, 0, 0))
    return pl.pallas_call(
        functools.partial(_pool_kernel, tt=tt, bb=bb, start=start),
        grid=(b_len // bb, t_len // tt),
        in_specs=[
            pl.BlockSpec((bb, tt, D_MODEL), lambda j, i: (j, i, 0)),
            state_spec,
            _layer_spec((1, D_MODEL), 0),
            _const_spec(w.shape),
            _const_spec((1, D_MODEL)),
        ],
        out_specs=[pl.BlockSpec((tt, bb, D_MODEL), lambda j, i: (i, j, 0)), state_spec],
        out_shape=[
            jax.ShapeDtypeStruct((t_len, b_len, D_MODEL), F32),
            jax.ShapeDtypeStruct((b_len, POOL_BUF, D_MODEL), F32),
        ],
        scratch_shapes=[pltpu.VMEM(((POOL_BUF + tt) * bb, D_MODEL), F32)],
        compiler_params=_params(("arbitrary", "arbitrary")),
        name="pool_mixer",
    )(x, prev, g, w, scale)


def _ssm_kernel(x_ref, h0_re_ref, h0_im_ref, g_ref, lb_re_ref, lb_im_ref,
                wb_re_ref, wb_im_ref, wc_re_ref, wc_im_ref, d_ref, wglu_ref,
                o_ref, s_re_ref, s_im_ref,
                st_re, st_im, traj_re, traj_im, z_ref, *, tt, bb):
    tm = tt * bb
    it = pl.program_id(1)

    @pl.when(it == 0)
    def _():
        st_re[...] = h0_re_ref[...]
        st_im[...] = h0_im_ref[...]

    x = x_ref[...].reshape(tm, D_MODEL)
    u = _rmsnorm(x, g_ref[...])
    ub = u.astype(BF16)

    for cb in range(SSM_N_BLOCKS):
        cols = slice(cb * SSM_COL_BLOCK, (cb + 1) * SSM_COL_BLOCK)
        states = slice(cb * SSM_STATE_BLOCK, (cb + 1) * SSM_STATE_BLOCK)
        traj_re[:, states] = _dot(ub[:, cols], wb_re_ref[cb])
        traj_im[:, states] = _dot(ub[:, cols], wb_im_ref[cb])

        for c0 in range(cb * SSM_STATE_BLOCK, (cb + 1) * SSM_STATE_BLOCK, SCAN_COLS):
            chain_cols = slice(c0, c0 + SCAN_COLS)
            lr = jnp.broadcast_to(lb_re_ref[:, chain_cols], (V7X_SUBLANES, SCAN_COLS))
            li = jnp.broadcast_to(lb_im_ref[:, chain_cols], (V7X_SUBLANES, SCAN_COLS))
            for r0 in range(0, bb, V7X_SUBLANES):
                sr = st_re[r0:r0 + V7X_SUBLANES, chain_cols]
                si = st_im[r0:r0 + V7X_SUBLANES, chain_cols]
                for t in range(tt):
                    rows = slice(t * bb + r0, t * bb + r0 + V7X_SUBLANES)
                    sr, si = (lr * sr - li * si + traj_re[rows, chain_cols],
                              lr * si + li * sr + traj_im[rows, chain_cols])
                    traj_re[rows, chain_cols] = sr
                    traj_im[rows, chain_cols] = si
                st_re[r0:r0 + V7X_SUBLANES, chain_cols] = sr
                st_im[r0:r0 + V7X_SUBLANES, chain_cols] = si

        y = (_dot(traj_re[:, states].astype(BF16), wc_re_ref[cb])
             + _dot(traj_im[:, states].astype(BF16), wc_im_ref[cb]))
        y = y + d_ref[:, cols] * u[:, cols]
        z_ref[:, cols] = jax.nn.gelu(y).astype(BF16)

    ag = _dot(z_ref[...], wglu_ref[...])
    mixed = ag[:, :D_MODEL] * jax.nn.sigmoid(ag[:, D_MODEL:])
    o_ref[...] = (x + mixed).reshape(tt, bb, D_MODEL)
    s_re_ref[...] = st_re[...]
    s_im_ref[...] = st_im[...]


def _ssm_layer(x, h0_re, h0_im, g, lb_re, lb_im, wb_re, wb_im, wc_re, wc_im, d, wglu, *, tt, bb):
    t_len, b_len, _ = x.shape
    assert t_len % tt == 0 and b_len % bb == 0 and bb % V7X_SUBLANES == 0
    tm = tt * bb
    x_spec = pl.BlockSpec((tt, bb, D_MODEL), lambda j, i: (i, j, 0))
    st_spec = pl.BlockSpec((bb, SSM_FLAT), lambda j, i: (j, 0))
    return pl.pallas_call(
        functools.partial(_ssm_kernel, tt=tt, bb=bb),
        grid=(b_len // bb, t_len // tt),
        in_specs=[
            x_spec, st_spec, st_spec,
            _layer_spec((1, D_MODEL), 1),
            _const_spec((1, SSM_FLAT)), _const_spec((1, SSM_FLAT)),
            _const_spec(wb_re.shape), _const_spec(wb_im.shape),
            _const_spec(wc_re.shape), _const_spec(wc_im.shape),
            _const_spec((1, D_MODEL)),
            _const_spec(wglu.shape),
        ],
        out_specs=[x_spec, st_spec, st_spec],
        out_shape=[
            jax.ShapeDtypeStruct(x.shape, F32),
            jax.ShapeDtypeStruct((b_len, SSM_FLAT), F32),
            jax.ShapeDtypeStruct((b_len, SSM_FLAT), F32),
        ],
        scratch_shapes=[
            pltpu.VMEM((bb, SSM_FLAT), F32), pltpu.VMEM((bb, SSM_FLAT), F32),
            pltpu.VMEM((tm, SSM_FLAT), F32), pltpu.VMEM((tm, SSM_FLAT), F32),
            pltpu.VMEM((tm, D_MODEL), BF16),
        ],
        compiler_params=_params(("arbitrary", "arbitrary")),
        name="ssm_mixer",
    )(x, h0_re, h0_im, g, lb_re, lb_im, wb_re, wb_im, wc_re, wc_im, d, wglu)


def _ffn_ple_kernel(x_ref, p_ref, g_ffn_ref, w_gate_ref, w_up_ref, w_down_ref,
                    g_ple_ref, ple_in_ref, ple_gate_ref, g_final_ref, o_ref, *, tt, bb, last):
    tm = tt * bb
    x = x_ref[...].reshape(tm, D_MODEL)
    hb = _rmsnorm(x, g_ffn_ref[...]).astype(BF16)
    gate = _dot(hb, w_gate_ref[...])
    up = _dot(hb, w_up_ref[...])
    act = (jax.nn.silu(gate) * up).astype(BF16)
    x = x + _dot(act, w_down_ref[...])
    hp = _rmsnorm(x, g_ple_ref[...]).astype(BF16)
    ple_gate = jax.nn.sigmoid(_dot(hp, ple_gate_ref[...]))
    p = _time_major(p_ref[...]).astype(BF16)
    x = x + _dot(p, ple_in_ref[...]) * ple_gate
    if last:
        o_ref[...] = _batch_major(_rmsnorm(x, g_final_ref[...]), tt, bb)
    else:
        o_ref[...] = x.reshape(tt, bb, D_MODEL)


def _ffn_ple_layer(x, p, g_ffn, w_gate, w_up, w_down, g_ple, ple_in, ple_gate, g_final,
                   *, layer, tt, bb, last):
    t_len, b_len, _ = x.shape
    assert t_len % tt == 0 and b_len % bb == 0 and bb % V7X_SUBLANES == 0
    ple_dim = p.shape[-1]
    tm_spec = pl.BlockSpec((tt, bb, D_MODEL), lambda j, i: (i, j, 0))
    bm_spec = pl.BlockSpec((bb, tt, D_MODEL), lambda j, i: (j, i, 0))
    out_shape = (b_len, t_len, D_MODEL) if last else (t_len, b_len, D_MODEL)
    return pl.pallas_call(
        functools.partial(_ffn_ple_kernel, tt=tt, bb=bb, last=last),
        grid=(b_len // bb, t_len // tt),
        in_specs=[
            tm_spec,
            pl.BlockSpec((None, bb, tt, ple_dim), lambda j, i: (layer, j, i, 0)),
            _layer_spec((1, D_MODEL), layer),
            _layer_spec(w_gate.shape[1:], layer), _layer_spec(w_up.shape[1:], layer),
            _layer_spec(w_down.shape[1:], layer),
            _layer_spec((1, D_MODEL), layer),
            _layer_spec(ple_in.shape[1:], layer), _layer_spec(ple_gate.shape[1:], layer),
            _const_spec((1, D_MODEL)),
        ],
        out_specs=bm_spec if last else tm_spec,
        out_shape=jax.ShapeDtypeStruct(out_shape, F32),
        compiler_params=_params(("arbitrary", "arbitrary")),
        name="ffn_ple",
    )(x, p, g_ffn, w_gate, w_up, w_down, g_ple, ple_in, ple_gate, g_final)


def _run_trunk(x, p, start, pool_state, ssm_re, ssm_im, prm, *, tt, bb):
    b_len = x.shape[0]

    def ffn(xt, layer, last):
        return _ffn_ple_layer(xt, p, prm['g_ffn'], prm['ffn_w_gate'], prm['ffn_w_up'], prm['ffn_w_down'],
                              prm['g_ple'], prm['ple_w_in'], prm['ple_w_gate'], prm['g_final'],
                              layer=layer, tt=tt, bb=bb, last=last)

    xt, pool_new = _pool_layer(x, pool_state, prm['g_mix'], prm['pool_w'], prm['pool_scale'],
                               tt=tt, bb=bb, start=start)
    xt = ffn(xt, 0, False)
    xt, s_re, s_im = _ssm_layer(xt, ssm_re.reshape(b_len, SSM_FLAT), ssm_im.reshape(b_len, SSM_FLAT),
                                prm['g_mix'], prm['lb_re'], prm['lb_im'],
                                prm['wb_re'], prm['wb_im'], prm['wc_re'], prm['wc_im'],
                                prm['ssm_d'], prm['ssm_w_glu'], tt=tt, bb=bb)
    y = ffn(xt, 1, True)
    s_re = s_re.reshape(1, b_len, SSM_GROUPS, SSM_STATE)
    s_im = s_im.reshape(1, b_len, SSM_GROUPS, SSM_STATE)
    return y, pool_new[None], s_re, s_im


def kernel(x_prompt, x_sample, state_pool, state_ssm_re, state_ssm_im, p_prompt, p_sample, g_mix, g_ffn, g_ple, g_final, pool_w, pool_scale, ssm_lambda_re, ssm_lambda_im, ssm_log_dt, ssm_b_re, ssm_b_im, ssm_c_re, ssm_c_im, ssm_d, ssm_w_glu, ffn_w_gate, ffn_w_up, ffn_w_down, ple_w_in, ple_w_gate):
    depth = g_mix.shape[0]
    lb_re, lb_im, wb_re, wb_im, wc_re, wc_im = _ssm_prep(
        ssm_lambda_re[0], ssm_lambda_im[0], ssm_log_dt[0], ssm_b_re[0], ssm_b_im[0], ssm_c_re[0], ssm_c_im[0])
    per_layer_vec = lambda a: a.reshape(depth, 1, D_MODEL)
    prm = dict(
        g_mix=per_layer_vec(g_mix), g_ffn=per_layer_vec(g_ffn), g_ple=per_layer_vec(g_ple),
        g_final=g_final.reshape(1, D_MODEL),
        pool_w=pool_w[0].astype(BF16), pool_scale=pool_scale,
        lb_re=lb_re.reshape(1, SSM_FLAT), lb_im=lb_im.reshape(1, SSM_FLAT),
        wb_re=wb_re.astype(BF16), wb_im=wb_im.astype(BF16),
        wc_re=wc_re.astype(BF16), wc_im=wc_im.astype(BF16),
        ssm_d=ssm_d, ssm_w_glu=ssm_w_glu[0].astype(BF16),
        ffn_w_gate=ffn_w_gate.astype(BF16), ffn_w_up=ffn_w_up.astype(BF16),
        ffn_w_down=ffn_w_down.astype(BF16),
        ple_w_in=ple_w_in.astype(BF16), ple_w_gate=ple_w_gate.astype(BF16),
    )
    b_p = x_prompt.shape[0]
    zeros_pool = jnp.zeros((b_p, POOL_BUF, D_MODEL), F32)
    zeros_ssm = jnp.zeros((b_p, SSM_GROUPS, SSM_STATE), F32)
    y_p, pool_p, re_p, im_p = _run_trunk(x_prompt, p_prompt, 0, zeros_pool, zeros_ssm, zeros_ssm, prm,
                                         tt=32, bb=8)
    y_s, pool_s, re_s, im_s = _run_trunk(x_sample, p_sample, PAST_LEN, state_pool[0],
                                         state_ssm_re[0], state_ssm_im[0], prm, tt=8, bb=32)
    return (y_p, y_s, pool_p, pool_s, re_p, im_p, re_s, im_s)
```

```python
import functools
import math

import jax
import jax.numpy as jnp
from jax import lax
from jax.experimental import pallas as pl
from jax.experimental.pallas import tpu as pltpu

D_MODEL = 1024
POOL_WINDOWS = (2, 4, 8, 16)
POOL_GROUP_DIM = D_MODEL // len(POOL_WINDOWS)
POOL_BUF = max(POOL_WINDOWS) - 1
SSM_GROUP_DIM = 16
SSM_GROUPS = D_MODEL // SSM_GROUP_DIM
SSM_STATE = 64
SSM_FLAT = SSM_GROUPS * SSM_STATE
PAST_LEN = 16384
EPS = 1e-6

V7X_SUBLANES = 8
V7X_MXU_DIM = 256
V7X_VMEM_LIMIT_BYTES = 56 * 1024 * 1024

SSM_COL_BLOCK = V7X_MXU_DIM
SSM_GROUPS_PER_BLOCK = SSM_COL_BLOCK // SSM_GROUP_DIM
SSM_STATE_BLOCK = SSM_GROUPS_PER_BLOCK * SSM_STATE
SSM_N_BLOCKS = D_MODEL // SSM_COL_BLOCK
SCAN_COLS = 512

BF16 = jnp.bfloat16
F32 = jnp.float32


def _rmsnorm(x, g):
    return x * lax.rsqrt(jnp.mean(x * x, axis=-1, keepdims=True) + EPS) * g


def _dot(a, b):
    return jnp.dot(a, b, preferred_element_type=F32)


def _time_major(block):
    bb, tt, c = block.shape
    return jnp.transpose(block, (1, 0, 2)).reshape(tt * bb, c)


def _batch_major(rows, tt, bb):
    return jnp.transpose(rows.reshape(tt, bb, rows.shape[-1]), (1, 0, 2))


def _const_spec(shape):
    zeros = (0,) * len(shape)
    return pl.BlockSpec(shape, lambda *_: zeros, pipeline_mode=pl.Buffered(1))


def _layer_spec(shape, layer):
    tail = (0,) * len(shape)
    return pl.BlockSpec((None,) + tuple(shape), lambda *_: (layer,) + tail,
                        pipeline_mode=pl.Buffered(1))


def _params(semantics):
    return pltpu.CompilerParams(dimension_semantics=semantics,
                                vmem_limit_bytes=V7X_VMEM_LIMIT_BYTES)


def _ssm_prep_kernel(lam_re_ref, lam_im_ref, log_dt_ref, b_re_ref, b_im_ref, c_re_ref, c_im_ref,
                     lb_re_ref, lb_im_ref, wb_re_ref, wb_im_ref, wc_re_ref, wc_im_ref):
    lam_re = lam_re_ref[...]
    lam_im = lam_im_ref[...]
    dt = jnp.exp(log_dt_ref[...])
    mag = jnp.exp(lam_re * dt)
    lb_re = mag * jnp.cos(lam_im * dt)
    lb_im = mag * jnp.sin(lam_im * dt)
    den = lam_re * lam_re + lam_im * lam_im
    f_re = ((lb_re - 1.0) * lam_re + lb_im * lam_im) / den
    f_im = (lb_im * lam_re - (lb_re - 1.0) * lam_im) / den
    lb_re_ref[...] = lb_re
    lb_im_ref[...] = lb_im
    b_re = b_re_ref[...]
    b_im = b_im_ref[...]
    fr = f_re[:, None, :]
    fi = f_im[:, None, :]
    bb_re = fr * b_re - fi * b_im
    bb_im = fr * b_im + fi * b_re
    c_re = c_re_ref[...]
    c_im_neg = -c_im_ref[...]

    for ref in (wb_re_ref, wb_im_ref, wc_re_ref, wc_im_ref):
        ref[...] = jnp.zeros(ref.shape, ref.dtype)
    for g in range(SSM_GROUPS):
        cb, gl = divmod(g, SSM_GROUPS_PER_BLOCK)
        cols = slice(gl * SSM_GROUP_DIM, (gl + 1) * SSM_GROUP_DIM)
        states = slice(gl * SSM_STATE, (gl + 1) * SSM_STATE)
        wb_re_ref[cb, cols, states] = bb_re[g]
        wb_im_ref[cb, cols, states] = bb_im[g]
        wc_re_ref[cb, states, cols] = c_re[g]
        wc_im_ref[cb, states, cols] = c_im_neg[g]


def _ssm_prep(lam_re, lam_im, log_dt, b_re, b_im, c_re, c_im):
    g, p, _ = b_re.shape
    gp = jax.ShapeDtypeStruct((g, p), F32)
    wb = jax.ShapeDtypeStruct((SSM_N_BLOCKS, SSM_COL_BLOCK, SSM_STATE_BLOCK), F32)
    wc = jax.ShapeDtypeStruct((SSM_N_BLOCKS, SSM_STATE_BLOCK, SSM_COL_BLOCK), F32)
    swap = lambda a: jnp.transpose(a, (0, 2, 1))
    return pl.pallas_call(
        _ssm_prep_kernel,
        out_shape=(gp, gp, wb, wb, wc, wc),
        compiler_params=pltpu.CompilerParams(vmem_limit_bytes=V7X_VMEM_LIMIT_BYTES),
        name="ssm_prep",
    )(lam_re, lam_im, log_dt.reshape(g, 1), swap(b_re), swap(b_im), swap(c_re), swap(c_im))


def _pool_kernel(x_ref, prev_ref, g_ref, w_ref, scale_ref, o_ref, state_ref, ext_ref,
                 *, tt, bb, start):
    tm = tt * bb
    halo = POOL_BUF * bb
    it = pl.program_id(1)

    @pl.when(it == 0)
    def _():
        ext_ref[0:halo, :] = _time_major(prev_ref[...])

    @pl.when(it > 0)
    def _():
        ext_ref[0:halo, :] = ext_ref[tm:tm + halo, :]

    x = _time_major(x_ref[...])
    h = _rmsnorm(x, g_ref[...])
    ext_ref[halo:halo + tm, :] = h

    row = lax.broadcasted_iota(jnp.int32, (tm, 1), 0)
    pos = start + it * tt + lax.shift_right_logical(row, int(math.log2(bb)))
    for gi, w in enumerate(POOL_WINDOWS):
        c0, c1 = gi * POOL_GROUP_DIM, (gi + 1) * POOL_GROUP_DIM
        acc = ext_ref[halo:halo + tm, c0:c1]
        for k in range(1, w):
            acc = acc + ext_ref[halo - k * bb:halo - k * bb + tm, c0:c1]
        inv_cnt = 1.0 / jnp.minimum(w, pos + 1).astype(F32)
        diff = acc * inv_cnt - ext_ref[halo:halo + tm, c0:c1]
        mixed = _dot(diff.astype(BF16), w_ref[gi]) * scale_ref[:, c0:c1]
        o_ref[:, :, c0:c1] = (x[:, c0:c1] + mixed).reshape(tt, bb, POOL_GROUP_DIM)

    state_ref[...] = _batch_major(ext_ref[tm:tm + halo, :], POOL_BUF, bb)


def _pool_layer(x, prev, g, w, scale, *, tt, bb, start):
    b_len, t_len, _ = x.shape
    assert t_len % tt == 0 and b_len % bb == 0 and bb % V7X_SUBLANES == 0
    assert tt >= POOL_BUF or t_len == tt
    state_spec = pl.BlockSpec((bb, POOL_BUF, D_MODEL), lambda j, i: (j, 0, 0))
    return pl.pallas_call(
        functools.partial(_pool_kernel, tt=tt, bb=bb, start=start),
        grid=(b_len // bb, t_len // tt),
        in_specs=[
            pl.BlockSpec((bb, tt, D_MODEL), lambda j, i: (j, i, 0)),
            state_spec,
            _layer_spec((1, D_MODEL), 0),
            _const_spec(w.shape),
            _const_spec((1, D_MODEL)),
        ],
        out_specs=[pl.BlockSpec((tt, bb, D_MODEL), lambda j, i: (i, j, 0)), state_spec],
        out_shape=[
            jax.ShapeDtypeStruct((t_len, b_len, D_MODEL), F32),
            jax.ShapeDtypeStruct((b_len, POOL_BUF, D_MODEL), F32),
        ],
        scratch_shapes=[pltpu.VMEM(((POOL_BUF + tt) * bb, D_MODEL), F32)],
        compiler_params=_params(("arbitrary", "arbitrary")),
        name="pool_mixer",
    )(x, prev, g, w, scale)


def _ssm_kernel(x_ref, h0_re_ref, h0_im_ref, g_ref, lb_re_ref, lb_im_ref,
                wb_re_ref, wb_im_ref, wc_re_ref, wc_im_ref, d_ref, wglu_ref,
                o_ref, s_re_ref, s_im_ref,
                st_re, st_im, traj_re, traj_im, z_ref, *, tt, bb):
    tm = tt * bb
    it = pl.program_id(1)

    @pl.when(it == 0)
    def _():
        st_re[...] = h0_re_ref[...]
        st_im[...] = h0_im_ref[...]

    x = x_ref[...].reshape(tm, D_MODEL)
    u = _rmsnorm(x, g_ref[...])
    ub = u.astype(BF16)

    for cb in range(SSM_N_BLOCKS):
        cols = slice(cb * SSM_COL_BLOCK, (cb + 1) * SSM_COL_BLOCK)
        states = slice(cb * SSM_STATE_BLOCK, (cb + 1) * SSM_STATE_BLOCK)
        traj_re[:, states] = _dot(ub[:, cols], wb_re_ref[cb])
        traj_im[:, states] = _dot(ub[:, cols], wb_im_ref[cb])

        for c0 in range(cb * SSM_STATE_BLOCK, (cb + 1) * SSM_STATE_BLOCK, SCAN_COLS):
            chain_cols = slice(c0, c0 + SCAN_COLS)
            lr = jnp.broadcast_to(lb_re_ref[:, chain_cols], (V7X_SUBLANES, SCAN_COLS))
            li = jnp.broadcast_to(lb_im_ref[:, chain_cols], (V7X_SUBLANES, SCAN_COLS))
            for r0 in range(0, bb, V7X_SUBLANES):
                sr = st_re[r0:r0 + V7X_SUBLANES, chain_cols]
                si = st_im[r0:r0 + V7X_SUBLANES, chain_cols]
                for t in range(tt):
                    rows = slice(t * bb + r0, t * bb + r0 + V7X_SUBLANES)
                    sr, si = (lr * sr - li * si + traj_re[rows, chain_cols],
                              lr * si + li * sr + traj_im[rows, chain_cols])
                    traj_re[rows, chain_cols] = sr
                    traj_im[rows, chain_cols] = si
                st_re[r0:r0 + V7X_SUBLANES, chain_cols] = sr
                st_im[r0:r0 + V7X_SUBLANES, chain_cols] = si

        y = (_dot(traj_re[:, states].astype(BF16), wc_re_ref[cb])
             + _dot(traj_im[:, states].astype(BF16), wc_im_ref[cb]))
        y = y + d_ref[:, cols] * u[:, cols]
        z_ref[:, cols] = jax.nn.gelu(y).astype(BF16)

    ag = _dot(z_ref[...], wglu_ref[...])
    mixed = ag[:, :D_MODEL] * jax.nn.sigmoid(ag[:, D_MODEL:])
    o_ref[...] = (x + mixed).reshape(tt, bb, D_MODEL)
    s_re_ref[...] = st_re[...]
    s_im_ref[...] = st_im[...]


def _ssm_layer(x, h0_re, h0_im, g, lb_re, lb_im, wb_re, wb_im, wc_re, wc_im, d, wglu, *, tt, bb):
    t_len, b_len, _ = x.shape
    assert t_len % tt == 0 and b_len % bb == 0 and bb % V7X_SUBLANES == 0
    tm = tt * bb
    x_spec = pl.BlockSpec((tt, bb, D_MODEL), lambda j, i: (i, j, 0))
    st_spec = pl.BlockSpec((bb, SSM_FLAT), lambda j, i: (j, 0))
    return pl.pallas_call(
        functools.partial(_ssm_kernel, tt=tt, bb=bb),
        grid=(b_len // bb, t_len // tt),
        in_specs=[
            x_spec, st_spec, st_spec,
            _layer_spec((1, D_MODEL), 1),
            _const_spec((1, SSM_FLAT)), _const_spec((1, SSM_FLAT)),
            _const_spec(wb_re.shape), _const_spec(wb_im.shape),
            _const_spec(wc_re.shape), _const_spec(wc_im.shape),
            _const_spec((1, D_MODEL)),
            _const_spec(wglu.shape),
        ],
        out_specs=[x_spec, st_spec, st_spec],
        out_shape=[
            jax.ShapeDtypeStruct(x.shape, F32),
            jax.ShapeDtypeStruct((b_len, SSM_FLAT), F32),
            jax.ShapeDtypeStruct((b_len, SSM_FLAT), F32),
        ],
        scratch_shapes=[
            pltpu.VMEM((bb, SSM_FLAT), F32), pltpu.VMEM((bb, SSM_FLAT), F32),
            pltpu.VMEM((tm, SSM_FLAT), F32), pltpu.VMEM((tm, SSM_FLAT), F32),
            pltpu.VMEM((tm, D_MODEL), BF16),
        ],
        compiler_params=_params(("arbitrary", "arbitrary")),
        name="ssm_mixer",
    )(x, h0_re, h0_im, g, lb_re, lb_im, wb_re, wb_im, wc_re, wc_im, d, wglu)


def _ffn_ple_kernel(x_ref, p_ref, g_ffn_ref, w_gate_ref, w_up_ref, w_down_ref,
                    g_ple_ref, ple_in_ref, ple_gate_ref, g_final_ref, o_ref, *, tt, bb, last):
    tm = tt * bb
    x = x_ref[...].reshape(tm, D_MODEL)
    hb = _rmsnorm(x, g_ffn_ref[...]).astype(BF16)
    gate = _dot(hb, w_gate_ref[...])
    up = _dot(hb, w_up_ref[...])
    act = (jax.nn.silu(gate) * up).astype(BF16)
    x = x + _dot(act, w_down_ref[...])
    hp = _rmsnorm(x, g_ple_ref[...]).astype(BF16)
    ple_gate = jax.nn.sigmoid(_dot(hp, ple_gate_ref[...]))
    p = _time_major(p_ref[...]).astype(BF16)
    x = x + _dot(p, ple_in_ref[...]) * ple_gate
    if last:
        o_ref[...] = _batch_major(_rmsnorm(x, g_final_ref[...]), tt, bb)
    else:
        o_ref[...] = x.reshape(tt, bb, D_MODEL)


def _ffn_ple_layer(x, p, g_ffn, w_gate, w_up, w_down, g_ple, ple_in, ple_gate, g_final,
                   *, layer, tt, bb, last):
    t_len, b_len, _ = x.shape
    assert t_len % tt == 0 and b_len % bb == 0 and bb % V7X_SUBLANES == 0
    ple_dim = p.shape[-1]
    tm_spec = pl.BlockSpec((tt, bb, D_MODEL), lambda j, i: (i, j, 0))
    bm_spec = pl.BlockSpec((bb, tt, D_MODEL), lambda j, i: (j, i, 0))
    out_shape = (b_len, t_len, D_MODEL) if last else (t_len, b_len, D_MODEL)
    return pl.pallas_call(
        functools.partial(_ffn_ple_kernel, tt=tt, bb=bb, last=last),
        grid=(b_len // bb, t_len // tt),
        in_specs=[
            tm_spec,
            pl.BlockSpec((None, bb, tt, ple_dim), lambda j, i: (layer, j, i, 0)),
            _layer_spec((1, D_MODEL), layer),
            _layer_spec(w_gate.shape[1:], layer), _layer_spec(w_up.shape[1:], layer),
            _layer_spec(w_down.shape[1:], layer),
            _layer_spec((1, D_MODEL), layer),
            _layer_spec(ple_in.shape[1:], layer), _layer_spec(ple_gate.shape[1:], layer),
            _const_spec((1, D_MODEL)),
        ],
        out_specs=bm_spec if last else tm_spec,
        out_shape=jax.ShapeDtypeStruct(out_shape, F32),
        compiler_params=_params(("arbitrary", "arbitrary")),
        name="ffn_ple",
    )(x, p, g_ffn, w_gate, w_up, w_down, g_ple, ple_in, ple_gate, g_final)


MIXER_TILE_ROWS = 256
FFN_TILE_ROWS = 512


def _tiles(b_len, t_len, rows):
    bb = min(b_len, max(V7X_SUBLANES, rows // t_len))
    tt = min(t_len, rows // bb)
    return dict(tt=tt, bb=bb)


def _run_trunk(x, p, start, pool_state, ssm_re, ssm_im, prm):
    b_len, t_len, _ = x.shape
    mixer_tiles = _tiles(b_len, t_len, MIXER_TILE_ROWS)
    ffn_tiles = _tiles(b_len, t_len, FFN_TILE_ROWS)

    def ffn(xt, layer, last):
        return _ffn_ple_layer(xt, p, prm['g_ffn'], prm['ffn_w_gate'], prm['ffn_w_up'], prm['ffn_w_down'],
                              prm['g_ple'], prm['ple_w_in'], prm['ple_w_gate'], prm['g_final'],
                              layer=layer, last=last, **ffn_tiles)

    xt, pool_new = _pool_layer(x, pool_state, prm['g_mix'], prm['pool_w'], prm['pool_scale'],
                               start=start, **mixer_tiles)
    xt = ffn(xt, 0, False)
    xt, s_re, s_im = _ssm_layer(xt, ssm_re.reshape(b_len, SSM_FLAT), ssm_im.reshape(b_len, SSM_FLAT),
                                prm['g_mix'], prm['lb_re'], prm['lb_im'],
                                prm['wb_re'], prm['wb_im'], prm['wc_re'], prm['wc_im'],
                                prm['ssm_d'], prm['ssm_w_glu'], **mixer_tiles)
    y = ffn(xt, 1, True)
    s_re = s_re.reshape(1, b_len, SSM_GROUPS, SSM_STATE)
    s_im = s_im.reshape(1, b_len, SSM_GROUPS, SSM_STATE)
    return y, pool_new[None], s_re, s_im


def kernel(x_prompt, x_sample, state_pool, state_ssm_re, state_ssm_im, p_prompt, p_sample, g_mix, g_ffn, g_ple, g_final, pool_w, pool_scale, ssm_lambda_re, ssm_lambda_im, ssm_log_dt, ssm_b_re, ssm_b_im, ssm_c_re, ssm_c_im, ssm_d, ssm_w_glu, ffn_w_gate, ffn_w_up, ffn_w_down, ple_w_in, ple_w_gate):
    depth = g_mix.shape[0]
    lb_re, lb_im, wb_re, wb_im, wc_re, wc_im = _ssm_prep(
        ssm_lambda_re[0], ssm_lambda_im[0], ssm_log_dt[0], ssm_b_re[0], ssm_b_im[0], ssm_c_re[0], ssm_c_im[0])
    per_layer_vec = lambda a: a.reshape(depth, 1, D_MODEL)
    prm = dict(
        g_mix=per_layer_vec(g_mix), g_ffn=per_layer_vec(g_ffn), g_ple=per_layer_vec(g_ple),
        g_final=g_final.reshape(1, D_MODEL),
        pool_w=pool_w[0].astype(BF16), pool_scale=pool_scale,
        lb_re=lb_re.reshape(1, SSM_FLAT), lb_im=lb_im.reshape(1, SSM_FLAT),
        wb_re=wb_re.astype(BF16), wb_im=wb_im.astype(BF16),
        wc_re=wc_re.astype(BF16), wc_im=wc_im.astype(BF16),
        ssm_d=ssm_d, ssm_w_glu=ssm_w_glu[0].astype(BF16),
        ffn_w_gate=ffn_w_gate.astype(BF16), ffn_w_up=ffn_w_up.astype(BF16),
        ffn_w_down=ffn_w_down.astype(BF16),
        ple_w_in=ple_w_in.astype(BF16), ple_w_gate=ple_w_gate.astype(BF16),
    )
    b_p = x_prompt.shape[0]
    zeros_pool = jnp.zeros((b_p, POOL_BUF, D_MODEL), F32)
    zeros_ssm = jnp.zeros((b_p, SSM_GROUPS, SSM_STATE), F32)
    y_p, pool_p, re_p, im_p = _run_trunk(x_prompt, p_prompt, 0, zeros_pool, zeros_ssm, zeros_ssm, prm)
    y_s, pool_s, re_s, im_s = _run_trunk(x_sample, p_sample, PAST_LEN, state_pool[0],
                                         state_ssm_re[0], state_ssm_im[0], prm)
    return (y_p, y_s, pool_p, pool_s, re_p, im_p, re_s, im_s)
```

```python
import functools
import math

import jax
import jax.numpy as jnp
from jax import lax
from jax.experimental import pallas as pl
from jax.experimental.pallas import tpu as pltpu

D_MODEL = 1024
POOL_WINDOWS = (2, 4, 8, 16)
POOL_GROUP_DIM = D_MODEL // len(POOL_WINDOWS)
POOL_BUF = max(POOL_WINDOWS) - 1
SSM_GROUP_DIM = 16
SSM_GROUPS = D_MODEL // SSM_GROUP_DIM
SSM_STATE = 64
SSM_FLAT = SSM_GROUPS * SSM_STATE
PAST_LEN = 16384
EPS = 1e-6

V7X_SUBLANES = 8
V7X_MXU_DIM = 256
V7X_VMEM_LIMIT_BYTES = 60 * 1024 * 1024

SSM_COL_BLOCK = V7X_MXU_DIM
SSM_GROUPS_PER_BLOCK = SSM_COL_BLOCK // SSM_GROUP_DIM
SSM_STATE_BLOCK = SSM_GROUPS_PER_BLOCK * SSM_STATE
SSM_N_BLOCKS = D_MODEL // SSM_COL_BLOCK
SCAN_COLS = 512
TILE_ROWS = 256
FFN_COLS = 2 * V7X_MXU_DIM
OUT_COLS = 2 * V7X_MXU_DIM
POOL_MIX_DELAY = 1

BF16 = jnp.bfloat16
F32 = jnp.float32


def _rmsnorm(x, g):
    return x * lax.rsqrt(jnp.mean(x * x, axis=-1, keepdims=True) + EPS) * g


def _dot(a, b):
    return jnp.dot(a, b, preferred_element_type=F32)


def _time_major(block):
    bb, tt, c = block.shape
    return jnp.transpose(block, (1, 0, 2)).reshape(tt * bb, c)


def _batch_major(rows, tt, bb):
    return jnp.transpose(rows.reshape(tt, bb, rows.shape[-1]), (1, 0, 2))


def _const_spec(shape):
    zeros = (0,) * len(shape)
    return pl.BlockSpec(shape, lambda *_: zeros, pipeline_mode=pl.Buffered(1))


def _layer_spec(shape, layer):
    tail = (0,) * len(shape)
    return pl.BlockSpec((None,) + tuple(shape), lambda *_: (layer,) + tail,
                        pipeline_mode=pl.Buffered(1))


def _params():
    return pltpu.CompilerParams(dimension_semantics=("arbitrary", "arbitrary"),
                                vmem_limit_bytes=V7X_VMEM_LIMIT_BYTES)


def _tiles(b_len, t_len):
    bb = min(b_len, max(V7X_SUBLANES, TILE_ROWS // t_len))
    tt = min(t_len, TILE_ROWS // bb)
    assert t_len % tt == 0 and b_len % bb == 0 and bb % V7X_SUBLANES == 0
    return tt, bb


class _Schedule:
    def __init__(self, b_len, t_len):
        self.tt, self.bb = _tiles(b_len, t_len)
        self.nt = t_len // self.tt
        self.lag = 1 if self.nt > 1 else 0
        self.grid = (b_len // self.bb, self.nt + self.lag)

    def mixer_block(self, i):
        return jnp.minimum(i, self.nt - 1)

    def ffn_block(self, i):
        return jnp.maximum(i - self.lag, 0)


def _ssm_prep_kernel(lam_re_ref, lam_im_ref, log_dt_ref, b_re_ref, b_im_ref, c_re_ref, c_im_ref,
                     lb_re_ref, lb_im_ref, wb_re_ref, wb_im_ref, wc_re_ref, wc_im_ref):
    lam_re = lam_re_ref[...]
    lam_im = lam_im_ref[...]
    dt = jnp.exp(log_dt_ref[...])
    mag = jnp.exp(lam_re * dt)
    lb_re = mag * jnp.cos(lam_im * dt)
    lb_im = mag * jnp.sin(lam_im * dt)
    den = lam_re * lam_re + lam_im * lam_im
    f_re = ((lb_re - 1.0) * lam_re + lb_im * lam_im) / den
    f_im = (lb_im * lam_re - (lb_re - 1.0) * lam_im) / den
    lb_re_ref[...] = lb_re
    lb_im_ref[...] = lb_im
    b_re = b_re_ref[...]
    b_im = b_im_ref[...]
    fr = f_re[:, None, :]
    fi = f_im[:, None, :]
    bb_re = fr * b_re - fi * b_im
    bb_im = fr * b_im + fi * b_re
    c_re = c_re_ref[...]
    c_im_neg = -c_im_ref[...]

    for ref in (wb_re_ref, wb_im_ref, wc_re_ref, wc_im_ref):
        ref[...] = jnp.zeros(ref.shape, ref.dtype)
    for g in range(SSM_GROUPS):
        cb, gl = divmod(g, SSM_GROUPS_PER_BLOCK)
        cols = slice(gl * SSM_GROUP_DIM, (gl + 1) * SSM_GROUP_DIM)
        states = slice(gl * SSM_STATE, (gl + 1) * SSM_STATE)
        wb_re_ref[cb, cols, states] = bb_re[g]
        wb_im_ref[cb, cols, states] = bb_im[g]
        wc_re_ref[cb, states, cols] = c_re[g]
        wc_im_ref[cb, states, cols] = c_im_neg[g]


def _ssm_prep(lam_re, lam_im, log_dt, b_re, b_im, c_re, c_im):
    g, p, _ = b_re.shape
    gp = jax.ShapeDtypeStruct((g, p), F32)
    wb = jax.ShapeDtypeStruct((SSM_N_BLOCKS, SSM_COL_BLOCK, SSM_STATE_BLOCK), F32)
    wc = jax.ShapeDtypeStruct((SSM_N_BLOCKS, SSM_STATE_BLOCK, SSM_COL_BLOCK), F32)
    swap = lambda a: jnp.transpose(a, (0, 2, 1))
    return pl.pallas_call(
        _ssm_prep_kernel,
        out_shape=(gp, gp, wb, wb, wc, wc),
        compiler_params=pltpu.CompilerParams(vmem_limit_bytes=V7X_VMEM_LIMIT_BYTES),
        name="ssm_prep",
    )(lam_re, lam_im, log_dt.reshape(g, 1), swap(b_re), swap(b_im), swap(c_re), swap(c_im))


def _pool_pieces(x_ref, dst_ref, ext_ref, xt_ref, diff_ref, w_ref, scale_ref, g, *, tt, bb, first_pos,
                 write_round):
    tm = tt * bb
    halo = POOL_BUF * bb
    xt_ref[...] = _time_major(x_ref[...])
    ext_ref[halo:halo + tm, :] = _rmsnorm(xt_ref[...], g)
    yield
    row = lax.broadcasted_iota(jnp.int32, (tm, 1), 0)
    pos = first_pos + lax.shift_right_logical(row, int(math.log2(bb)))
    n_groups = len(POOL_WINDOWS)

    def window_mean_minus_self(gi):
        cols = slice(gi * POOL_GROUP_DIM, (gi + 1) * POOL_GROUP_DIM)
        w = POOL_WINDOWS[gi]
        acc = ext_ref[halo:halo + tm, cols]
        for k in range(1, w):
            acc = acc + ext_ref[halo - k * bb:halo - k * bb + tm, cols]
        inv_cnt = 1.0 / jnp.minimum(w, pos + 1).astype(F32)
        diff_ref[:, cols] = (acc * inv_cnt - ext_ref[halo:halo + tm, cols]).astype(BF16)

    def mix(gi):
        cols = slice(gi * POOL_GROUP_DIM, (gi + 1) * POOL_GROUP_DIM)
        xt_ref[:, cols] += _dot(diff_ref[:, cols], w_ref[gi]) * scale_ref[:, cols]

    rounds = 1
    for k in range(n_groups + POOL_MIX_DELAY):
        if k >= POOL_MIX_DELAY:
            mix(k - POOL_MIX_DELAY)
        if k < n_groups:
            window_mean_minus_self(k)
        rounds += 1
        yield
    for _ in range(write_round - rounds):
        yield
    dst_ref[...] = xt_ref[...]
    yield


def _ssm_pieces(x_ref, dst_ref, g, lb_re_ref, lb_im_ref, wb_re_ref, wb_im_ref, wc_re_ref, wc_im_ref,
                d_ref, wglu_ref, st_re, st_im, traj_re, traj_im, u_ref, ub_ref, z_ref, *, tt, bb):
    tm = tt * bb
    u = _rmsnorm(x_ref[...].reshape(tm, D_MODEL), g)
    u_ref[...] = u
    ub_ref[...] = u.astype(BF16)
    yield

    def block_slices(cb):
        return (slice(cb * SSM_COL_BLOCK, (cb + 1) * SSM_COL_BLOCK),
                slice(cb * SSM_STATE_BLOCK, (cb + 1) * SSM_STATE_BLOCK))

    def project_in(cb):
        cols, states = block_slices(cb)
        traj_re[:, states] = _dot(ub_ref[:, cols], wb_re_ref[cb])
        traj_im[:, states] = _dot(ub_ref[:, cols], wb_im_ref[cb])

    def scan(cb):
        for c0 in range(cb * SSM_STATE_BLOCK, (cb + 1) * SSM_STATE_BLOCK, SCAN_COLS):
            chain_cols = slice(c0, c0 + SCAN_COLS)
            lr = jnp.broadcast_to(lb_re_ref[:, chain_cols], (V7X_SUBLANES, SCAN_COLS))
            li = jnp.broadcast_to(lb_im_ref[:, chain_cols], (V7X_SUBLANES, SCAN_COLS))
            for r0 in range(0, bb, V7X_SUBLANES):
                sr = st_re[r0:r0 + V7X_SUBLANES, chain_cols]
                si = st_im[r0:r0 + V7X_SUBLANES, chain_cols]
                for t in range(tt):
                    rows = slice(t * bb + r0, t * bb + r0 + V7X_SUBLANES)
                    sr, si = (lr * sr - li * si + traj_re[rows, chain_cols],
                              lr * si + li * sr + traj_im[rows, chain_cols])
                    traj_re[rows, chain_cols] = sr
                    traj_im[rows, chain_cols] = si
                st_re[r0:r0 + V7X_SUBLANES, chain_cols] = sr
                st_im[r0:r0 + V7X_SUBLANES, chain_cols] = si

    def project_out(cb):
        cols, states = block_slices(cb)
        y = (_dot(traj_re[:, states].astype(BF16), wc_re_ref[cb])
             + _dot(traj_im[:, states].astype(BF16), wc_im_ref[cb]))
        y = y + d_ref[:, cols] * u_ref[:, cols]
        z_ref[:, cols] = jax.nn.gelu(y).astype(BF16)

    for k in range(SSM_N_BLOCKS + 2):
        if 0 <= k - 2:
            project_out(k - 2)
        if k < SSM_N_BLOCKS:
            project_in(k)
        if 0 <= k - 1 < SSM_N_BLOCKS:
            scan(k - 1)
        yield

    for n0 in range(0, D_MODEL, OUT_COLS):
        cols = slice(n0, n0 + OUT_COLS)
        a = _dot(z_ref[...], wglu_ref[:, cols])
        gate = _dot(z_ref[...], wglu_ref[:, D_MODEL + n0:D_MODEL + n0 + OUT_COLS])
        dst_ref[:, cols] = x_ref[:, :, cols].reshape(tm, OUT_COLS) + a * jax.nn.sigmoid(gate)
        yield


def _ffn_ple_pieces(rows_ref, h_in_ref, p_ref, w_gate_ref, w_up_ref, w_down_ref, g_ple, ple_in_ref, ple_gate_ref,
                    h_ref, pt_ref, act_ref, release_h_in, emit):
    d_ff = act_ref.shape[1]
    pt_ref[...] = _time_major(p_ref[...]).astype(BF16)
    for c0 in range(0, d_ff, FFN_COLS):
        cols = slice(c0, min(c0 + FFN_COLS, d_ff))
        gate = _dot(h_in_ref[...], w_gate_ref[:, cols])
        up = _dot(h_in_ref[...], w_up_ref[:, cols])
        act_ref[:, cols] = (jax.nn.silu(gate) * up).astype(BF16)
        yield
    release_h_in()
    for n0 in range(0, D_MODEL, OUT_COLS):
        cols = slice(n0, n0 + OUT_COLS)
        rows_ref[:, cols] += _dot(act_ref[...], w_down_ref[:, cols])
        yield
    h_ref[...] = _rmsnorm(rows_ref[...], g_ple).astype(BF16)
    for n0 in range(0, D_MODEL, OUT_COLS):
        cols = slice(n0, n0 + OUT_COLS)
        ple_gate = jax.nn.sigmoid(_dot(h_ref[...], ple_gate_ref[:, cols]))
        rows_ref[:, cols] += _dot(pt_ref[...], ple_in_ref[:, cols]) * ple_gate
        yield
    emit()
    yield


def _ffn_rounds_reading_rows(d_ff):
    return -(-d_ff // FFN_COLS) + D_MODEL // OUT_COLS


_DONE = object()


def _interleave(*stages):
    live = list(stages)
    while live:
        for stage in list(live):
            if next(stage, _DONE) is _DONE:
                live.remove(stage)


def _then(stage, last_piece):
    yield from stage
    last_piece()
    yield


def _run_halves(sched, g_ffn, mid_ref, hmid_ref, rows_ref, mixer, ffn):
    h_in_reads_issued = []

    def mixer_into(dst_ref):
        def norm_for_ffn():
            assert h_in_reads_issued or not sched.lag, "hmid_ref rewritten before the FFN half read it"
            hmid_ref[...] = _rmsnorm(dst_ref[...], g_ffn).astype(BF16)
        return _then(mixer(dst_ref), norm_for_ffn)

    def release_h_in():
        h_in_reads_issued.append(True)

    if sched.lag:
        rows_ref[...] = mid_ref[...]
        _interleave(ffn(hmid_ref, release_h_in), mixer_into(mid_ref))
    else:
        _interleave(mixer_into(rows_ref))
        _interleave(ffn(hmid_ref, release_h_in))


def _init_handoff(sched, mid_ref, hmid_ref):
    if sched.lag:
        mid_ref[...] = jnp.zeros(mid_ref.shape, mid_ref.dtype)
        hmid_ref[...] = jnp.zeros(hmid_ref.shape, hmid_ref.dtype)


def _ffn_scratch(sched, d_ff, ple_dim):
    tm = sched.tt * sched.bb
    mid_rows = tm if sched.lag else 2 * V7X_SUBLANES
    return [
        pltpu.VMEM((mid_rows, D_MODEL), F32),
        pltpu.VMEM((tm, D_MODEL), BF16),
        pltpu.VMEM((tm, D_MODEL), F32),
        pltpu.VMEM((tm, D_MODEL), BF16),
        pltpu.VMEM((tm, ple_dim), BF16),
        pltpu.VMEM((tm, d_ff), BF16),
    ]


def _layer0_kernel(x_ref, prev_ref, p_ref, g_mix_ref, pool_w_ref, pool_scale_ref,
                   g_ffn_ref, w_gate_ref, w_up_ref, w_down_ref, g_ple_ref, ple_in_ref, ple_gate_ref,
                   o_ref, state_ref, ext_ref, xt_ref, diff_ref, mid_ref, hmid_ref, rows_ref, h_ref, pt_ref, act_ref,
                   *, sched, start):
    tt, bb = sched.tt, sched.bb
    tm = tt * bb
    halo = POOL_BUF * bb
    i = pl.program_id(1)

    @pl.when(i == 0)
    def _():
        ext_ref[0:halo, :] = _time_major(prev_ref[...])
        _init_handoff(sched, mid_ref, hmid_ref)

    @pl.when(i > 0)
    def _():
        ext_ref[0:halo, :] = ext_ref[tm:tm + halo, :]

    def mixer(dst_ref):
        return _pool_pieces(x_ref, dst_ref, ext_ref, xt_ref, diff_ref, pool_w_ref, pool_scale_ref, g_mix_ref[...],
                            tt=tt, bb=bb, first_pos=start + sched.mixer_block(i) * tt,
                            write_round=_ffn_rounds_reading_rows(act_ref.shape[1]) if sched.lag else 0)

    def emit():
        o_ref[...] = rows_ref[...].reshape(tt, bb, D_MODEL)

    def ffn(h_in_ref, release_h_in):
        return _ffn_ple_pieces(rows_ref, h_in_ref, p_ref, w_gate_ref, w_up_ref, w_down_ref,
                               g_ple_ref[...], ple_in_ref, ple_gate_ref, h_ref, pt_ref, act_ref,
                               release_h_in, emit)

    _run_halves(sched, g_ffn_ref[...], mid_ref, hmid_ref, rows_ref, mixer, ffn)

    @pl.when(i == sched.nt - 1)
    def _():
        state_ref[...] = _batch_major(ext_ref[tm:tm + halo, :], POOL_BUF, bb)


def _layer0(x, prev, p, prm, *, start):
    b_len, t_len, _ = x.shape
    sched = _Schedule(b_len, t_len)
    tt, bb = sched.tt, sched.bb
    state_spec = pl.BlockSpec((bb, POOL_BUF, D_MODEL), lambda j, i: (j, 0, 0))
    return pl.pallas_call(
        functools.partial(_layer0_kernel, sched=sched, start=start),
        grid=sched.grid,
        in_specs=[
            pl.BlockSpec((bb, tt, D_MODEL), lambda j, i: (j, sched.mixer_block(i), 0)),
            state_spec,
            pl.BlockSpec((None, bb, tt, p.shape[-1]), lambda j, i: (0, j, sched.ffn_block(i), 0)),
            _layer_spec((1, D_MODEL), 0),
            _const_spec(prm['pool_w'].shape),
            _const_spec((1, D_MODEL)),
            _layer_spec((1, D_MODEL), 0),
            _layer_spec(prm['ffn_w_gate'].shape[1:], 0), _layer_spec(prm['ffn_w_up'].shape[1:], 0),
            _layer_spec(prm['ffn_w_down'].shape[1:], 0),
            _layer_spec((1, D_MODEL), 0),
            _layer_spec(prm['ple_w_in'].shape[1:], 0), _layer_spec(prm['ple_w_gate'].shape[1:], 0),
        ],
        out_specs=[pl.BlockSpec((tt, bb, D_MODEL), lambda j, i: (sched.ffn_block(i), j, 0)), state_spec],
        out_shape=[
            jax.ShapeDtypeStruct((t_len, b_len, D_MODEL), F32),
            jax.ShapeDtypeStruct((b_len, POOL_BUF, D_MODEL), F32),
        ],
        scratch_shapes=([pltpu.VMEM(((POOL_BUF + tt) * bb, D_MODEL), F32),
                         pltpu.VMEM((tt * bb, D_MODEL), F32), pltpu.VMEM((tt * bb, D_MODEL), BF16)]
                        + _ffn_scratch(sched, prm['ffn_w_gate'].shape[-1], p.shape[-1])),
        compiler_params=_params(),
        name="layer0_pool_ffn",
    )(x, prev, p, prm['g_mix'], prm['pool_w'], prm['pool_scale'],
      prm['g_ffn'], prm['ffn_w_gate'], prm['ffn_w_up'], prm['ffn_w_down'],
      prm['g_ple'], prm['ple_w_in'], prm['ple_w_gate'])


def _layer1_kernel(x_ref, h0_re_ref, h0_im_ref, p_ref, g_mix_ref, lb_re_ref, lb_im_ref,
                   wb_re_ref, wb_im_ref, wc_re_ref, wc_im_ref, d_ref, wglu_ref,
                   g_ffn_ref, w_gate_ref, w_up_ref, w_down_ref, g_ple_ref, ple_in_ref, ple_gate_ref,
                   g_final_ref, o_ref, s_re_ref, s_im_ref,
                   st_re, st_im, traj_re, traj_im, u_ref, ub_ref, z_ref,
                   mid_ref, hmid_ref, rows_ref, h_ref, pt_ref, act_ref, *, sched):
    tt, bb = sched.tt, sched.bb
    i = pl.program_id(1)

    @pl.when(i == 0)
    def _():
        st_re[...] = h0_re_ref[...]
        st_im[...] = h0_im_ref[...]
        _init_handoff(sched, mid_ref, hmid_ref)

    def mixer(dst_ref):
        return _ssm_pieces(x_ref, dst_ref, g_mix_ref[...], lb_re_ref, lb_im_ref,
                           wb_re_ref, wb_im_ref, wc_re_ref, wc_im_ref, d_ref, wglu_ref,
                           st_re, st_im, traj_re, traj_im, u_ref, ub_ref, z_ref, tt=tt, bb=bb)

    def emit():
        o_ref[...] = _batch_major(_rmsnorm(rows_ref[...], g_final_ref[...]), tt, bb)

    def ffn(h_in_ref, release_h_in):
        return _ffn_ple_pieces(rows_ref, h_in_ref, p_ref, w_gate_ref, w_up_ref, w_down_ref,
                               g_ple_ref[...], ple_in_ref, ple_gate_ref, h_ref, pt_ref, act_ref,
                               release_h_in, emit)

    _run_halves(sched, g_ffn_ref[...], mid_ref, hmid_ref, rows_ref, mixer, ffn)

    @pl.when(i == sched.nt - 1)
    def _():
        s_re_ref[...] = st_re[...]
        s_im_ref[...] = st_im[...]


def _layer1(x, h0_re, h0_im, p, prm):
    t_len, b_len, _ = x.shape
    sched = _Schedule(b_len, t_len)
    tt, bb = sched.tt, sched.bb
    tm = tt * bb
    st_spec = pl.BlockSpec((bb, SSM_FLAT), lambda j, i: (j, 0))
    return pl.pallas_call(
        functools.partial(_layer1_kernel, sched=sched),
        grid=sched.grid,
        in_specs=[
            pl.BlockSpec((tt, bb, D_MODEL), lambda j, i: (sched.mixer_block(i), j, 0)),
            st_spec, st_spec,
            pl.BlockSpec((None, bb, tt, p.shape[-1]), lambda j, i: (1, j, sched.ffn_block(i), 0)),
            _layer_spec((1, D_MODEL), 1),
            _const_spec((1, SSM_FLAT)), _const_spec((1, SSM_FLAT)),
            _const_spec(prm['wb_re'].shape), _const_spec(prm['wb_im'].shape),
            _const_spec(prm['wc_re'].shape), _const_spec(prm['wc_im'].shape),
            _const_spec((1, D_MODEL)),
            _const_spec(prm['ssm_w_glu'].shape),
            _layer_spec((1, D_MODEL), 1),
            _layer_spec(prm['ffn_w_gate'].shape[1:], 1), _layer_spec(prm['ffn_w_up'].shape[1:], 1),
            _layer_spec(prm['ffn_w_down'].shape[1:], 1),
            _layer_spec((1, D_MODEL), 1),
            _layer_spec(prm['ple_w_in'].shape[1:], 1), _layer_spec(prm['ple_w_gate'].shape[1:], 1),
            _const_spec((1, D_MODEL)),
        ],
        out_specs=[pl.BlockSpec((bb, tt, D_MODEL), lambda j, i: (j, sched.ffn_block(i), 0)),
                   st_spec, st_spec],
        out_shape=[
            jax.ShapeDtypeStruct((b_len, t_len, D_MODEL), F32),
            jax.ShapeDtypeStruct((b_len, SSM_FLAT), F32),
            jax.ShapeDtypeStruct((b_len, SSM_FLAT), F32),
        ],
        scratch_shapes=[
            pltpu.VMEM((bb, SSM_FLAT), F32), pltpu.VMEM((bb, SSM_FLAT), F32),
            pltpu.VMEM((tm, SSM_FLAT), F32), pltpu.VMEM((tm, SSM_FLAT), F32),
            pltpu.VMEM((tm, D_MODEL), F32),
            pltpu.VMEM((tm, D_MODEL), BF16), pltpu.VMEM((tm, D_MODEL), BF16),
        ] + _ffn_scratch(sched, prm['ffn_w_gate'].shape[-1], p.shape[-1]),
        compiler_params=_params(),
        name="layer1_ssm_ffn",
    )(x, h0_re, h0_im, p, prm['g_mix'], prm['lb_re'], prm['lb_im'],
      prm['wb_re'], prm['wb_im'], prm['wc_re'], prm['wc_im'], prm['ssm_d'], prm['ssm_w_glu'],
      prm['g_ffn'], prm['ffn_w_gate'], prm['ffn_w_up'], prm['ffn_w_down'],
      prm['g_ple'], prm['ple_w_in'], prm['ple_w_gate'], prm['g_final'])


def _run_trunk(x, p, start, pool_state, ssm_re, ssm_im, prm):
    b_len = x.shape[0]
    xt, pool_new = _layer0(x, pool_state, p, prm, start=start)
    y, s_re, s_im = _layer1(xt, ssm_re.reshape(b_len, SSM_FLAT), ssm_im.reshape(b_len, SSM_FLAT), p, prm)
    s_re = s_re.reshape(1, b_len, SSM_GROUPS, SSM_STATE)
    s_im = s_im.reshape(1, b_len, SSM_GROUPS, SSM_STATE)
    return y, pool_new[None], s_re, s_im


def kernel(x_prompt, x_sample, state_pool, state_ssm_re, state_ssm_im, p_prompt, p_sample, g_mix, g_ffn, g_ple, g_final, pool_w, pool_scale, ssm_lambda_re, ssm_lambda_im, ssm_log_dt, ssm_b_re, ssm_b_im, ssm_c_re, ssm_c_im, ssm_d, ssm_w_glu, ffn_w_gate, ffn_w_up, ffn_w_down, ple_w_in, ple_w_gate):
    depth = g_mix.shape[0]
    lb_re, lb_im, wb_re, wb_im, wc_re, wc_im = _ssm_prep(
        ssm_lambda_re[0], ssm_lambda_im[0], ssm_log_dt[0], ssm_b_re[0], ssm_b_im[0], ssm_c_re[0], ssm_c_im[0])
    per_layer_vec = lambda a: a.reshape(depth, 1, D_MODEL)
    prm = dict(
        g_mix=per_layer_vec(g_mix), g_ffn=per_layer_vec(g_ffn), g_ple=per_layer_vec(g_ple),
        g_final=g_final.reshape(1, D_MODEL),
        pool_w=pool_w[0].astype(BF16), pool_scale=pool_scale,
        lb_re=lb_re.reshape(1, SSM_FLAT), lb_im=lb_im.reshape(1, SSM_FLAT),
        wb_re=wb_re.astype(BF16), wb_im=wb_im.astype(BF16),
        wc_re=wc_re.astype(BF16), wc_im=wc_im.astype(BF16),
        ssm_d=ssm_d, ssm_w_glu=ssm_w_glu[0].astype(BF16),
        ffn_w_gate=ffn_w_gate.astype(BF16), ffn_w_up=ffn_w_up.astype(BF16),
        ffn_w_down=ffn_w_down.astype(BF16),
        ple_w_in=ple_w_in.astype(BF16), ple_w_gate=ple_w_gate.astype(BF16),
    )
    b_p = x_prompt.shape[0]
    zeros_pool = jnp.zeros((b_p, POOL_BUF, D_MODEL), F32)
    zeros_ssm = jnp.zeros((b_p, SSM_GROUPS, SSM_STATE), F32)
    y_p, pool_p, re_p, im_p = _run_trunk(x_prompt, p_prompt, 0, zeros_pool, zeros_ssm, zeros_ssm, prm)
    y_s, pool_s, re_s, im_s = _run_trunk(x_sample, p_sample, PAST_LEN, state_pool[0],
                                         state_ssm_re[0], state_ssm_im[0], prm)
    return (y_p, y_s, pool_p, pool_s, re_p, im_p, re_s, im_s)
```

```python
import functools
import math

import jax
import jax.numpy as jnp
from jax import lax
from jax.experimental import pallas as pl
from jax.experimental.pallas import tpu as pltpu

D_MODEL = 1024
POOL_WINDOWS = (2, 4, 8, 16)
POOL_GROUP_DIM = D_MODEL // len(POOL_WINDOWS)
POOL_BUF = max(POOL_WINDOWS) - 1
SSM_GROUP_DIM = 16
SSM_GROUPS = D_MODEL // SSM_GROUP_DIM
SSM_STATE = 64
SSM_FLAT = SSM_GROUPS * SSM_STATE
PAST_LEN = 16384
EPS = 1e-6

V7X_SUBLANES = 8
V7X_LANES = 128
V7X_MXU_DIM = 256
V7X_VMEM_LIMIT_BYTES = 60 * 1024 * 1024

SSM_COL_BLOCK = V7X_MXU_DIM
SSM_GROUPS_PER_BLOCK = SSM_COL_BLOCK // SSM_GROUP_DIM
SSM_STATE_BLOCK = SSM_GROUPS_PER_BLOCK * SSM_STATE
SSM_N_BLOCKS = D_MODEL // SSM_COL_BLOCK
SSM_LANE_GROUPS = V7X_LANES // SSM_GROUP_DIM
SSM_LANE_STATES = SSM_LANE_GROUPS * SSM_STATE
SSM_N_LANE_BLOCKS = D_MODEL // V7X_LANES
SSM_PAIR_COLS = 2 * V7X_LANES
SSM_PAIR_STATES = 2 * SSM_LANE_STATES

POOL_LAYER_ROWS = 512
SSM_ROWS = 512
FFN_ROWS = 512
FFN_COLS = 2 * V7X_MXU_DIM
OUT_COLS = 2 * V7X_MXU_DIM

BF16 = jnp.bfloat16
F32 = jnp.float32


def _rmsnorm(x, g):
    return x * lax.rsqrt(jnp.mean(x * x, axis=-1, keepdims=True) + EPS) * g


def _dot(a, b):
    return jnp.dot(a, b, preferred_element_type=F32)


def _dot_nt_f32(a, b):
    return lax.dot_general(a, b, (((1,), (1,)), ((), ())), precision=lax.Precision.HIGHEST,
                           preferred_element_type=F32)


def _time_major(block):
    bb, tt, c = block.shape
    return jnp.transpose(block, (1, 0, 2)).reshape(tt * bb, c)


def _batch_major(rows, tt, bb):
    return jnp.transpose(rows.reshape(tt, bb, rows.shape[-1]), (1, 0, 2))


def _const_spec(shape):
    zeros = (0,) * len(shape)
    return pl.BlockSpec(shape, lambda *_: zeros, pipeline_mode=pl.Buffered(1))


def _layer_spec(shape, layer):
    tail = (0,) * len(shape)
    return pl.BlockSpec((None,) + tuple(shape), lambda *_: (layer,) + tail,
                        pipeline_mode=pl.Buffered(1))


def _params():
    return pltpu.CompilerParams(dimension_semantics=("arbitrary", "arbitrary"),
                                vmem_limit_bytes=V7X_VMEM_LIMIT_BYTES)


def _tiles(b_len, t_len, rows):
    bb = min(b_len, max(V7X_SUBLANES, rows // t_len))
    tt = min(t_len, rows // bb)
    assert t_len % tt == 0 and b_len % bb == 0 and bb % V7X_SUBLANES == 0
    return tt, bb


class _Schedule:
    def __init__(self, b_len, t_len, rows):
        self.tt, self.bb = _tiles(b_len, t_len, rows)
        self.nt = t_len // self.tt
        self.lag = 1 if self.nt > 1 else 0
        self.grid = (b_len // self.bb, self.nt + self.lag)

    def mixer_block(self, i):
        return jnp.minimum(i, self.nt - 1)

    def ffn_block(self, i):
        return jnp.maximum(i - self.lag, 0)


def _ssm_prep_kernel(lam_re_ref, lam_im_ref, log_dt_ref, b_re_ref, b_im_ref, c_re_ref, c_im_ref,
                     a2_re_ref, a2_im_ref, in_ref, out_ref, direct_ref,
                     lb_re_row, lb_im_row, wb_re, wb_im, wc_re, wc_im):
    lam_re = lam_re_ref[...]
    lam_im = lam_im_ref[...]
    dt = jnp.exp(log_dt_ref[...])
    mag = jnp.exp(lam_re * dt)
    lb_re = mag * jnp.cos(lam_im * dt)
    lb_im = mag * jnp.sin(lam_im * dt)
    den = lam_re * lam_re + lam_im * lam_im
    f_re = ((lb_re - 1.0) * lam_re + lb_im * lam_im) / den
    f_im = (lb_im * lam_re - (lb_re - 1.0) * lam_im) / den
    b_re = b_re_ref[...]
    b_im = b_im_ref[...]
    fr = f_re[:, None, :]
    fi = f_im[:, None, :]
    bb_re = fr * b_re - fi * b_im
    bb_im = fr * b_im + fi * b_re
    c_re = c_re_ref[...]
    c_im_neg = -c_im_ref[...]

    for ref in (wb_re, wb_im, wc_re, wc_im):
        ref[...] = jnp.zeros(ref.shape, ref.dtype)
    for g in range(SSM_GROUPS):
        cb, gl = divmod(g, SSM_GROUPS_PER_BLOCK)
        cols = slice(gl * SSM_GROUP_DIM, (gl + 1) * SSM_GROUP_DIM)
        states = slice(gl * SSM_STATE, (gl + 1) * SSM_STATE)
        wb_re[cb, cols, states] = bb_re[g]
        wb_im[cb, cols, states] = bb_im[g]
        wc_re[cb, cols, states] = c_re[g]
        wc_im[cb, cols, states] = c_im_neg[g]
        flat = slice(g * SSM_STATE, (g + 1) * SSM_STATE)
        lb_re_row[:, flat] = lb_re[g:g + 1, :]
        lb_im_row[:, flat] = lb_im[g:g + 1, :]

    lr_all = lb_re_row[...]
    li_all = lb_im_row[...]
    a2_re_ref[...] = lr_all * lr_all - li_all * li_all
    a2_im_ref[...] = 2.0 * lr_all * li_all

    halves = SSM_COL_BLOCK // V7X_LANES
    for cb in range(SSM_N_BLOCKS):
        flat = slice(cb * SSM_STATE_BLOCK, (cb + 1) * SSM_STATE_BLOCK)
        lr, li = lb_re_row[:, flat], lb_im_row[:, flat]
        l2r, l2i = a2_re_ref[:, flat], a2_im_ref[:, flat]
        w_r, w_i = wb_re[cb], wb_im[cb]
        wl_r, wl_i = w_r * lr - w_i * li, w_r * li + w_i * lr
        c_r, c_n = wc_re[cb], wc_im[cb]
        k0 = _dot_nt_f32(w_r, c_r) + _dot_nt_f32(w_i, c_n)
        k1 = _dot_nt_f32(wl_r, c_r) + _dot_nt_f32(wl_i, c_n)
        o1_r, o1_i = c_r * lr + c_n * li, c_n * lr - c_r * li
        o2_r, o2_i = c_r * l2r + c_n * l2i, c_n * l2r - c_r * l2i
        for half in range(halves):
            j = cb * halves + half
            r = slice(half * V7X_LANES, (half + 1) * V7X_LANES)
            s = slice(half * SSM_LANE_STATES, (half + 1) * SSM_LANE_STATES)
            t0, t1 = slice(0, V7X_LANES), slice(V7X_LANES, SSM_PAIR_COLS)
            re, im = slice(0, SSM_LANE_STATES), slice(SSM_LANE_STATES, SSM_PAIR_STATES)
            in_ref[j, t0, re] = wl_r[r, s]
            in_ref[j, t0, im] = wl_i[r, s]
            in_ref[j, t1, re] = w_r[r, s]
            in_ref[j, t1, im] = w_i[r, s]
            direct_ref[j, t0, t0] = k0[r, r]
            direct_ref[j, t0, t1] = k1[r, r]
            direct_ref[j, t1, t0] = jnp.zeros((V7X_LANES, V7X_LANES), F32)
            direct_ref[j, t1, t1] = k0[r, r]
            out_ref[j, re, t0] = o1_r[r, s].T
            out_ref[j, im, t0] = o1_i[r, s].T
            out_ref[j, re, t1] = o2_r[r, s].T
            out_ref[j, im, t1] = o2_i[r, s].T


def _ssm_prep(lam_re, lam_im, log_dt, b_re, b_im, c_re, c_im):
    g, p, _ = b_re.shape
    row = jax.ShapeDtypeStruct((1, SSM_FLAT), F32)
    n = SSM_N_LANE_BLOCKS
    swap = lambda a: jnp.transpose(a, (0, 2, 1))
    block_diag = pltpu.VMEM((SSM_N_BLOCKS, SSM_COL_BLOCK, SSM_STATE_BLOCK), F32)
    return pl.pallas_call(
        _ssm_prep_kernel,
        out_shape=(row, row,
                   jax.ShapeDtypeStruct((n, SSM_PAIR_COLS, SSM_PAIR_STATES), F32),
                   jax.ShapeDtypeStruct((n, SSM_PAIR_STATES, SSM_PAIR_COLS), F32),
                   jax.ShapeDtypeStruct((n, SSM_PAIR_COLS, SSM_PAIR_COLS), F32)),
        scratch_shapes=[pltpu.VMEM((1, SSM_FLAT), F32), pltpu.VMEM((1, SSM_FLAT), F32),
                        block_diag, block_diag, block_diag, block_diag],
        compiler_params=pltpu.CompilerParams(vmem_limit_bytes=V7X_VMEM_LIMIT_BYTES),
        name="ssm_prep",
    )(lam_re, lam_im, log_dt.reshape(g, 1), swap(b_re), swap(b_im), c_re, c_im)


def _pool_pieces(x_ref, dst_ref, ext_ref, w_ref, scale_ref, g, *, tt, bb, first_pos):
    tm = tt * bb
    halo = POOL_BUF * bb
    dst_ref[...] = _time_major(x_ref[...])
    ext_ref[halo:halo + tm, :] = _rmsnorm(dst_ref[...], g)
    yield
    row = lax.broadcasted_iota(jnp.int32, (tm, 1), 0)
    pos = first_pos + lax.shift_right_logical(row, int(math.log2(bb)))
    for gi, w in enumerate(POOL_WINDOWS):
        cols = slice(gi * POOL_GROUP_DIM, (gi + 1) * POOL_GROUP_DIM)
        acc = ext_ref[halo:halo + tm, cols]
        for k in range(1, w):
            acc = acc + ext_ref[halo - k * bb:halo - k * bb + tm, cols]
        inv_cnt = 1.0 / jnp.minimum(w, pos + 1).astype(F32)
        diff = acc * inv_cnt - ext_ref[halo:halo + tm, cols]
        dst_ref[:, cols] += _dot(diff.astype(BF16), w_ref[gi]) * scale_ref[:, cols]
        yield


def _ffn_ple_pieces(rows_ref, h_in_ref, p_ref, w_gate_ref, w_up_ref, w_down_ref, g_ple, ple_in_ref, ple_gate_ref,
                    h_ref, pt_ref, act_ref, release_h_in, emit):
    d_ff = act_ref.shape[1]
    pt_ref[...] = _time_major(p_ref[...]).astype(BF16)
    for c0 in range(0, d_ff, FFN_COLS):
        cols = slice(c0, min(c0 + FFN_COLS, d_ff))
        gate = _dot(h_in_ref[...], w_gate_ref[:, cols])
        up = _dot(h_in_ref[...], w_up_ref[:, cols])
        act_ref[:, cols] = (jax.nn.silu(gate) * up).astype(BF16)
        if cols.stop == d_ff:
            release_h_in()
        yield
    for n0 in range(0, D_MODEL, OUT_COLS):
        cols = slice(n0, n0 + OUT_COLS)
        rows_ref[:, cols] += _dot(act_ref[...], w_down_ref[:, cols])
        yield
    h_ref[...] = _rmsnorm(rows_ref[...], g_ple).astype(BF16)
    for n0 in range(0, D_MODEL, OUT_COLS):
        cols = slice(n0, n0 + OUT_COLS)
        ple_gate = jax.nn.sigmoid(_dot(h_ref[...], ple_gate_ref[:, cols]))
        rows_ref[:, cols] += _dot(pt_ref[...], ple_in_ref[:, cols]) * ple_gate
        yield
    emit()
    yield


_DONE = object()


def _interleave(*stages):
    live = list(stages)
    while live:
        for stage in list(live):
            if next(stage, _DONE) is _DONE:
                live.remove(stage)


def _then(stage, last_piece):
    yield from stage
    last_piece()
    yield


def _run_halves(sched, g_ffn, mid_ref, hmid_ref, rows_ref, mixer, ffn):
    h_in_reads_issued = []

    def mixer_into(dst_ref):
        def norm_for_ffn():
            assert h_in_reads_issued or not sched.lag, "hmid_ref rewritten before the FFN half read it"
            hmid_ref[...] = _rmsnorm(dst_ref[...], g_ffn).astype(BF16)
        return _then(mixer(dst_ref), norm_for_ffn)

    def release_h_in():
        h_in_reads_issued.append(True)

    if sched.lag:
        rows_ref[...] = mid_ref[...]
        _interleave(ffn(hmid_ref, release_h_in), mixer_into(mid_ref))
    else:
        _interleave(mixer_into(rows_ref))
        _interleave(ffn(hmid_ref, release_h_in))


def _init_handoff(sched, mid_ref, hmid_ref):
    if sched.lag:
        mid_ref[...] = jnp.zeros(mid_ref.shape, mid_ref.dtype)
        hmid_ref[...] = jnp.zeros(hmid_ref.shape, hmid_ref.dtype)


def _ffn_scratch(tm, d_ff, ple_dim):
    return [
        pltpu.VMEM((tm, D_MODEL), BF16),
        pltpu.VMEM((tm, D_MODEL), F32),
        pltpu.VMEM((tm, D_MODEL), BF16),
        pltpu.VMEM((tm, ple_dim), BF16),
        pltpu.VMEM((tm, d_ff), BF16),
    ]


def _layer0_kernel(x_ref, prev_ref, p_ref, g_mix_ref, pool_w_ref, pool_scale_ref,
                   g_ffn_ref, w_gate_ref, w_up_ref, w_down_ref, g_ple_ref, ple_in_ref, ple_gate_ref,
                   o_ref, state_ref, ext_ref, mid_ref, hmid_ref, rows_ref, h_ref, pt_ref, act_ref,
                   *, sched, start):
    tt, bb = sched.tt, sched.bb
    tm = tt * bb
    halo = POOL_BUF * bb
    i = pl.program_id(1)

    @pl.when(i == 0)
    def _():
        ext_ref[0:halo, :] = _time_major(prev_ref[...])
        _init_handoff(sched, mid_ref, hmid_ref)

    @pl.when(i > 0)
    def _():
        ext_ref[0:halo, :] = ext_ref[tm:tm + halo, :]

    def mixer(dst_ref):
        return _pool_pieces(x_ref, dst_ref, ext_ref, pool_w_ref, pool_scale_ref, g_mix_ref[...],
                            tt=tt, bb=bb, first_pos=start + sched.mixer_block(i) * tt)

    def emit():
        o_ref[...] = rows_ref[...].reshape(tt, bb, D_MODEL)

    def ffn(h_in_ref, release_h_in):
        return _ffn_ple_pieces(rows_ref, h_in_ref, p_ref, w_gate_ref, w_up_ref, w_down_ref,
                               g_ple_ref[...], ple_in_ref, ple_gate_ref, h_ref, pt_ref, act_ref,
                               release_h_in, emit)

    _run_halves(sched, g_ffn_ref[...], mid_ref, hmid_ref, rows_ref, mixer, ffn)

    @pl.when(i == sched.nt - 1)
    def _():
        state_ref[...] = _batch_major(ext_ref[tm:tm + halo, :], POOL_BUF, bb)


def _ffn_weight_specs(prm, layer):
    return [
        _layer_spec((1, D_MODEL), layer),
        _layer_spec(prm['ffn_w_gate'].shape[1:], layer), _layer_spec(prm['ffn_w_up'].shape[1:], layer),
        _layer_spec(prm['ffn_w_down'].shape[1:], layer),
        _layer_spec((1, D_MODEL), layer),
        _layer_spec(prm['ple_w_in'].shape[1:], layer), _layer_spec(prm['ple_w_gate'].shape[1:], layer),
    ]


def _ffn_weights(prm):
    return (prm['g_ffn'], prm['ffn_w_gate'], prm['ffn_w_up'], prm['ffn_w_down'],
            prm['g_ple'], prm['ple_w_in'], prm['ple_w_gate'])


def _layer0(x, prev, p, prm, *, start):
    b_len, t_len, _ = x.shape
    sched = _Schedule(b_len, t_len, POOL_LAYER_ROWS)
    tt, bb = sched.tt, sched.bb
    tm = tt * bb
    state_spec = pl.BlockSpec((bb, POOL_BUF, D_MODEL), lambda j, i: (j, 0, 0))
    mid_rows = tm if sched.lag else 2 * V7X_SUBLANES
    return pl.pallas_call(
        functools.partial(_layer0_kernel, sched=sched, start=start),
        grid=sched.grid,
        in_specs=[
            pl.BlockSpec((bb, tt, D_MODEL), lambda j, i: (j, sched.mixer_block(i), 0)),
            state_spec,
            pl.BlockSpec((None, bb, tt, p.shape[-1]), lambda j, i: (0, j, sched.ffn_block(i), 0)),
            _layer_spec((1, D_MODEL), 0),
            _const_spec(prm['pool_w'].shape),
            _const_spec((1, D_MODEL)),
        ] + _ffn_weight_specs(prm, 0),
        out_specs=[pl.BlockSpec((tt, bb, D_MODEL), lambda j, i: (sched.ffn_block(i), j, 0)), state_spec],
        out_shape=[
            jax.ShapeDtypeStruct((t_len, b_len, D_MODEL), F32),
            jax.ShapeDtypeStruct((b_len, POOL_BUF, D_MODEL), F32),
        ],
        scratch_shapes=([pltpu.VMEM(((POOL_BUF + tt) * bb, D_MODEL), F32),
                         pltpu.VMEM((mid_rows, D_MODEL), F32)]
                        + _ffn_scratch(tm, prm['ffn_w_gate'].shape[-1], p.shape[-1])),
        compiler_params=_params(),
        name="layer0_pool_ffn",
    )(x, prev, p, prm['g_mix'], prm['pool_w'], prm['pool_scale'], *_ffn_weights(prm))


def _ssm_kernel(x_ref, h0_re_ref, h0_im_ref, g_ref, a2_re_ref, a2_im_ref, in_ref, out_ref, direct_ref,
                d_ref, wglu_ref, o_ref, s_re_ref, s_im_ref,
                st_re, st_im, u_ref, xj_ref, sbuf_ref, y_ref, z_ref, *, tt, bb):
    tm = tt * bb
    pairs = tt // 2
    m = pairs * bb
    i = pl.program_id(1)

    @pl.when(i == 0)
    def _():
        st_re[...] = h0_re_ref[...]
        st_im[...] = h0_im_ref[...]

    u_ref[...] = _rmsnorm(x_ref[...].reshape(tm, D_MODEL), g_ref[...]).reshape(pairs, 2, bb, D_MODEL)

    def contributions(j):
        cols = slice(j * V7X_LANES, (j + 1) * V7X_LANES)
        both = jnp.concatenate([u_ref[:, 0, :, cols].reshape(m, V7X_LANES),
                                u_ref[:, 1, :, cols].reshape(m, V7X_LANES)], axis=-1)
        xj_ref[j] = both.astype(BF16)
        sbuf_ref[:, j * SSM_PAIR_STATES:(j + 1) * SSM_PAIR_STATES] = _dot(xj_ref[j], in_ref[j])

    def scan(j):
        states = slice(j * SSM_LANE_STATES, (j + 1) * SSM_LANE_STATES)
        re = slice(j * SSM_PAIR_STATES, j * SSM_PAIR_STATES + SSM_LANE_STATES)
        im = slice(j * SSM_PAIR_STATES + SSM_LANE_STATES, (j + 1) * SSM_PAIR_STATES)
        ar = jnp.broadcast_to(a2_re_ref[:, states], (V7X_SUBLANES, SSM_LANE_STATES))
        ai = jnp.broadcast_to(a2_im_ref[:, states], (V7X_SUBLANES, SSM_LANE_STATES))
        for r0 in range(0, bb, V7X_SUBLANES):
            sr = st_re[r0:r0 + V7X_SUBLANES, states]
            si = st_im[r0:r0 + V7X_SUBLANES, states]
            for c in range(pairs):
                rows = slice(c * bb + r0, c * bb + r0 + V7X_SUBLANES)
                pr, pi = sbuf_ref[rows, re], sbuf_ref[rows, im]
                sbuf_ref[rows, re] = sr
                sbuf_ref[rows, im] = si
                sr, si = ar * sr - ai * si + pr, ar * si + ai * sr + pi
            st_re[r0:r0 + V7X_SUBLANES, states] = sr
            st_im[r0:r0 + V7X_SUBLANES, states] = si

    def outputs(j):
        cols = slice(j * V7X_LANES, (j + 1) * V7X_LANES)
        entering = sbuf_ref[:, j * SSM_PAIR_STATES:(j + 1) * SSM_PAIR_STATES].astype(BF16)
        y = _dot(entering, out_ref[j]) + _dot(xj_ref[j], direct_ref[j])
        y_ref[:, 0, :, cols] = y[:, :V7X_LANES].reshape(pairs, bb, V7X_LANES)
        y_ref[:, 1, :, cols] = y[:, V7X_LANES:].reshape(pairs, bb, V7X_LANES)

    for k in range(SSM_N_LANE_BLOCKS + 2):
        if 0 <= k - 2:
            outputs(k - 2)
        if k < SSM_N_LANE_BLOCKS:
            contributions(k)
        if 0 <= k - 1 < SSM_N_LANE_BLOCKS:
            scan(k - 1)

    y = y_ref[...].reshape(tm, D_MODEL) + d_ref[...] * u_ref[...].reshape(tm, D_MODEL)
    z_ref[...] = jax.nn.gelu(y).astype(BF16)
    for n0 in range(0, D_MODEL, OUT_COLS):
        cols = slice(n0, n0 + OUT_COLS)
        a = _dot(z_ref[...], wglu_ref[:, cols])
        gate = _dot(z_ref[...], wglu_ref[:, D_MODEL + n0:D_MODEL + n0 + OUT_COLS])
        o_ref[:, :, cols] = x_ref[:, :, cols] + (a * jax.nn.sigmoid(gate)).reshape(tt, bb, OUT_COLS)

    s_re_ref[...] = st_re[...]
    s_im_ref[...] = st_im[...]


def _ssm_layer(x, h0_re, h0_im, prm):
    t_len, b_len, _ = x.shape
    tt, bb = _tiles(b_len, t_len, SSM_ROWS)
    assert tt % 2 == 0
    tm = tt * bb
    m = tm // 2
    x_spec = pl.BlockSpec((tt, bb, D_MODEL), lambda j, i: (i, j, 0))
    st_spec = pl.BlockSpec((bb, SSM_FLAT), lambda j, i: (j, 0))
    return pl.pallas_call(
        functools.partial(_ssm_kernel, tt=tt, bb=bb),
        grid=(b_len // bb, t_len // tt),
        in_specs=[
            x_spec, st_spec, st_spec,
            _layer_spec((1, D_MODEL), 1),
            _const_spec((1, SSM_FLAT)), _const_spec((1, SSM_FLAT)),
            _const_spec(prm['ssm_in'].shape), _const_spec(prm['ssm_out'].shape),
            _const_spec(prm['ssm_direct'].shape),
            _const_spec((1, D_MODEL)),
            _const_spec(prm['ssm_w_glu'].shape),
        ],
        out_specs=[x_spec, st_spec, st_spec],
        out_shape=[
            jax.ShapeDtypeStruct(x.shape, F32),
            jax.ShapeDtypeStruct((b_len, SSM_FLAT), F32),
            jax.ShapeDtypeStruct((b_len, SSM_FLAT), F32),
        ],
        scratch_shapes=[
            pltpu.VMEM((bb, SSM_FLAT), F32), pltpu.VMEM((bb, SSM_FLAT), F32),
            pltpu.VMEM((tt // 2, 2, bb, D_MODEL), F32),
            pltpu.VMEM((SSM_N_LANE_BLOCKS, m, SSM_PAIR_COLS), BF16),
            pltpu.VMEM((m, SSM_N_LANE_BLOCKS * SSM_PAIR_STATES), F32),
            pltpu.VMEM((tt // 2, 2, bb, D_MODEL), F32),
            pltpu.VMEM((tm, D_MODEL), BF16),
        ],
        compiler_params=_params(),
        name="ssm_mixer",
    )(x, h0_re, h0_im, prm['g_mix'], prm['a2_re'], prm['a2_im'],
      prm['ssm_in'], prm['ssm_out'], prm['ssm_direct'], prm['ssm_d'], prm['ssm_w_glu'])


def _ffn1_kernel(x_ref, p_ref, g_ffn_ref, w_gate_ref, w_up_ref, w_down_ref, g_ple_ref, ple_in_ref, ple_gate_ref,
                 g_final_ref, o_ref, h_in_ref, rows_ref, h_ref, pt_ref, act_ref, *, tt, bb):
    rows_ref[...] = x_ref[...].reshape(tt * bb, D_MODEL)
    h_in_ref[...] = _rmsnorm(rows_ref[...], g_ffn_ref[...]).astype(BF16)

    def emit():
        o_ref[...] = _batch_major(_rmsnorm(rows_ref[...], g_final_ref[...]), tt, bb)

    _interleave(_ffn_ple_pieces(rows_ref, h_in_ref, p_ref, w_gate_ref, w_up_ref, w_down_ref,
                                g_ple_ref[...], ple_in_ref, ple_gate_ref, h_ref, pt_ref, act_ref,
                                lambda: None, emit))


def _ffn1_layer(x, p, prm):
    t_len, b_len, _ = x.shape
    tt, bb = _tiles(b_len, t_len, FFN_ROWS)
    return pl.pallas_call(
        functools.partial(_ffn1_kernel, tt=tt, bb=bb),
        grid=(b_len // bb, t_len // tt),
        in_specs=[
            pl.BlockSpec((tt, bb, D_MODEL), lambda j, i: (i, j, 0)),
            pl.BlockSpec((None, bb, tt, p.shape[-1]), lambda j, i: (1, j, i, 0)),
        ] + _ffn_weight_specs(prm, 1) + [_const_spec((1, D_MODEL))],
        out_specs=pl.BlockSpec((bb, tt, D_MODEL), lambda j, i: (j, i, 0)),
        out_shape=jax.ShapeDtypeStruct((b_len, t_len, D_MODEL), F32),
        scratch_shapes=_ffn_scratch(tt * bb, prm['ffn_w_gate'].shape[-1], p.shape[-1]),
        compiler_params=_params(),
        name="layer1_ffn",
    )(x, p, *_ffn_weights(prm), prm['g_final'])


def _run_trunk(x, p, start, pool_state, ssm_re, ssm_im, prm):
    b_len = x.shape[0]
    xt, pool_new = _layer0(x, pool_state, p, prm, start=start)
    xt, s_re, s_im = _ssm_layer(xt, ssm_re.reshape(b_len, SSM_FLAT), ssm_im.reshape(b_len, SSM_FLAT), prm)
    y = _ffn1_layer(xt, p, prm)
    s_re = s_re.reshape(1, b_len, SSM_GROUPS, SSM_STATE)
    s_im = s_im.reshape(1, b_len, SSM_GROUPS, SSM_STATE)
    return y, pool_new[None], s_re, s_im


def kernel(x_prompt, x_sample, state_pool, state_ssm_re, state_ssm_im, p_prompt, p_sample, g_mix, g_ffn, g_ple, g_final, pool_w, pool_scale, ssm_lambda_re, ssm_lambda_im, ssm_log_dt, ssm_b_re, ssm_b_im, ssm_c_re, ssm_c_im, ssm_d, ssm_w_glu, ffn_w_gate, ffn_w_up, ffn_w_down, ple_w_in, ple_w_gate):
    depth = g_mix.shape[0]
    a2_re, a2_im, ssm_in, ssm_out, ssm_direct = _ssm_prep(
        ssm_lambda_re[0], ssm_lambda_im[0], ssm_log_dt[0], ssm_b_re[0], ssm_b_im[0], ssm_c_re[0], ssm_c_im[0])
    per_layer_vec = lambda a: a.reshape(depth, 1, D_MODEL)
    prm = dict(
        g_mix=per_layer_vec(g_mix), g_ffn=per_layer_vec(g_ffn), g_ple=per_layer_vec(g_ple),
        g_final=g_final.reshape(1, D_MODEL),
        pool_w=pool_w[0].astype(BF16), pool_scale=pool_scale,
        a2_re=a2_re, a2_im=a2_im,
        ssm_in=ssm_in.astype(BF16), ssm_out=ssm_out.astype(BF16), ssm_direct=ssm_direct.astype(BF16),
        ssm_d=ssm_d, ssm_w_glu=ssm_w_glu[0].astype(BF16),
        ffn_w_gate=ffn_w_gate.astype(BF16), ffn_w_up=ffn_w_up.astype(BF16),
        ffn_w_down=ffn_w_down.astype(BF16),
        ple_w_in=ple_w_in.astype(BF16), ple_w_gate=ple_w_gate.astype(BF16),
    )
    b_p = x_prompt.shape[0]
    zeros_pool = jnp.zeros((b_p, POOL_BUF, D_MODEL), F32)
    zeros_ssm = jnp.zeros((b_p, SSM_GROUPS, SSM_STATE), F32)
    y_p, pool_p, re_p, im_p = _run_trunk(x_prompt, p_prompt, 0, zeros_pool, zeros_ssm, zeros_ssm, prm)
    y_s, pool_s, re_s, im_s = _run_trunk(x_sample, p_sample, PAST_LEN, state_pool[0],
                                         state_ssm_re[0], state_ssm_im[0], prm)
    return (y_p, y_s, pool_p, pool_s, re_p, im_p, re_s, im_s)
```

```python
import functools
import math

import jax
import jax.numpy as jnp
from jax import lax
from jax.experimental import pallas as pl
from jax.experimental.pallas import tpu as pltpu

D_MODEL = 1024
POOL_WINDOWS = (2, 4, 8, 16)
POOL_GROUP_DIM = D_MODEL // len(POOL_WINDOWS)
POOL_BUF = max(POOL_WINDOWS) - 1
SSM_GROUP_DIM = 16
SSM_GROUPS = D_MODEL // SSM_GROUP_DIM
SSM_STATE = 64
SSM_FLAT = SSM_GROUPS * SSM_STATE
PAST_LEN = 16384
EPS = 1e-6

V7X_SUBLANES = 8
V7X_LANES = 128
V7X_MXU_DIM = 256
V7X_VMEM_LIMIT_BYTES = 60 * 1024 * 1024

SSM_COL_BLOCK = V7X_MXU_DIM
SSM_GROUPS_PER_BLOCK = SSM_COL_BLOCK // SSM_GROUP_DIM
SSM_STATE_BLOCK = SSM_GROUPS_PER_BLOCK * SSM_STATE
SSM_N_BLOCKS = D_MODEL // SSM_COL_BLOCK
SSM_LANE_GROUPS = V7X_LANES // SSM_GROUP_DIM
SSM_LANE_STATES = SSM_LANE_GROUPS * SSM_STATE
SSM_N_LANE_BLOCKS = D_MODEL // V7X_LANES
SSM_PAIR_COLS = 2 * V7X_LANES
SSM_PAIR_STATES = 2 * SSM_LANE_STATES

POOL_LAYER_ROWS = 512
SSM_ROWS = 512
FFN_ROWS = 512
FFN_COLS = 2 * V7X_MXU_DIM
OUT_COLS = 2 * V7X_MXU_DIM

BF16 = jnp.bfloat16
F32 = jnp.float32


def _rmsnorm(x, g):
    return x * lax.rsqrt(jnp.mean(x * x, axis=-1, keepdims=True) + EPS) * g


def _dot(a, b):
    return jnp.dot(a, b, preferred_element_type=F32)


def _dot_nt_f32(a, b):
    return lax.dot_general(a, b, (((1,), (1,)), ((), ())), precision=lax.Precision.HIGHEST,
                           preferred_element_type=F32)


def _time_major(block):
    bb, tt, c = block.shape
    return jnp.transpose(block, (1, 0, 2)).reshape(tt * bb, c)


def _batch_major(rows, tt, bb):
    return jnp.transpose(rows.reshape(tt, bb, rows.shape[-1]), (1, 0, 2))


def _const_spec(shape):
    zeros = (0,) * len(shape)
    return pl.BlockSpec(shape, lambda *_: zeros, pipeline_mode=pl.Buffered(1))


def _layer_spec(shape, layer):
    tail = (0,) * len(shape)
    return pl.BlockSpec((None,) + tuple(shape), lambda *_: (layer,) + tail,
                        pipeline_mode=pl.Buffered(1))


def _params():
    return pltpu.CompilerParams(dimension_semantics=("arbitrary", "arbitrary"),
                                vmem_limit_bytes=V7X_VMEM_LIMIT_BYTES)


def _tiles(b_len, t_len, rows):
    bb = min(b_len, max(V7X_SUBLANES, rows // t_len))
    tt = min(t_len, rows // bb)
    assert t_len % tt == 0 and b_len % bb == 0 and bb % V7X_SUBLANES == 0
    return tt, bb


class _Schedule:
    def __init__(self, b_len, t_len, rows):
        self.tt, self.bb = _tiles(b_len, t_len, rows)
        self.nt = t_len // self.tt
        self.lag = 1 if self.nt > 1 else 0
        self.grid = (b_len // self.bb, self.nt + self.lag)

    def mixer_block(self, i):
        return jnp.minimum(i, self.nt - 1)

    def ffn_block(self, i):
        return jnp.maximum(i - self.lag, 0)


def _ssm_prep_kernel(lam_re_ref, lam_im_ref, log_dt_ref, b_re_ref, b_im_ref, c_re_ref, c_im_ref,
                     a2_re_ref, a2_im_ref, in_ref, out_ref, direct_ref,
                     lb_re_row, lb_im_row, wb_re, wb_im, wc_re, wc_im):
    lam_re = lam_re_ref[...]
    lam_im = lam_im_ref[...]
    dt = jnp.exp(log_dt_ref[...])
    mag = jnp.exp(lam_re * dt)
    lb_re = mag * jnp.cos(lam_im * dt)
    lb_im = mag * jnp.sin(lam_im * dt)
    den = lam_re * lam_re + lam_im * lam_im
    f_re = ((lb_re - 1.0) * lam_re + lb_im * lam_im) / den
    f_im = (lb_im * lam_re - (lb_re - 1.0) * lam_im) / den
    b_re = b_re_ref[...]
    b_im = b_im_ref[...]
    fr = f_re[:, None, :]
    fi = f_im[:, None, :]
    bb_re = fr * b_re - fi * b_im
    bb_im = fr * b_im + fi * b_re
    c_re = c_re_ref[...]
    c_im_neg = -c_im_ref[...]

    for ref in (wb_re, wb_im, wc_re, wc_im):
        ref[...] = jnp.zeros(ref.shape, ref.dtype)
    for g in range(SSM_GROUPS):
        cb, gl = divmod(g, SSM_GROUPS_PER_BLOCK)
        cols = slice(gl * SSM_GROUP_DIM, (gl + 1) * SSM_GROUP_DIM)
        states = slice(gl * SSM_STATE, (gl + 1) * SSM_STATE)
        wb_re[cb, cols, states] = bb_re[g]
        wb_im[cb, cols, states] = bb_im[g]
        wc_re[cb, cols, states] = c_re[g]
        wc_im[cb, cols, states] = c_im_neg[g]
        flat = slice(g * SSM_STATE, (g + 1) * SSM_STATE)
        lb_re_row[:, flat] = lb_re[g:g + 1, :]
        lb_im_row[:, flat] = lb_im[g:g + 1, :]

    lr_all = lb_re_row[...]
    li_all = lb_im_row[...]
    a2_re_ref[...] = lr_all * lr_all - li_all * li_all
    a2_im_ref[...] = 2.0 * lr_all * li_all

    halves = SSM_COL_BLOCK // V7X_LANES
    for cb in range(SSM_N_BLOCKS):
        flat = slice(cb * SSM_STATE_BLOCK, (cb + 1) * SSM_STATE_BLOCK)
        lr, li = lb_re_row[:, flat], lb_im_row[:, flat]
        l2r, l2i = a2_re_ref[:, flat], a2_im_ref[:, flat]
        w_r, w_i = wb_re[cb], wb_im[cb]
        wl_r, wl_i = w_r * lr - w_i * li, w_r * li + w_i * lr
        c_r, c_n = wc_re[cb], wc_im[cb]
        k0 = _dot_nt_f32(w_r, c_r) + _dot_nt_f32(w_i, c_n)
        k1 = _dot_nt_f32(wl_r, c_r) + _dot_nt_f32(wl_i, c_n)
        o1_r, o1_i = c_r * lr + c_n * li, c_n * lr - c_r * li
        o2_r, o2_i = c_r * l2r + c_n * l2i, c_n * l2r - c_r * l2i
        for half in range(halves):
            j = cb * halves + half
            r = slice(half * V7X_LANES, (half + 1) * V7X_LANES)
            s = slice(half * SSM_LANE_STATES, (half + 1) * SSM_LANE_STATES)
            t0, t1 = slice(0, V7X_LANES), slice(V7X_LANES, SSM_PAIR_COLS)
            re, im = slice(0, SSM_LANE_STATES), slice(SSM_LANE_STATES, SSM_PAIR_STATES)
            in_ref[j, t0, re] = wl_r[r, s]
            in_ref[j, t0, im] = wl_i[r, s]
            in_ref[j, t1, re] = w_r[r, s]
            in_ref[j, t1, im] = w_i[r, s]
            direct_ref[j, t0, t0] = k0[r, r]
            direct_ref[j, t0, t1] = k1[r, r]
            direct_ref[j, t1, t0] = jnp.zeros((V7X_LANES, V7X_LANES), F32)
            direct_ref[j, t1, t1] = k0[r, r]
            out_ref[j, re, t0] = o1_r[r, s].T
            out_ref[j, im, t0] = o1_i[r, s].T
            out_ref[j, re, t1] = o2_r[r, s].T
            out_ref[j, im, t1] = o2_i[r, s].T


def _ssm_prep(lam_re, lam_im, log_dt, b_re, b_im, c_re, c_im):
    g, p, _ = b_re.shape
    row = jax.ShapeDtypeStruct((1, SSM_FLAT), F32)
    n = SSM_N_LANE_BLOCKS
    swap = lambda a: jnp.transpose(a, (0, 2, 1))
    block_diag = pltpu.VMEM((SSM_N_BLOCKS, SSM_COL_BLOCK, SSM_STATE_BLOCK), F32)
    return pl.pallas_call(
        _ssm_prep_kernel,
        out_shape=(row, row,
                   jax.ShapeDtypeStruct((n, SSM_PAIR_COLS, SSM_PAIR_STATES), F32),
                   jax.ShapeDtypeStruct((n, SSM_PAIR_STATES, SSM_PAIR_COLS), F32),
                   jax.ShapeDtypeStruct((n, SSM_PAIR_COLS, SSM_PAIR_COLS), F32)),
        scratch_shapes=[pltpu.VMEM((1, SSM_FLAT), F32), pltpu.VMEM((1, SSM_FLAT), F32),
                        block_diag, block_diag, block_diag, block_diag],
        compiler_params=pltpu.CompilerParams(vmem_limit_bytes=V7X_VMEM_LIMIT_BYTES),
        name="ssm_prep",
    )(lam_re, lam_im, log_dt.reshape(g, 1), swap(b_re), swap(b_im), c_re, c_im)


def _pool_pieces(x_ref, dst_ref, ext_ref, w_ref, scale_ref, g, *, tt, bb, first_pos):
    tm = tt * bb
    halo = POOL_BUF * bb
    dst_ref[...] = _time_major(x_ref[...])
    ext_ref[halo:halo + tm, :] = _rmsnorm(dst_ref[...], g)
    yield
    row = lax.broadcasted_iota(jnp.int32, (tm, 1), 0)
    pos = first_pos + lax.shift_right_logical(row, int(math.log2(bb)))
    for gi, w in enumerate(POOL_WINDOWS):
        cols = slice(gi * POOL_GROUP_DIM, (gi + 1) * POOL_GROUP_DIM)
        acc = ext_ref[halo:halo + tm, cols]
        for k in range(1, w):
            acc = acc + ext_ref[halo - k * bb:halo - k * bb + tm, cols]
        inv_cnt = 1.0 / jnp.minimum(w, pos + 1).astype(F32)
        diff = acc * inv_cnt - ext_ref[halo:halo + tm, cols]
        dst_ref[:, cols] += _dot(diff.astype(BF16), w_ref[gi]) * scale_ref[:, cols]
        yield


def _ffn_ple_pieces(rows_ref, h_in_ref, p_ref, w_gate_ref, w_up_ref, w_down_ref, g_ple, ple_in_ref, ple_gate_ref,
                    h_ref, pt_ref, act_ref, release_h_in, emit):
    d_ff = act_ref.shape[1]
    pt_ref[...] = _time_major(p_ref[...]).astype(BF16)
    for c0 in range(0, d_ff, FFN_COLS):
        cols = slice(c0, min(c0 + FFN_COLS, d_ff))
        gate = _dot(h_in_ref[...], w_gate_ref[:, cols])
        up = _dot(h_in_ref[...], w_up_ref[:, cols])
        act_ref[:, cols] = (jax.nn.silu(gate) * up).astype(BF16)
        if cols.stop == d_ff:
            release_h_in()
        yield
    for n0 in range(0, D_MODEL, OUT_COLS):
        cols = slice(n0, n0 + OUT_COLS)
        rows_ref[:, cols] += _dot(act_ref[...], w_down_ref[:, cols])
        yield
    h_ref[...] = _rmsnorm(rows_ref[...], g_ple).astype(BF16)
    for n0 in range(0, D_MODEL, OUT_COLS):
        cols = slice(n0, n0 + OUT_COLS)
        ple_gate = jax.nn.sigmoid(_dot(h_ref[...], ple_gate_ref[:, cols]))
        rows_ref[:, cols] += _dot(pt_ref[...], ple_in_ref[:, cols]) * ple_gate
        yield
    emit()
    yield


_DONE = object()


def _interleave(*stages):
    live = list(stages)
    while live:
        for stage in list(live):
            if next(stage, _DONE) is _DONE:
                live.remove(stage)


def _then(stage, last_piece):
    yield from stage
    last_piece()
    yield


def _run_halves(sched, g_ffn, mid_ref, hmid_ref, rows_ref, mixer, ffn):
    h_in_reads_issued = []

    def mixer_into(dst_ref):
        def norm_for_ffn():
            assert h_in_reads_issued or not sched.lag, "hmid_ref rewritten before the FFN half read it"
            hmid_ref[...] = _rmsnorm(dst_ref[...], g_ffn).astype(BF16)
        return _then(mixer(dst_ref), norm_for_ffn)

    def release_h_in():
        h_in_reads_issued.append(True)

    if sched.lag:
        rows_ref[...] = mid_ref[...]
        _interleave(ffn(hmid_ref, release_h_in), mixer_into(mid_ref))
    else:
        _interleave(mixer_into(rows_ref))
        _interleave(ffn(hmid_ref, release_h_in))


def _init_handoff(sched, mid_ref, hmid_ref):
    if sched.lag:
        mid_ref[...] = jnp.zeros(mid_ref.shape, mid_ref.dtype)
        hmid_ref[...] = jnp.zeros(hmid_ref.shape, hmid_ref.dtype)


def _ffn_scratch(tm, d_ff, ple_dim):
    return [
        pltpu.VMEM((tm, D_MODEL), BF16),
        pltpu.VMEM((tm, D_MODEL), F32),
        pltpu.VMEM((tm, D_MODEL), BF16),
        pltpu.VMEM((tm, ple_dim), BF16),
        pltpu.VMEM((tm, d_ff), BF16),
    ]


def _layer0_kernel(x_ref, prev_ref, p_ref, g_mix_ref, pool_w_ref, pool_scale_ref,
                   g_ffn_ref, w_gate_ref, w_up_ref, w_down_ref, g_ple_ref, ple_in_ref, ple_gate_ref,
                   o_ref, state_ref, ext_ref, mid_ref, hmid_ref, rows_ref, h_ref, pt_ref, act_ref,
                   *, sched, start):
    tt, bb = sched.tt, sched.bb
    tm = tt * bb
    halo = POOL_BUF * bb
    i = pl.program_id(1)

    @pl.when(i == 0)
    def _():
        ext_ref[0:halo, :] = _time_major(prev_ref[...])
        _init_handoff(sched, mid_ref, hmid_ref)

    @pl.when(i > 0)
    def _():
        ext_ref[0:halo, :] = ext_ref[tm:tm + halo, :]

    def mixer(dst_ref):
        return _pool_pieces(x_ref, dst_ref, ext_ref, pool_w_ref, pool_scale_ref, g_mix_ref[...],
                            tt=tt, bb=bb, first_pos=start + sched.mixer_block(i) * tt)

    def emit():
        o_ref[...] = rows_ref[...].reshape(tt, bb, D_MODEL)

    def ffn(h_in_ref, release_h_in):
        return _ffn_ple_pieces(rows_ref, h_in_ref, p_ref, w_gate_ref, w_up_ref, w_down_ref,
                               g_ple_ref[...], ple_in_ref, ple_gate_ref, h_ref, pt_ref, act_ref,
                               release_h_in, emit)

    _run_halves(sched, g_ffn_ref[...], mid_ref, hmid_ref, rows_ref, mixer, ffn)

    @pl.when(i == sched.nt - 1)
    def _():
        state_ref[...] = _batch_major(ext_ref[tm:tm + halo, :], POOL_BUF, bb)


def _ffn_weight_specs(prm, layer):
    return [
        _layer_spec((1, D_MODEL), layer),
        _layer_spec(prm['ffn_w_gate'].shape[1:], layer), _layer_spec(prm['ffn_w_up'].shape[1:], layer),
        _layer_spec(prm['ffn_w_down'].shape[1:], layer),
        _layer_spec((1, D_MODEL), layer),
        _layer_spec(prm['ple_w_in'].shape[1:], layer), _layer_spec(prm['ple_w_gate'].shape[1:], layer),
    ]


def _ffn_weights(prm):
    return (prm['g_ffn'], prm['ffn_w_gate'], prm['ffn_w_up'], prm['ffn_w_down'],
            prm['g_ple'], prm['ple_w_in'], prm['ple_w_gate'])


def _layer0(x, prev, p, prm, *, start):
    b_len, t_len, _ = x.shape
    sched = _Schedule(b_len, t_len, POOL_LAYER_ROWS)
    tt, bb = sched.tt, sched.bb
    tm = tt * bb
    state_spec = pl.BlockSpec((bb, POOL_BUF, D_MODEL), lambda j, i: (j, 0, 0))
    mid_rows = tm if sched.lag else 2 * V7X_SUBLANES
    return pl.pallas_call(
        functools.partial(_layer0_kernel, sched=sched, start=start),
        grid=sched.grid,
        in_specs=[
            pl.BlockSpec((bb, tt, D_MODEL), lambda j, i: (j, sched.mixer_block(i), 0)),
            state_spec,
            pl.BlockSpec((None, bb, tt, p.shape[-1]), lambda j, i: (0, j, sched.ffn_block(i), 0)),
            _layer_spec((1, D_MODEL), 0),
            _const_spec(prm['pool_w'].shape),
            _const_spec((1, D_MODEL)),
        ] + _ffn_weight_specs(prm, 0),
        out_specs=[pl.BlockSpec((tt, bb, D_MODEL), lambda j, i: (sched.ffn_block(i), j, 0)), state_spec],
        out_shape=[
            jax.ShapeDtypeStruct((t_len, b_len, D_MODEL), F32),
            jax.ShapeDtypeStruct((b_len, POOL_BUF, D_MODEL), F32),
        ],
        scratch_shapes=([pltpu.VMEM(((POOL_BUF + tt) * bb, D_MODEL), F32),
                         pltpu.VMEM((mid_rows, D_MODEL), F32)]
                        + _ffn_scratch(tm, prm['ffn_w_gate'].shape[-1], p.shape[-1])),
        compiler_params=_params(),
        name="layer0_pool_ffn",
    )(x, prev, p, prm['g_mix'], prm['pool_w'], prm['pool_scale'], *_ffn_weights(prm))


def _ssm_kernel(x_ref, h0_re_ref, h0_im_ref, g_ref, a2_re_ref, a2_im_ref, in_ref, out_ref, direct_ref,
                d_ref, wglu_ref, o_ref, s_re_ref, s_im_ref,
                st_re, st_im, u_ref, xj_ref, sbuf_ref, z_ref, *, pairs, bb):
    m = pairs * bb
    i = pl.program_id(1)

    @pl.when(i == 0)
    def _():
        st_re[...] = h0_re_ref[...]
        st_im[...] = h0_im_ref[...]

    for t in range(2):
        u_ref[t] = _rmsnorm(x_ref[:, t].reshape(m, D_MODEL), g_ref[...])

    def contributions(j):
        cols = slice(j * V7X_LANES, (j + 1) * V7X_LANES)
        both = jnp.concatenate([u_ref[0, :, cols], u_ref[1, :, cols]], axis=-1)
        xj_ref[j] = both.astype(BF16)
        sbuf_ref[:, j * SSM_PAIR_STATES:(j + 1) * SSM_PAIR_STATES] = _dot(xj_ref[j], in_ref[j])

    def scan(j):
        states = slice(j * SSM_LANE_STATES, (j + 1) * SSM_LANE_STATES)
        re = slice(j * SSM_PAIR_STATES, j * SSM_PAIR_STATES + SSM_LANE_STATES)
        im = slice(j * SSM_PAIR_STATES + SSM_LANE_STATES, (j + 1) * SSM_PAIR_STATES)
        ar = jnp.broadcast_to(a2_re_ref[:, states], (V7X_SUBLANES, SSM_LANE_STATES))
        ai = jnp.broadcast_to(a2_im_ref[:, states], (V7X_SUBLANES, SSM_LANE_STATES))
        for r0 in range(0, bb, V7X_SUBLANES):
            sr = st_re[r0:r0 + V7X_SUBLANES, states]
            si = st_im[r0:r0 + V7X_SUBLANES, states]
            for c in range(pairs):
                rows = slice(c * bb + r0, c * bb + r0 + V7X_SUBLANES)
                pr, pi = sbuf_ref[rows, re], sbuf_ref[rows, im]
                sbuf_ref[rows, re] = sr
                sbuf_ref[rows, im] = si
                sr, si = ar * sr - ai * si + pr, ar * si + ai * sr + pi
            st_re[r0:r0 + V7X_SUBLANES, states] = sr
            st_im[r0:r0 + V7X_SUBLANES, states] = si

    def outputs(j):
        cols = slice(j * V7X_LANES, (j + 1) * V7X_LANES)
        entering = sbuf_ref[:, j * SSM_PAIR_STATES:(j + 1) * SSM_PAIR_STATES].astype(BF16)
        y = _dot(entering, out_ref[j]) + _dot(xj_ref[j], direct_ref[j])
        for t in range(2):
            yt = y[:, t * V7X_LANES:(t + 1) * V7X_LANES] + d_ref[:, cols] * u_ref[t, :, cols]
            z_ref[t, :, cols] = jax.nn.gelu(yt).astype(BF16)

    for k in range(SSM_N_LANE_BLOCKS + 2):
        if 0 <= k - 2:
            outputs(k - 2)
        if k < SSM_N_LANE_BLOCKS:
            contributions(k)
        if 0 <= k - 1 < SSM_N_LANE_BLOCKS:
            scan(k - 1)

    for t in range(2):
        for n0 in range(0, D_MODEL, OUT_COLS):
            cols = slice(n0, n0 + OUT_COLS)
            a = _dot(z_ref[t], wglu_ref[:, cols])
            gate = _dot(z_ref[t], wglu_ref[:, D_MODEL + n0:D_MODEL + n0 + OUT_COLS])
            o_ref[:, t, :, cols] = x_ref[:, t, :, cols] + (a * jax.nn.sigmoid(gate)).reshape(pairs, bb, OUT_COLS)

    s_re_ref[...] = st_re[...]
    s_im_ref[...] = st_im[...]


def _ssm_layer(x, h0_re, h0_im, prm):
    t_len, b_len, _ = x.shape
    tt, bb = _tiles(b_len, t_len, SSM_ROWS)
    assert tt % 2 == 0
    pairs = tt // 2
    m = pairs * bb
    x_spec = pl.BlockSpec((pairs, 2, bb, D_MODEL), lambda j, i: (i, 0, j, 0))
    st_spec = pl.BlockSpec((bb, SSM_FLAT), lambda j, i: (j, 0))
    out, s_re, s_im = pl.pallas_call(
        functools.partial(_ssm_kernel, pairs=pairs, bb=bb),
        grid=(b_len // bb, t_len // tt),
        in_specs=[
            x_spec, st_spec, st_spec,
            _layer_spec((1, D_MODEL), 1),
            _const_spec((1, SSM_FLAT)), _const_spec((1, SSM_FLAT)),
            _const_spec(prm['ssm_in'].shape), _const_spec(prm['ssm_out'].shape),
            _const_spec(prm['ssm_direct'].shape),
            _const_spec((1, D_MODEL)),
            _const_spec(prm['ssm_w_glu'].shape),
        ],
        out_specs=[x_spec, st_spec, st_spec],
        out_shape=[
            jax.ShapeDtypeStruct((t_len // 2, 2, b_len, D_MODEL), F32),
            jax.ShapeDtypeStruct((b_len, SSM_FLAT), F32),
            jax.ShapeDtypeStruct((b_len, SSM_FLAT), F32),
        ],
        scratch_shapes=[
            pltpu.VMEM((bb, SSM_FLAT), F32), pltpu.VMEM((bb, SSM_FLAT), F32),
            pltpu.VMEM((2, m, D_MODEL), F32),
            pltpu.VMEM((SSM_N_LANE_BLOCKS, m, SSM_PAIR_COLS), BF16),
            pltpu.VMEM((m, SSM_N_LANE_BLOCKS * SSM_PAIR_STATES), F32),
            pltpu.VMEM((2, m, D_MODEL), BF16),
        ],
        compiler_params=_params(),
        name="ssm_mixer",
    )(x.reshape(t_len // 2, 2, b_len, D_MODEL), h0_re, h0_im, prm['g_mix'], prm['a2_re'], prm['a2_im'],
      prm['ssm_in'], prm['ssm_out'], prm['ssm_direct'], prm['ssm_d'], prm['ssm_w_glu'])
    return out.reshape(t_len, b_len, D_MODEL), s_re, s_im


def _ffn1_kernel(x_ref, p_ref, g_ffn_ref, w_gate_ref, w_up_ref, w_down_ref, g_ple_ref, ple_in_ref, ple_gate_ref,
                 g_final_ref, o_ref, h_in_ref, rows_ref, h_ref, pt_ref, act_ref, *, tt, bb):
    rows_ref[...] = x_ref[...].reshape(tt * bb, D_MODEL)
    h_in_ref[...] = _rmsnorm(rows_ref[...], g_ffn_ref[...]).astype(BF16)

    def emit():
        o_ref[...] = _batch_major(_rmsnorm(rows_ref[...], g_final_ref[...]), tt, bb)

    _interleave(_ffn_ple_pieces(rows_ref, h_in_ref, p_ref, w_gate_ref, w_up_ref, w_down_ref,
                                g_ple_ref[...], ple_in_ref, ple_gate_ref, h_ref, pt_ref, act_ref,
                                lambda: None, emit))


def _ffn1_layer(x, p, prm):
    t_len, b_len, _ = x.shape
    tt, bb = _tiles(b_len, t_len, FFN_ROWS)
    return pl.pallas_call(
        functools.partial(_ffn1_kernel, tt=tt, bb=bb),
        grid=(b_len // bb, t_len // tt),
        in_specs=[
            pl.BlockSpec((tt, bb, D_MODEL), lambda j, i: (i, j, 0)),
            pl.BlockSpec((None, bb, tt, p.shape[-1]), lambda j, i: (1, j, i, 0)),
        ] + _ffn_weight_specs(prm, 1) + [_const_spec((1, D_MODEL))],
        out_specs=pl.BlockSpec((bb, tt, D_MODEL), lambda j, i: (j, i, 0)),
        out_shape=jax.ShapeDtypeStruct((b_len, t_len, D_MODEL), F32),
        scratch_shapes=_ffn_scratch(tt * bb, prm['ffn_w_gate'].shape[-1], p.shape[-1]),
        compiler_params=_params(),
        name="layer1_ffn",
    )(x, p, *_ffn_weights(prm), prm['g_final'])


def _run_trunk(x, p, start, pool_state, ssm_re, ssm_im, prm):
    b_len = x.shape[0]
    xt, pool_new = _layer0(x, pool_state, p, prm, start=start)
    xt, s_re, s_im = _ssm_layer(xt, ssm_re.reshape(b_len, SSM_FLAT), ssm_im.reshape(b_len, SSM_FLAT), prm)
    y = _ffn1_layer(xt, p, prm)
    s_re = s_re.reshape(1, b_len, SSM_GROUPS, SSM_STATE)
    s_im = s_im.reshape(1, b_len, SSM_GROUPS, SSM_STATE)
    return y, pool_new[None], s_re, s_im


def kernel(x_prompt, x_sample, state_pool, state_ssm_re, state_ssm_im, p_prompt, p_sample, g_mix, g_ffn, g_ple, g_final, pool_w, pool_scale, ssm_lambda_re, ssm_lambda_im, ssm_log_dt, ssm_b_re, ssm_b_im, ssm_c_re, ssm_c_im, ssm_d, ssm_w_glu, ffn_w_gate, ffn_w_up, ffn_w_down, ple_w_in, ple_w_gate):
    depth = g_mix.shape[0]
    a2_re, a2_im, ssm_in, ssm_out, ssm_direct = _ssm_prep(
        ssm_lambda_re[0], ssm_lambda_im[0], ssm_log_dt[0], ssm_b_re[0], ssm_b_im[0], ssm_c_re[0], ssm_c_im[0])
    per_layer_vec = lambda a: a.reshape(depth, 1, D_MODEL)
    prm = dict(
        g_mix=per_layer_vec(g_mix), g_ffn=per_layer_vec(g_ffn), g_ple=per_layer_vec(g_ple),
        g_final=g_final.reshape(1, D_MODEL),
        pool_w=pool_w[0].astype(BF16), pool_scale=pool_scale,
        a2_re=a2_re, a2_im=a2_im,
        ssm_in=ssm_in.astype(BF16), ssm_out=ssm_out.astype(BF16), ssm_direct=ssm_direct.astype(BF16),
        ssm_d=ssm_d, ssm_w_glu=ssm_w_glu[0].astype(BF16),
        ffn_w_gate=ffn_w_gate.astype(BF16), ffn_w_up=ffn_w_up.astype(BF16),
        ffn_w_down=ffn_w_down.astype(BF16),
        ple_w_in=ple_w_in.astype(BF16), ple_w_gate=ple_w_gate.astype(BF16),
    )
    b_p = x_prompt.shape[0]
    zeros_pool = jnp.zeros((b_p, POOL_BUF, D_MODEL), F32)
    zeros_ssm = jnp.zeros((b_p, SSM_GROUPS, SSM_STATE), F32)
    y_p, pool_p, re_p, im_p = _run_trunk(x_prompt, p_prompt, 0, zeros_pool, zeros_ssm, zeros_ssm, prm)
    y_s, pool_s, re_s, im_s = _run_trunk(x_sample, p_sample, PAST_LEN, state_pool[0],
                                         state_ssm_re[0], state_ssm_im[0], prm)
    return (y_p, y_s, pool_p, pool_s, re_p, im_p, re_s, im_s)
```

```python
import functools
import math

import jax
import jax.numpy as jnp
from jax import lax
from jax.experimental import pallas as pl
from jax.experimental.pallas import tpu as pltpu

D_MODEL = 1024
POOL_WINDOWS = (2, 4, 8, 16)
POOL_GROUP_DIM = D_MODEL // len(POOL_WINDOWS)
POOL_BUF = max(POOL_WINDOWS) - 1
SSM_GROUP_DIM = 16
SSM_GROUPS = D_MODEL // SSM_GROUP_DIM
SSM_STATE = 64
SSM_FLAT = SSM_GROUPS * SSM_STATE
PAST_LEN = 16384
EPS = 1e-6

V7X_SUBLANES = 8
V7X_LANES = 128
V7X_MXU_DIM = 256
V7X_VMEM_LIMIT_BYTES = 60 * 1024 * 1024

SSM_COL_BLOCK = V7X_MXU_DIM
SSM_GROUPS_PER_BLOCK = SSM_COL_BLOCK // SSM_GROUP_DIM
SSM_STATE_BLOCK = SSM_GROUPS_PER_BLOCK * SSM_STATE
SSM_N_BLOCKS = D_MODEL // SSM_COL_BLOCK
SSM_LANE_GROUPS = V7X_LANES // SSM_GROUP_DIM
SSM_LANE_STATES = SSM_LANE_GROUPS * SSM_STATE
SSM_N_LANE_BLOCKS = D_MODEL // V7X_LANES
SSM_PAIR_COLS = 2 * V7X_LANES
SSM_PAIR_STATES = 2 * SSM_LANE_STATES

POOL_LAYER_ROWS = 512
SSM_ROWS = 512
FFN_ROWS = 1024
FFN_COLS = 2 * V7X_MXU_DIM
OUT_COLS = 2 * V7X_MXU_DIM

BF16 = jnp.bfloat16
F32 = jnp.float32


def _rmsnorm(x, g):
    return x * lax.rsqrt(jnp.mean(x * x, axis=-1, keepdims=True) + EPS) * g


def _dot(a, b):
    return jnp.dot(a, b, preferred_element_type=F32)


def _dot_nt_f32(a, b):
    return lax.dot_general(a, b, (((1,), (1,)), ((), ())), precision=lax.Precision.HIGHEST,
                           preferred_element_type=F32)


def _time_major(block):
    bb, tt, c = block.shape
    return jnp.transpose(block, (1, 0, 2)).reshape(tt * bb, c)


def _batch_major(rows, tt, bb):
    return jnp.transpose(rows.reshape(tt, bb, rows.shape[-1]), (1, 0, 2))


def _const_spec(shape):
    zeros = (0,) * len(shape)
    return pl.BlockSpec(shape, lambda *_: zeros, pipeline_mode=pl.Buffered(1))


def _layer_spec(shape, layer):
    tail = (0,) * len(shape)
    return pl.BlockSpec((None,) + tuple(shape), lambda *_: (layer,) + tail,
                        pipeline_mode=pl.Buffered(1))


def _params():
    return pltpu.CompilerParams(dimension_semantics=("arbitrary", "arbitrary"),
                                vmem_limit_bytes=V7X_VMEM_LIMIT_BYTES)


def _tiles(b_len, t_len, rows):
    bb = min(b_len, max(V7X_SUBLANES, rows // t_len))
    tt = min(t_len, rows // bb)
    assert t_len % tt == 0 and b_len % bb == 0 and bb % V7X_SUBLANES == 0
    return tt, bb


class _Schedule:
    def __init__(self, b_len, t_len, rows):
        self.tt, self.bb = _tiles(b_len, t_len, rows)
        self.nt = t_len // self.tt
        self.lag = 1 if self.nt > 1 else 0
        self.grid = (b_len // self.bb, self.nt + self.lag)

    def mixer_block(self, i):
        return jnp.minimum(i, self.nt - 1)

    def ffn_block(self, i):
        return jnp.maximum(i - self.lag, 0)


def _ssm_prep_kernel(lam_re_ref, lam_im_ref, log_dt_ref, b_re_ref, b_im_ref, c_re_ref, c_im_ref,
                     a2_re_ref, a2_im_ref, in_ref, out_ref, direct_ref,
                     lb_re_row, lb_im_row, wb_re, wb_im, wc_re, wc_im):
    lam_re = lam_re_ref[...]
    lam_im = lam_im_ref[...]
    dt = jnp.exp(log_dt_ref[...])
    mag = jnp.exp(lam_re * dt)
    lb_re = mag * jnp.cos(lam_im * dt)
    lb_im = mag * jnp.sin(lam_im * dt)
    den = lam_re * lam_re + lam_im * lam_im
    f_re = ((lb_re - 1.0) * lam_re + lb_im * lam_im) / den
    f_im = (lb_im * lam_re - (lb_re - 1.0) * lam_im) / den
    b_re = b_re_ref[...]
    b_im = b_im_ref[...]
    fr = f_re[:, None, :]
    fi = f_im[:, None, :]
    bb_re = fr * b_re - fi * b_im
    bb_im = fr * b_im + fi * b_re
    c_re = c_re_ref[...]
    c_im_neg = -c_im_ref[...]

    for ref in (wb_re, wb_im, wc_re, wc_im):
        ref[...] = jnp.zeros(ref.shape, ref.dtype)
    for g in range(SSM_GROUPS):
        cb, gl = divmod(g, SSM_GROUPS_PER_BLOCK)
        cols = slice(gl * SSM_GROUP_DIM, (gl + 1) * SSM_GROUP_DIM)
        states = slice(gl * SSM_STATE, (gl + 1) * SSM_STATE)
        wb_re[cb, cols, states] = bb_re[g]
        wb_im[cb, cols, states] = bb_im[g]
        wc_re[cb, cols, states] = c_re[g]
        wc_im[cb, cols, states] = c_im_neg[g]
        flat = slice(g * SSM_STATE, (g + 1) * SSM_STATE)
        lb_re_row[:, flat] = lb_re[g:g + 1, :]
        lb_im_row[:, flat] = lb_im[g:g + 1, :]

    lr_all = lb_re_row[...]
    li_all = lb_im_row[...]
    a2_re_ref[...] = lr_all * lr_all - li_all * li_all
    a2_im_ref[...] = 2.0 * lr_all * li_all

    halves = SSM_COL_BLOCK // V7X_LANES
    for cb in range(SSM_N_BLOCKS):
        flat = slice(cb * SSM_STATE_BLOCK, (cb + 1) * SSM_STATE_BLOCK)
        lr, li = lb_re_row[:, flat], lb_im_row[:, flat]
        l2r, l2i = a2_re_ref[:, flat], a2_im_ref[:, flat]
        w_r, w_i = wb_re[cb], wb_im[cb]
        wl_r, wl_i = w_r * lr - w_i * li, w_r * li + w_i * lr
        c_r, c_n = wc_re[cb], wc_im[cb]
        o1_r, o1_i = c_r * lr + c_n * li, c_n * lr - c_r * li
        o2_r, o2_i = c_r * l2r + c_n * l2i, c_n * l2r - c_r * l2i
        for half in range(halves):
            j = cb * halves + half
            r = slice(half * V7X_LANES, (half + 1) * V7X_LANES)
            s = slice(half * SSM_LANE_STATES, (half + 1) * SSM_LANE_STATES)
            t0, t1 = slice(0, V7X_LANES), slice(V7X_LANES, SSM_PAIR_COLS)
            re, im = slice(0, SSM_LANE_STATES), slice(SSM_LANE_STATES, SSM_PAIR_STATES)
            in_ref[j, t0, re] = wl_r[r, s]
            in_ref[j, t0, im] = wl_i[r, s]
            in_ref[j, t1, re] = w_r[r, s]
            in_ref[j, t1, im] = w_i[r, s]
            k0 = _dot_nt_f32(w_r[r, s], c_r[r, s]) + _dot_nt_f32(w_i[r, s], c_n[r, s])
            k1 = _dot_nt_f32(wl_r[r, s], c_r[r, s]) + _dot_nt_f32(wl_i[r, s], c_n[r, s])
            direct_ref[j, t0, t0] = k0
            direct_ref[j, t0, t1] = k1
            direct_ref[j, t1, t0] = jnp.zeros((V7X_LANES, V7X_LANES), F32)
            direct_ref[j, t1, t1] = k0
            out_ref[j, re, t0] = o1_r[r, s].T
            out_ref[j, im, t0] = o1_i[r, s].T
            out_ref[j, re, t1] = o2_r[r, s].T
            out_ref[j, im, t1] = o2_i[r, s].T


def _ssm_prep(lam_re, lam_im, log_dt, b_re, b_im, c_re, c_im):
    g, p, _ = b_re.shape
    row = jax.ShapeDtypeStruct((1, SSM_FLAT), F32)
    n = SSM_N_LANE_BLOCKS
    swap = lambda a: jnp.transpose(a, (0, 2, 1))
    block_diag = pltpu.VMEM((SSM_N_BLOCKS, SSM_COL_BLOCK, SSM_STATE_BLOCK), F32)
    return pl.pallas_call(
        _ssm_prep_kernel,
        out_shape=(row, row,
                   jax.ShapeDtypeStruct((n, SSM_PAIR_COLS, SSM_PAIR_STATES), F32),
                   jax.ShapeDtypeStruct((n, SSM_PAIR_STATES, SSM_PAIR_COLS), F32),
                   jax.ShapeDtypeStruct((n, SSM_PAIR_COLS, SSM_PAIR_COLS), F32)),
        scratch_shapes=[pltpu.VMEM((1, SSM_FLAT), F32), pltpu.VMEM((1, SSM_FLAT), F32),
                        block_diag, block_diag, block_diag, block_diag],
        compiler_params=pltpu.CompilerParams(vmem_limit_bytes=V7X_VMEM_LIMIT_BYTES),
        name="ssm_prep",
    )(lam_re, lam_im, log_dt.reshape(g, 1), swap(b_re), swap(b_im), c_re, c_im)


def _pool_pieces(x_ref, dst_ref, ext_ref, w_ref, scale_ref, g, *, tt, bb, first_pos):
    tm = tt * bb
    halo = POOL_BUF * bb
    dst_ref[...] = _time_major(x_ref[...])
    ext_ref[halo:halo + tm, :] = _rmsnorm(dst_ref[...], g)
    yield
    row = lax.broadcasted_iota(jnp.int32, (tm, 1), 0)
    pos = first_pos + lax.shift_right_logical(row, int(math.log2(bb)))
    for gi, w in enumerate(POOL_WINDOWS):
        cols = slice(gi * POOL_GROUP_DIM, (gi + 1) * POOL_GROUP_DIM)
        acc = ext_ref[halo:halo + tm, cols]
        for k in range(1, w):
            acc = acc + ext_ref[halo - k * bb:halo - k * bb + tm, cols]
        inv_cnt = 1.0 / jnp.minimum(w, pos + 1).astype(F32)
        diff = acc * inv_cnt - ext_ref[halo:halo + tm, cols]
        dst_ref[:, cols] += _dot(diff.astype(BF16), w_ref[gi]) * scale_ref[:, cols]
        yield


def _ffn_ple_pieces(rows_ref, h_in_ref, p_ref, w_gate_ref, w_up_ref, w_down_ref, g_ple, ple_in_ref, ple_gate_ref,
                    h_ref, pt_ref, act_ref, release_h_in, emit):
    d_ff = act_ref.shape[1]
    pt_ref[...] = _time_major(p_ref[...]).astype(BF16)
    for c0 in range(0, d_ff, FFN_COLS):
        cols = slice(c0, min(c0 + FFN_COLS, d_ff))
        gate = _dot(h_in_ref[...], w_gate_ref[:, cols])
        up = _dot(h_in_ref[...], w_up_ref[:, cols])
        act_ref[:, cols] = (jax.nn.silu(gate) * up).astype(BF16)
        if cols.stop == d_ff:
            release_h_in()
        yield
    for n0 in range(0, D_MODEL, OUT_COLS):
        cols = slice(n0, n0 + OUT_COLS)
        rows_ref[:, cols] += _dot(act_ref[...], w_down_ref[:, cols])
        yield
    h_ref[...] = _rmsnorm(rows_ref[...], g_ple).astype(BF16)
    for n0 in range(0, D_MODEL, OUT_COLS):
        cols = slice(n0, n0 + OUT_COLS)
        ple_gate = jax.nn.sigmoid(_dot(h_ref[...], ple_gate_ref[:, cols]))
        rows_ref[:, cols] += _dot(pt_ref[...], ple_in_ref[:, cols]) * ple_gate
        yield
    emit()
    yield


_DONE = object()


def _interleave(*stages):
    live = list(stages)
    while live:
        for stage in list(live):
            if next(stage, _DONE) is _DONE:
                live.remove(stage)


def _then(stage, last_piece):
    yield from stage
    last_piece()
    yield


def _run_halves(sched, g_ffn, mid_ref, hmid_ref, rows_ref, mixer, ffn):
    h_in_reads_issued = []

    def mixer_into(dst_ref):
        def norm_for_ffn():
            assert h_in_reads_issued or not sched.lag, "hmid_ref rewritten before the FFN half read it"
            hmid_ref[...] = _rmsnorm(dst_ref[...], g_ffn).astype(BF16)
        return _then(mixer(dst_ref), norm_for_ffn)

    def release_h_in():
        h_in_reads_issued.append(True)

    if sched.lag:
        rows_ref[...] = mid_ref[...]
        _interleave(ffn(hmid_ref, release_h_in), mixer_into(mid_ref))
    else:
        _interleave(mixer_into(rows_ref))
        _interleave(ffn(hmid_ref, release_h_in))


def _init_handoff(sched, mid_ref, hmid_ref):
    if sched.lag:
        mid_ref[...] = jnp.zeros(mid_ref.shape, mid_ref.dtype)
        hmid_ref[...] = jnp.zeros(hmid_ref.shape, hmid_ref.dtype)


def _ffn_scratch(tm, d_ff, ple_dim):
    return [
        pltpu.VMEM((tm, D_MODEL), BF16),
        pltpu.VMEM((tm, D_MODEL), F32),
        pltpu.VMEM((tm, D_MODEL), BF16),
        pltpu.VMEM((tm, ple_dim), BF16),
        pltpu.VMEM((tm, d_ff), BF16),
    ]


def _layer0_kernel(x_ref, prev_ref, p_ref, g_mix_ref, pool_w_ref, pool_scale_ref,
                   g_ffn_ref, w_gate_ref, w_up_ref, w_down_ref, g_ple_ref, ple_in_ref, ple_gate_ref,
                   o_ref, state_ref, ext_ref, mid_ref, hmid_ref, rows_ref, h_ref, pt_ref, act_ref,
                   *, sched, start):
    tt, bb = sched.tt, sched.bb
    tm = tt * bb
    halo = POOL_BUF * bb
    i = pl.program_id(1)

    @pl.when(i == 0)
    def _():
        ext_ref[0:halo, :] = _time_major(prev_ref[...])
        _init_handoff(sched, mid_ref, hmid_ref)

    @pl.when(i > 0)
    def _():
        ext_ref[0:halo, :] = ext_ref[tm:tm + halo, :]

    def mixer(dst_ref):
        return _pool_pieces(x_ref, dst_ref, ext_ref, pool_w_ref, pool_scale_ref, g_mix_ref[...],
                            tt=tt, bb=bb, first_pos=start + sched.mixer_block(i) * tt)

    def emit():
        o_ref[...] = rows_ref[...].reshape(tt, bb, D_MODEL)

    def ffn(h_in_ref, release_h_in):
        return _ffn_ple_pieces(rows_ref, h_in_ref, p_ref, w_gate_ref, w_up_ref, w_down_ref,
                               g_ple_ref[...], ple_in_ref, ple_gate_ref, h_ref, pt_ref, act_ref,
                               release_h_in, emit)

    _run_halves(sched, g_ffn_ref[...], mid_ref, hmid_ref, rows_ref, mixer, ffn)

    @pl.when(i == sched.nt - 1)
    def _():
        state_ref[...] = _batch_major(ext_ref[tm:tm + halo, :], POOL_BUF, bb)


def _ffn_weight_specs(prm, layer):
    return [
        _layer_spec((1, D_MODEL), layer),
        _layer_spec(prm['ffn_w_gate'].shape[1:], layer), _layer_spec(prm['ffn_w_up'].shape[1:], layer),
        _layer_spec(prm['ffn_w_down'].shape[1:], layer),
        _layer_spec((1, D_MODEL), layer),
        _layer_spec(prm['ple_w_in'].shape[1:], layer), _layer_spec(prm['ple_w_gate'].shape[1:], layer),
    ]


def _ffn_weights(prm):
    return (prm['g_ffn'], prm['ffn_w_gate'], prm['ffn_w_up'], prm['ffn_w_down'],
            prm['g_ple'], prm['ple_w_in'], prm['ple_w_gate'])


def _layer0(x, prev, p, prm, *, start):
    b_len, t_len, _ = x.shape
    sched = _Schedule(b_len, t_len, POOL_LAYER_ROWS)
    tt, bb = sched.tt, sched.bb
    tm = tt * bb
    state_spec = pl.BlockSpec((bb, POOL_BUF, D_MODEL), lambda j, i: (j, 0, 0))
    mid_rows = tm if sched.lag else 2 * V7X_SUBLANES
    return pl.pallas_call(
        functools.partial(_layer0_kernel, sched=sched, start=start),
        grid=sched.grid,
        in_specs=[
            pl.BlockSpec((bb, tt, D_MODEL), lambda j, i: (j, sched.mixer_block(i), 0)),
            state_spec,
            pl.BlockSpec((None, bb, tt, p.shape[-1]), lambda j, i: (0, j, sched.ffn_block(i), 0)),
            _layer_spec((1, D_MODEL), 0),
            _const_spec(prm['pool_w'].shape),
            _const_spec((1, D_MODEL)),
        ] + _ffn_weight_specs(prm, 0),
        out_specs=[pl.BlockSpec((tt, bb, D_MODEL), lambda j, i: (sched.ffn_block(i), j, 0)), state_spec],
        out_shape=[
            jax.ShapeDtypeStruct((t_len, b_len, D_MODEL), F32),
            jax.ShapeDtypeStruct((b_len, POOL_BUF, D_MODEL), F32),
        ],
        scratch_shapes=([pltpu.VMEM(((POOL_BUF + tt) * bb, D_MODEL), F32),
                         pltpu.VMEM((mid_rows, D_MODEL), F32)]
                        + _ffn_scratch(tm, prm['ffn_w_gate'].shape[-1], p.shape[-1])),
        compiler_params=_params(),
        name="layer0_pool_ffn",
    )(x, prev, p, prm['g_mix'], prm['pool_w'], prm['pool_scale'], *_ffn_weights(prm))


def _ssm_kernel(x_ref, h0_re_ref, h0_im_ref, g_ref, a2_re_ref, a2_im_ref, in_ref, out_ref, direct_ref,
                d_ref, wglu_ref, o_ref, s_re_ref, s_im_ref,
                st_re, st_im, u_ref, xj_ref, sbuf_ref, z_ref, *, pairs, bb):
    m = pairs * bb
    i = pl.program_id(1)

    @pl.when(i == 0)
    def _():
        st_re[...] = h0_re_ref[...]
        st_im[...] = h0_im_ref[...]

    for t in range(2):
        u_ref[t] = _rmsnorm(x_ref[:, t].reshape(m, D_MODEL), g_ref[...])

    def contributions(j):
        cols = slice(j * V7X_LANES, (j + 1) * V7X_LANES)
        both = jnp.concatenate([u_ref[0, :, cols], u_ref[1, :, cols]], axis=-1)
        xj_ref[j] = both.astype(BF16)
        sbuf_ref[:, j * SSM_PAIR_STATES:(j + 1) * SSM_PAIR_STATES] = _dot(xj_ref[j], in_ref[j])

    def scan(j):
        states = slice(j * SSM_LANE_STATES, (j + 1) * SSM_LANE_STATES)
        re = slice(j * SSM_PAIR_STATES, j * SSM_PAIR_STATES + SSM_LANE_STATES)
        im = slice(j * SSM_PAIR_STATES + SSM_LANE_STATES, (j + 1) * SSM_PAIR_STATES)
        ar = jnp.broadcast_to(a2_re_ref[:, states], (V7X_SUBLANES, SSM_LANE_STATES))
        ai = jnp.broadcast_to(a2_im_ref[:, states], (V7X_SUBLANES, SSM_LANE_STATES))
        for r0 in range(0, bb, V7X_SUBLANES):
            sr = st_re[r0:r0 + V7X_SUBLANES, states]
            si = st_im[r0:r0 + V7X_SUBLANES, states]
            for c in range(pairs):
                rows = slice(c * bb + r0, c * bb + r0 + V7X_SUBLANES)
                pr, pi = sbuf_ref[rows, re], sbuf_ref[rows, im]
                sbuf_ref[rows, re] = sr
                sbuf_ref[rows, im] = si
                sr, si = ar * sr - ai * si + pr, ar * si + ai * sr + pi
            st_re[r0:r0 + V7X_SUBLANES, states] = sr
            st_im[r0:r0 + V7X_SUBLANES, states] = si

    def outputs(j):
        cols = slice(j * V7X_LANES, (j + 1) * V7X_LANES)
        entering = sbuf_ref[:, j * SSM_PAIR_STATES:(j + 1) * SSM_PAIR_STATES].astype(BF16)
        y = _dot(entering, out_ref[j]) + _dot(xj_ref[j], direct_ref[j])
        for t in range(2):
            yt = y[:, t * V7X_LANES:(t + 1) * V7X_LANES] + d_ref[:, cols] * u_ref[t, :, cols]
            z_ref[t, :, cols] = jax.nn.gelu(yt).astype(BF16)

    for k in range(SSM_N_LANE_BLOCKS + 2):
        if 0 <= k - 2:
            outputs(k - 2)
        if k < SSM_N_LANE_BLOCKS:
            contributions(k)
        if 0 <= k - 1 < SSM_N_LANE_BLOCKS:
            scan(k - 1)

    for t in range(2):
        for n0 in range(0, D_MODEL, OUT_COLS):
            cols = slice(n0, n0 + OUT_COLS)
            a = _dot(z_ref[t], wglu_ref[:, cols])
            gate = _dot(z_ref[t], wglu_ref[:, D_MODEL + n0:D_MODEL + n0 + OUT_COLS])
            o_ref[:, t, :, cols] = x_ref[:, t, :, cols] + (a * jax.nn.sigmoid(gate)).reshape(pairs, bb, OUT_COLS)

    s_re_ref[...] = st_re[...]
    s_im_ref[...] = st_im[...]


def _ssm_layer(x, h0_re, h0_im, prm):
    t_len, b_len, _ = x.shape
    tt, bb = _tiles(b_len, t_len, SSM_ROWS)
    assert tt % 2 == 0
    pairs = tt // 2
    m = pairs * bb
    x_spec = pl.BlockSpec((pairs, 2, bb, D_MODEL), lambda j, i: (i, 0, j, 0))
    st_spec = pl.BlockSpec((bb, SSM_FLAT), lambda j, i: (j, 0))
    out, s_re, s_im = pl.pallas_call(
        functools.partial(_ssm_kernel, pairs=pairs, bb=bb),
        grid=(b_len // bb, t_len // tt),
        in_specs=[
            x_spec, st_spec, st_spec,
            _layer_spec((1, D_MODEL), 1),
            _const_spec((1, SSM_FLAT)), _const_spec((1, SSM_FLAT)),
            _const_spec(prm['ssm_in'].shape), _const_spec(prm['ssm_out'].shape),
            _const_spec(prm['ssm_direct'].shape),
            _const_spec((1, D_MODEL)),
            _const_spec(prm['ssm_w_glu'].shape),
        ],
        out_specs=[x_spec, st_spec, st_spec],
        out_shape=[
            jax.ShapeDtypeStruct((t_len // 2, 2, b_len, D_MODEL), F32),
            jax.ShapeDtypeStruct((b_len, SSM_FLAT), F32),
            jax.ShapeDtypeStruct((b_len, SSM_FLAT), F32),
        ],
        scratch_shapes=[
            pltpu.VMEM((bb, SSM_FLAT), F32), pltpu.VMEM((bb, SSM_FLAT), F32),
            pltpu.VMEM((2, m, D_MODEL), F32),
            pltpu.VMEM((SSM_N_LANE_BLOCKS, m, SSM_PAIR_COLS), BF16),
            pltpu.VMEM((m, SSM_N_LANE_BLOCKS * SSM_PAIR_STATES), F32),
            pltpu.VMEM((2, m, D_MODEL), BF16),
        ],
        compiler_params=_params(),
        name="ssm_mixer",
    )(x.reshape(t_len // 2, 2, b_len, D_MODEL), h0_re, h0_im, prm['g_mix'], prm['a2_re'], prm['a2_im'],
      prm['ssm_in'], prm['ssm_out'], prm['ssm_direct'], prm['ssm_d'], prm['ssm_w_glu'])
    return out.reshape(t_len, b_len, D_MODEL), s_re, s_im


def _ffn1_kernel(x_ref, p_ref, g_ffn_ref, w_gate_ref, w_up_ref, w_down_ref, g_ple_ref, ple_in_ref, ple_gate_ref,
                 g_final_ref, o_ref, h_in_ref, rows_ref, h_ref, pt_ref, act_ref, *, tt, bb):
    rows_ref[...] = x_ref[...].reshape(tt * bb, D_MODEL)
    h_in_ref[...] = _rmsnorm(rows_ref[...], g_ffn_ref[...]).astype(BF16)

    def emit():
        o_ref[...] = _batch_major(_rmsnorm(rows_ref[...], g_final_ref[...]), tt, bb)

    _interleave(_ffn_ple_pieces(rows_ref, h_in_ref, p_ref, w_gate_ref, w_up_ref, w_down_ref,
                                g_ple_ref[...], ple_in_ref, ple_gate_ref, h_ref, pt_ref, act_ref,
                                lambda: None, emit))


def _ffn1_layer(x, p, prm):
    t_len, b_len, _ = x.shape
    tt, bb = _tiles(b_len, t_len, FFN_ROWS)
    return pl.pallas_call(
        functools.partial(_ffn1_kernel, tt=tt, bb=bb),
        grid=(b_len // bb, t_len // tt),
        in_specs=[
            pl.BlockSpec((tt, bb, D_MODEL), lambda j, i: (i, j, 0)),
            pl.BlockSpec((None, bb, tt, p.shape[-1]), lambda j, i: (1, j, i, 0)),
        ] + _ffn_weight_specs(prm, 1) + [_const_spec((1, D_MODEL))],
        out_specs=pl.BlockSpec((bb, tt, D_MODEL), lambda j, i: (j, i, 0)),
        out_shape=jax.ShapeDtypeStruct((b_len, t_len, D_MODEL), F32),
        scratch_shapes=_ffn_scratch(tt * bb, prm['ffn_w_gate'].shape[-1], p.shape[-1]),
        compiler_params=_params(),
        name="layer1_ffn",
    )(x, p, *_ffn_weights(prm), prm['g_final'])


def _run_trunk(x, p, start, pool_state, ssm_re, ssm_im, prm):
    b_len = x.shape[0]
    xt, pool_new = _layer0(x, pool_state, p, prm, start=start)
    xt, s_re, s_im = _ssm_layer(xt, ssm_re.reshape(b_len, SSM_FLAT), ssm_im.reshape(b_len, SSM_FLAT), prm)
    y = _ffn1_layer(xt, p, prm)
    s_re = s_re.reshape(1, b_len, SSM_GROUPS, SSM_STATE)
    s_im = s_im.reshape(1, b_len, SSM_GROUPS, SSM_STATE)
    return y, pool_new[None], s_re, s_im


def kernel(x_prompt, x_sample, state_pool, state_ssm_re, state_ssm_im, p_prompt, p_sample, g_mix, g_ffn, g_ple, g_final, pool_w, pool_scale, ssm_lambda_re, ssm_lambda_im, ssm_log_dt, ssm_b_re, ssm_b_im, ssm_c_re, ssm_c_im, ssm_d, ssm_w_glu, ffn_w_gate, ffn_w_up, ffn_w_down, ple_w_in, ple_w_gate):
    depth = g_mix.shape[0]
    a2_re, a2_im, ssm_in, ssm_out, ssm_direct = _ssm_prep(
        ssm_lambda_re[0], ssm_lambda_im[0], ssm_log_dt[0], ssm_b_re[0], ssm_b_im[0], ssm_c_re[0], ssm_c_im[0])
    per_layer_vec = lambda a: a.reshape(depth, 1, D_MODEL)
    prm = dict(
        g_mix=per_layer_vec(g_mix), g_ffn=per_layer_vec(g_ffn), g_ple=per_layer_vec(g_ple),
        g_final=g_final.reshape(1, D_MODEL),
        pool_w=pool_w[0].astype(BF16), pool_scale=pool_scale,
        a2_re=a2_re, a2_im=a2_im,
        ssm_in=ssm_in.astype(BF16), ssm_out=ssm_out.astype(BF16), ssm_direct=ssm_direct.astype(BF16),
        ssm_d=ssm_d, ssm_w_glu=ssm_w_glu[0].astype(BF16),
        ffn_w_gate=ffn_w_gate.astype(BF16), ffn_w_up=ffn_w_up.astype(BF16),
        ffn_w_down=ffn_w_down.astype(BF16),
        ple_w_in=ple_w_in.astype(BF16), ple_w_gate=ple_w_gate.astype(BF16),
    )
    b_p = x_prompt.shape[0]
    zeros_pool = jnp.zeros((b_p, POOL_BUF, D_MODEL), F32)
    zeros_ssm = jnp.zeros((b_p, SSM_GROUPS, SSM_STATE), F32)
    y_p, pool_p, re_p, im_p = _run_trunk(x_prompt, p_prompt, 0, zeros_pool, zeros_ssm, zeros_ssm, prm)
    y_s, pool_s, re_s, im_s = _run_trunk(x_sample, p_sample, PAST_LEN, state_pool[0],
                                         state_ssm_re[0], state_ssm_im[0], prm)
    return (y_p, y_s, pool_p, pool_s, re_p, im_p, re_s, im_s)
```

```python
import functools
import math

import jax
import jax.numpy as jnp
from jax import lax
from jax.experimental import pallas as pl
from jax.experimental.pallas import tpu as pltpu

D_MODEL = 1024
POOL_WINDOWS = (2, 4, 8, 16)
POOL_GROUP_DIM = D_MODEL // len(POOL_WINDOWS)
POOL_BUF = max(POOL_WINDOWS) - 1
SSM_GROUP_DIM = 16
SSM_GROUPS = D_MODEL // SSM_GROUP_DIM
SSM_STATE = 64
SSM_FLAT = SSM_GROUPS * SSM_STATE
PAST_LEN = 16384
EPS = 1e-6

V7X_SUBLANES = 8
V7X_LANES = 128
V7X_MXU_DIM = 256
V7X_VMEM_LIMIT_BYTES = 60 * 1024 * 1024

SSM_COL_BLOCK = V7X_MXU_DIM
SSM_GROUPS_PER_BLOCK = SSM_COL_BLOCK // SSM_GROUP_DIM
SSM_STATE_BLOCK = SSM_GROUPS_PER_BLOCK * SSM_STATE
SSM_N_BLOCKS = D_MODEL // SSM_COL_BLOCK
SSM_LANE_GROUPS = V7X_LANES // SSM_GROUP_DIM
SSM_LANE_STATES = SSM_LANE_GROUPS * SSM_STATE
SSM_N_LANE_BLOCKS = D_MODEL // V7X_LANES
SSM_PAIR_COLS = 2 * V7X_LANES
SSM_PAIR_STATES = 2 * SSM_LANE_STATES

POOL_LAYER_ROWS = 512
SSM_ROWS = 512
FFN_ROWS = 1024
FFN_COLS = 2 * V7X_MXU_DIM
OUT_COLS = 2 * V7X_MXU_DIM
GLU_COLS = V7X_MXU_DIM

BF16 = jnp.bfloat16
F32 = jnp.float32


def _rmsnorm(x, g):
    return x * lax.rsqrt(jnp.mean(x * x, axis=-1, keepdims=True) + EPS) * g


def _dot(a, b):
    return jnp.dot(a, b, preferred_element_type=F32)


def _dot_nt_f32(a, b):
    return lax.dot_general(a, b, (((1,), (1,)), ((), ())), precision=lax.Precision.HIGHEST,
                           preferred_element_type=F32)


def _time_major(block):
    bb, tt, c = block.shape
    return jnp.transpose(block, (1, 0, 2)).reshape(tt * bb, c)


def _batch_major(rows, tt, bb):
    return jnp.transpose(rows.reshape(tt, bb, rows.shape[-1]), (1, 0, 2))


def _const_spec(shape):
    zeros = (0,) * len(shape)
    return pl.BlockSpec(shape, lambda *_: zeros, pipeline_mode=pl.Buffered(1))


def _layer_spec(shape, layer):
    tail = (0,) * len(shape)
    return pl.BlockSpec((None,) + tuple(shape), lambda *_: (layer,) + tail,
                        pipeline_mode=pl.Buffered(1))


def _params():
    return pltpu.CompilerParams(dimension_semantics=("arbitrary", "arbitrary"),
                                vmem_limit_bytes=V7X_VMEM_LIMIT_BYTES)


def _tiles(b_len, t_len, rows):
    bb = min(b_len, max(V7X_SUBLANES, rows // t_len))
    tt = min(t_len, rows // bb)
    assert t_len % tt == 0 and b_len % bb == 0 and bb % V7X_SUBLANES == 0
    return tt, bb


class _Schedule:
    def __init__(self, b_len, t_len, rows):
        self.tt, self.bb = _tiles(b_len, t_len, rows)
        self.nt = t_len // self.tt
        self.lag = 1 if self.nt > 1 else 0
        self.grid = (b_len // self.bb, self.nt + self.lag)

    def mixer_block(self, i):
        return jnp.minimum(i, self.nt - 1)

    def ffn_block(self, i):
        return jnp.maximum(i - self.lag, 0)


def _ssm_prep_kernel(lam_re_ref, lam_im_ref, log_dt_ref, b_re_ref, b_im_ref, c_re_ref, c_im_ref,
                     a2_re_ref, a2_im_ref, in_ref, out_ref, direct_ref,
                     lb_re_row, lb_im_row, wb_re, wb_im, wc_re, wc_im):
    lam_re = lam_re_ref[...]
    lam_im = lam_im_ref[...]
    dt = jnp.exp(log_dt_ref[...])
    mag = jnp.exp(lam_re * dt)
    lb_re = mag * jnp.cos(lam_im * dt)
    lb_im = mag * jnp.sin(lam_im * dt)
    den = lam_re * lam_re + lam_im * lam_im
    f_re = ((lb_re - 1.0) * lam_re + lb_im * lam_im) / den
    f_im = (lb_im * lam_re - (lb_re - 1.0) * lam_im) / den
    b_re = b_re_ref[...]
    b_im = b_im_ref[...]
    fr = f_re[:, None, :]
    fi = f_im[:, None, :]
    bb_re = fr * b_re - fi * b_im
    bb_im = fr * b_im + fi * b_re
    c_re = c_re_ref[...]
    c_im_neg = -c_im_ref[...]

    for ref in (wb_re, wb_im, wc_re, wc_im):
        ref[...] = jnp.zeros(ref.shape, ref.dtype)
    for g in range(SSM_GROUPS):
        cb, gl = divmod(g, SSM_GROUPS_PER_BLOCK)
        cols = slice(gl * SSM_GROUP_DIM, (gl + 1) * SSM_GROUP_DIM)
        states = slice(gl * SSM_STATE, (gl + 1) * SSM_STATE)
        wb_re[cb, cols, states] = bb_re[g]
        wb_im[cb, cols, states] = bb_im[g]
        wc_re[cb, cols, states] = c_re[g]
        wc_im[cb, cols, states] = c_im_neg[g]
        flat = slice(g * SSM_STATE, (g + 1) * SSM_STATE)
        lb_re_row[:, flat] = lb_re[g:g + 1, :]
        lb_im_row[:, flat] = lb_im[g:g + 1, :]

    lr_all = lb_re_row[...]
    li_all = lb_im_row[...]
    a2_re_ref[...] = lr_all * lr_all - li_all * li_all
    a2_im_ref[...] = 2.0 * lr_all * li_all

    halves = SSM_COL_BLOCK // V7X_LANES
    for cb in range(SSM_N_BLOCKS):
        flat = slice(cb * SSM_STATE_BLOCK, (cb + 1) * SSM_STATE_BLOCK)
        lr, li = lb_re_row[:, flat], lb_im_row[:, flat]
        l2r, l2i = a2_re_ref[:, flat], a2_im_ref[:, flat]
        w_r, w_i = wb_re[cb], wb_im[cb]
        wl_r, wl_i = w_r * lr - w_i * li, w_r * li + w_i * lr
        c_r, c_n = wc_re[cb], wc_im[cb]
        o1_r, o1_i = c_r * lr + c_n * li, c_n * lr - c_r * li
        o2_r, o2_i = c_r * l2r + c_n * l2i, c_n * l2r - c_r * l2i
        for half in range(halves):
            j = cb * halves + half
            r = slice(half * V7X_LANES, (half + 1) * V7X_LANES)
            s = slice(half * SSM_LANE_STATES, (half + 1) * SSM_LANE_STATES)
            t0, t1 = slice(0, V7X_LANES), slice(V7X_LANES, SSM_PAIR_COLS)
            re, im = slice(0, SSM_LANE_STATES), slice(SSM_LANE_STATES, SSM_PAIR_STATES)
            in_ref[j, t0, re] = wl_r[r, s]
            in_ref[j, t0, im] = wl_i[r, s]
            in_ref[j, t1, re] = w_r[r, s]
            in_ref[j, t1, im] = w_i[r, s]
            k0 = _dot_nt_f32(w_r[r, s], c_r[r, s]) + _dot_nt_f32(w_i[r, s], c_n[r, s])
            k1 = _dot_nt_f32(wl_r[r, s], c_r[r, s]) + _dot_nt_f32(wl_i[r, s], c_n[r, s])
            direct_ref[j, t0, t0] = k0
            direct_ref[j, t0, t1] = k1
            direct_ref[j, t1, t0] = jnp.zeros((V7X_LANES, V7X_LANES), F32)
            direct_ref[j, t1, t1] = k0
            out_ref[j, re, t0] = o1_r[r, s].T
            out_ref[j, im, t0] = o1_i[r, s].T
            out_ref[j, re, t1] = o2_r[r, s].T
            out_ref[j, im, t1] = o2_i[r, s].T


def _ssm_prep(lam_re, lam_im, log_dt, b_re, b_im, c_re, c_im):
    g, p, _ = b_re.shape
    row = jax.ShapeDtypeStruct((1, SSM_FLAT), F32)
    n = SSM_N_LANE_BLOCKS
    swap = lambda a: jnp.transpose(a, (0, 2, 1))
    block_diag = pltpu.VMEM((SSM_N_BLOCKS, SSM_COL_BLOCK, SSM_STATE_BLOCK), F32)
    return pl.pallas_call(
        _ssm_prep_kernel,
        out_shape=(row, row,
                   jax.ShapeDtypeStruct((n, SSM_PAIR_COLS, SSM_PAIR_STATES), F32),
                   jax.ShapeDtypeStruct((n, SSM_PAIR_STATES, SSM_PAIR_COLS), F32),
                   jax.ShapeDtypeStruct((n, SSM_PAIR_COLS, SSM_PAIR_COLS), F32)),
        scratch_shapes=[pltpu.VMEM((1, SSM_FLAT), F32), pltpu.VMEM((1, SSM_FLAT), F32),
                        block_diag, block_diag, block_diag, block_diag],
        compiler_params=pltpu.CompilerParams(vmem_limit_bytes=V7X_VMEM_LIMIT_BYTES),
        name="ssm_prep",
    )(lam_re, lam_im, log_dt.reshape(g, 1), swap(b_re), swap(b_im), c_re, c_im)


def _pool_pieces(x_ref, dst_ref, ext_ref, w_ref, scale_ref, g, *, tt, bb, first_pos):
    tm = tt * bb
    halo = POOL_BUF * bb
    dst_ref[...] = _time_major(x_ref[...])
    ext_ref[halo:halo + tm, :] = _rmsnorm(dst_ref[...], g)
    yield
    row = lax.broadcasted_iota(jnp.int32, (tm, 1), 0)
    pos = first_pos + lax.shift_right_logical(row, int(math.log2(bb)))
    for gi, w in enumerate(POOL_WINDOWS):
        cols = slice(gi * POOL_GROUP_DIM, (gi + 1) * POOL_GROUP_DIM)
        acc = ext_ref[halo:halo + tm, cols]
        for k in range(1, w):
            acc = acc + ext_ref[halo - k * bb:halo - k * bb + tm, cols]
        inv_cnt = 1.0 / jnp.minimum(w, pos + 1).astype(F32)
        diff = acc * inv_cnt - ext_ref[halo:halo + tm, cols]
        dst_ref[:, cols] += _dot(diff.astype(BF16), w_ref[gi]) * scale_ref[:, cols]
        yield


def _ffn_ple_pieces(rows_ref, h_in_ref, p_ref, w_gate_ref, w_up_ref, w_down_ref, g_ple, ple_in_ref, ple_gate_ref,
                    h_ref, pt_ref, act_ref, release_h_in, emit):
    d_ff = act_ref.shape[1]
    pt_ref[...] = _time_major(p_ref[...]).astype(BF16)
    for c0 in range(0, d_ff, FFN_COLS):
        cols = slice(c0, min(c0 + FFN_COLS, d_ff))
        gate = _dot(h_in_ref[...], w_gate_ref[:, cols])
        up = _dot(h_in_ref[...], w_up_ref[:, cols])
        act_ref[:, cols] = (jax.nn.silu(gate) * up).astype(BF16)
        if cols.stop == d_ff:
            release_h_in()
        yield
    for n0 in range(0, D_MODEL, OUT_COLS):
        cols = slice(n0, n0 + OUT_COLS)
        rows_ref[:, cols] += _dot(act_ref[...], w_down_ref[:, cols])
        yield
    h_ref[...] = _rmsnorm(rows_ref[...], g_ple).astype(BF16)
    for n0 in range(0, D_MODEL, OUT_COLS):
        cols = slice(n0, n0 + OUT_COLS)
        ple_gate = jax.nn.sigmoid(_dot(h_ref[...], ple_gate_ref[:, cols]))
        rows_ref[:, cols] += _dot(pt_ref[...], ple_in_ref[:, cols]) * ple_gate
        yield
    emit()
    yield


_DONE = object()


def _interleave(*stages):
    live = list(stages)
    while live:
        for stage in list(live):
            if next(stage, _DONE) is _DONE:
                live.remove(stage)


def _then(stage, last_piece):
    yield from stage
    last_piece()
    yield


def _run_halves(sched, g_ffn, mid_ref, hmid_ref, rows_ref, mixer, ffn):
    h_in_reads_issued = []

    def mixer_into(dst_ref):
        def norm_for_ffn():
            assert h_in_reads_issued or not sched.lag, "hmid_ref rewritten before the FFN half read it"
            hmid_ref[...] = _rmsnorm(dst_ref[...], g_ffn).astype(BF16)
        return _then(mixer(dst_ref), norm_for_ffn)

    def release_h_in():
        h_in_reads_issued.append(True)

    if sched.lag:
        rows_ref[...] = mid_ref[...]
        _interleave(ffn(hmid_ref, release_h_in), mixer_into(mid_ref))
    else:
        _interleave(mixer_into(rows_ref))
        _interleave(ffn(hmid_ref, release_h_in))


def _init_handoff(sched, mid_ref, hmid_ref):
    if sched.lag:
        mid_ref[...] = jnp.zeros(mid_ref.shape, mid_ref.dtype)
        hmid_ref[...] = jnp.zeros(hmid_ref.shape, hmid_ref.dtype)


def _ffn_scratch(tm, d_ff, ple_dim):
    return [
        pltpu.VMEM((tm, D_MODEL), BF16),
        pltpu.VMEM((tm, D_MODEL), F32),
        pltpu.VMEM((tm, D_MODEL), BF16),
        pltpu.VMEM((tm, ple_dim), BF16),
        pltpu.VMEM((tm, d_ff), BF16),
    ]


def _layer0_kernel(x_ref, prev_ref, p_ref, g_mix_ref, pool_w_ref, pool_scale_ref,
                   g_ffn_ref, w_gate_ref, w_up_ref, w_down_ref, g_ple_ref, ple_in_ref, ple_gate_ref,
                   o_ref, state_ref, ext_ref, mid_ref, hmid_ref, rows_ref, h_ref, pt_ref, act_ref,
                   *, sched, start):
    tt, bb = sched.tt, sched.bb
    tm = tt * bb
    halo = POOL_BUF * bb
    i = pl.program_id(1)

    @pl.when(i == 0)
    def _():
        ext_ref[0:halo, :] = _time_major(prev_ref[...])
        _init_handoff(sched, mid_ref, hmid_ref)

    @pl.when(i > 0)
    def _():
        ext_ref[0:halo, :] = ext_ref[tm:tm + halo, :]

    def mixer(dst_ref):
        return _pool_pieces(x_ref, dst_ref, ext_ref, pool_w_ref, pool_scale_ref, g_mix_ref[...],
                            tt=tt, bb=bb, first_pos=start + sched.mixer_block(i) * tt)

    def emit():
        o_ref[...] = rows_ref[...].reshape(tt, bb, D_MODEL)

    def ffn(h_in_ref, release_h_in):
        return _ffn_ple_pieces(rows_ref, h_in_ref, p_ref, w_gate_ref, w_up_ref, w_down_ref,
                               g_ple_ref[...], ple_in_ref, ple_gate_ref, h_ref, pt_ref, act_ref,
                               release_h_in, emit)

    _run_halves(sched, g_ffn_ref[...], mid_ref, hmid_ref, rows_ref, mixer, ffn)

    @pl.when(i == sched.nt - 1)
    def _():
        state_ref[...] = _batch_major(ext_ref[tm:tm + halo, :], POOL_BUF, bb)


def _ffn_weight_specs(prm, layer):
    return [
        _layer_spec((1, D_MODEL), layer),
        _layer_spec(prm['ffn_w_gate'].shape[1:], layer), _layer_spec(prm['ffn_w_up'].shape[1:], layer),
        _layer_spec(prm['ffn_w_down'].shape[1:], layer),
        _layer_spec((1, D_MODEL), layer),
        _layer_spec(prm['ple_w_in'].shape[1:], layer), _layer_spec(prm['ple_w_gate'].shape[1:], layer),
    ]


def _ffn_weights(prm):
    return (prm['g_ffn'], prm['ffn_w_gate'], prm['ffn_w_up'], prm['ffn_w_down'],
            prm['g_ple'], prm['ple_w_in'], prm['ple_w_gate'])


def _layer0(x, prev, p, prm, *, start):
    b_len, t_len, _ = x.shape
    sched = _Schedule(b_len, t_len, POOL_LAYER_ROWS)
    tt, bb = sched.tt, sched.bb
    tm = tt * bb
    state_spec = pl.BlockSpec((bb, POOL_BUF, D_MODEL), lambda j, i: (j, 0, 0))
    mid_rows = tm if sched.lag else 2 * V7X_SUBLANES
    return pl.pallas_call(
        functools.partial(_layer0_kernel, sched=sched, start=start),
        grid=sched.grid,
        in_specs=[
            pl.BlockSpec((bb, tt, D_MODEL), lambda j, i: (j, sched.mixer_block(i), 0)),
            state_spec,
            pl.BlockSpec((None, bb, tt, p.shape[-1]), lambda j, i: (0, j, sched.ffn_block(i), 0)),
            _layer_spec((1, D_MODEL), 0),
            _const_spec(prm['pool_w'].shape),
            _const_spec((1, D_MODEL)),
        ] + _ffn_weight_specs(prm, 0),
        out_specs=[pl.BlockSpec((tt, bb, D_MODEL), lambda j, i: (sched.ffn_block(i), j, 0)), state_spec],
        out_shape=[
            jax.ShapeDtypeStruct((t_len, b_len, D_MODEL), F32),
            jax.ShapeDtypeStruct((b_len, POOL_BUF, D_MODEL), F32),
        ],
        scratch_shapes=([pltpu.VMEM(((POOL_BUF + tt) * bb, D_MODEL), F32),
                         pltpu.VMEM((mid_rows, D_MODEL), F32)]
                        + _ffn_scratch(tm, prm['ffn_w_gate'].shape[-1], p.shape[-1])),
        compiler_params=_params(),
        name="layer0_pool_ffn",
    )(x, prev, p, prm['g_mix'], prm['pool_w'], prm['pool_scale'], *_ffn_weights(prm))


def _ssm_mixer_pieces(x_ref, z_ref, g, a2_re_ref, a2_im_ref, in_ref, out_ref, direct_ref, d_ref,
                      st_re, st_im, u_ref, xj_ref, sbuf_ref, *, pairs, bb):
    m = pairs * bb
    for t in range(2):
        u_ref[t] = _rmsnorm(x_ref[:, t].reshape(m, D_MODEL), g)
    yield

    def contributions(j):
        cols = slice(j * V7X_LANES, (j + 1) * V7X_LANES)
        both = jnp.concatenate([u_ref[0, :, cols], u_ref[1, :, cols]], axis=-1)
        xj_ref[j] = both.astype(BF16)
        sbuf_ref[:, j * SSM_PAIR_STATES:(j + 1) * SSM_PAIR_STATES] = _dot(xj_ref[j], in_ref[j])

    def scan(j):
        states = slice(j * SSM_LANE_STATES, (j + 1) * SSM_LANE_STATES)
        re = slice(j * SSM_PAIR_STATES, j * SSM_PAIR_STATES + SSM_LANE_STATES)
        im = slice(j * SSM_PAIR_STATES + SSM_LANE_STATES, (j + 1) * SSM_PAIR_STATES)
        ar = jnp.broadcast_to(a2_re_ref[:, states], (V7X_SUBLANES, SSM_LANE_STATES))
        ai = jnp.broadcast_to(a2_im_ref[:, states], (V7X_SUBLANES, SSM_LANE_STATES))
        for r0 in range(0, bb, V7X_SUBLANES):
            sr = st_re[r0:r0 + V7X_SUBLANES, states]
            si = st_im[r0:r0 + V7X_SUBLANES, states]
            for c in range(pairs):
                rows = slice(c * bb + r0, c * bb + r0 + V7X_SUBLANES)
                pr, pi = sbuf_ref[rows, re], sbuf_ref[rows, im]
                sbuf_ref[rows, re] = sr
                sbuf_ref[rows, im] = si
                sr, si = ar * sr - ai * si + pr, ar * si + ai * sr + pi
            st_re[r0:r0 + V7X_SUBLANES, states] = sr
            st_im[r0:r0 + V7X_SUBLANES, states] = si

    def outputs(j):
        cols = slice(j * V7X_LANES, (j + 1) * V7X_LANES)
        entering = sbuf_ref[:, j * SSM_PAIR_STATES:(j + 1) * SSM_PAIR_STATES].astype(BF16)
        y = _dot(entering, out_ref[j]) + _dot(xj_ref[j], direct_ref[j])
        for t in range(2):
            yt = y[:, t * V7X_LANES:(t + 1) * V7X_LANES] + d_ref[:, cols] * u_ref[t, :, cols]
            z_ref[t, :, cols] = jax.nn.gelu(yt).astype(BF16)

    for k in range(SSM_N_LANE_BLOCKS + 2):
        if 0 <= k - 2:
            outputs(k - 2)
        if k < SSM_N_LANE_BLOCKS:
            contributions(k)
        if 0 <= k - 1 < SSM_N_LANE_BLOCKS:
            scan(k - 1)
        yield


def _glu_pieces(z_ref, x_ref, wglu_ref, o_ref, *, pairs, bb):
    for t in range(2):
        for n0 in range(0, D_MODEL, GLU_COLS):
            cols = slice(n0, n0 + GLU_COLS)
            a = _dot(z_ref[t], wglu_ref[:, cols])
            gate = _dot(z_ref[t], wglu_ref[:, D_MODEL + n0:D_MODEL + n0 + GLU_COLS])
            o_ref[:, t, :, cols] = x_ref[:, t, :, cols] + (a * jax.nn.sigmoid(gate)).reshape(pairs, bb, GLU_COLS)
            yield


def _ssm_kernel(x_ref, x_lag_ref, h0_re_ref, h0_im_ref, g_ref, a2_re_ref, a2_im_ref, in_ref, out_ref, direct_ref,
                d_ref, wglu_ref, o_ref, s_re_ref, s_im_ref,
                st_re, st_im, u_ref, xj_ref, sbuf_ref, z_even, z_odd, *, pairs, bb, nt, lag):
    i = pl.program_id(1)

    @pl.when(i == 0)
    def _():
        st_re[...] = h0_re_ref[...]
        st_im[...] = h0_im_ref[...]
        if lag:
            z_odd[...] = jnp.zeros(z_odd.shape, z_odd.dtype)

    def mixer(z_ref):
        return _ssm_mixer_pieces(x_ref, z_ref, g_ref[...], a2_re_ref, a2_im_ref, in_ref, out_ref, direct_ref,
                                 d_ref, st_re, st_im, u_ref, xj_ref, sbuf_ref, pairs=pairs, bb=bb)

    def glu(z_ref):
        return _glu_pieces(z_ref, x_lag_ref, wglu_ref, o_ref, pairs=pairs, bb=bb)

    if lag:
        @pl.when(i % 2 == 0)
        def _():
            _interleave(mixer(z_even), glu(z_odd))

        @pl.when(i % 2 == 1)
        def _():
            _interleave(mixer(z_odd), glu(z_even))
    else:
        _interleave(mixer(z_even))
        _interleave(glu(z_even))

    @pl.when(i == nt - 1)
    def _():
        s_re_ref[...] = st_re[...]
        s_im_ref[...] = st_im[...]


def _ssm_layer(x, h0_re, h0_im, prm):
    t_len, b_len, _ = x.shape
    tt, bb = _tiles(b_len, t_len, SSM_ROWS)
    assert tt % 2 == 0
    pairs = tt // 2
    m = pairs * bb
    nt = t_len // tt
    lag = 1 if nt > 1 else 0
    block = (pairs, 2, bb, D_MODEL)
    st_spec = pl.BlockSpec((bb, SSM_FLAT), lambda j, i: (j, 0))
    lagged = pl.BlockSpec(block, lambda j, i: (jnp.maximum(i - lag, 0), 0, j, 0))
    z_buf = pltpu.VMEM((2, m, D_MODEL), BF16)
    x4 = x.reshape(t_len // 2, 2, b_len, D_MODEL)
    out, s_re, s_im = pl.pallas_call(
        functools.partial(_ssm_kernel, pairs=pairs, bb=bb, nt=nt, lag=lag),
        grid=(b_len // bb, nt + lag),
        in_specs=[
            pl.BlockSpec(block, lambda j, i: (jnp.minimum(i, nt - 1), 0, j, 0)),
            lagged,
            st_spec, st_spec,
            _layer_spec((1, D_MODEL), 1),
            _const_spec((1, SSM_FLAT)), _const_spec((1, SSM_FLAT)),
            _const_spec(prm['ssm_in'].shape), _const_spec(prm['ssm_out'].shape),
            _const_spec(prm['ssm_direct'].shape),
            _const_spec((1, D_MODEL)),
            _const_spec(prm['ssm_w_glu'].shape),
        ],
        out_specs=[lagged, st_spec, st_spec],
        out_shape=[
            jax.ShapeDtypeStruct(x4.shape, F32),
            jax.ShapeDtypeStruct((b_len, SSM_FLAT), F32),
            jax.ShapeDtypeStruct((b_len, SSM_FLAT), F32),
        ],
        scratch_shapes=[
            pltpu.VMEM((bb, SSM_FLAT), F32), pltpu.VMEM((bb, SSM_FLAT), F32),
            pltpu.VMEM((2, m, D_MODEL), F32),
            pltpu.VMEM((SSM_N_LANE_BLOCKS, m, SSM_PAIR_COLS), BF16),
            pltpu.VMEM((m, SSM_N_LANE_BLOCKS * SSM_PAIR_STATES), F32),
            z_buf, z_buf,
        ],
        compiler_params=_params(),
        name="ssm_mixer",
    )(x4, x4, h0_re, h0_im, prm['g_mix'], prm['a2_re'], prm['a2_im'],
      prm['ssm_in'], prm['ssm_out'], prm['ssm_direct'], prm['ssm_d'], prm['ssm_w_glu'])
    return out.reshape(t_len, b_len, D_MODEL), s_re, s_im


def _ffn1_kernel(x_ref, p_ref, g_ffn_ref, w_gate_ref, w_up_ref, w_down_ref, g_ple_ref, ple_in_ref, ple_gate_ref,
                 g_final_ref, o_ref, h_in_ref, rows_ref, h_ref, pt_ref, act_ref, *, tt, bb):
    rows_ref[...] = x_ref[...].reshape(tt * bb, D_MODEL)
    h_in_ref[...] = _rmsnorm(rows_ref[...], g_ffn_ref[...]).astype(BF16)

    def emit():
        o_ref[...] = _batch_major(_rmsnorm(rows_ref[...], g_final_ref[...]), tt, bb)

    _interleave(_ffn_ple_pieces(rows_ref, h_in_ref, p_ref, w_gate_ref, w_up_ref, w_down_ref,
                                g_ple_ref[...], ple_in_ref, ple_gate_ref, h_ref, pt_ref, act_ref,
                                lambda: None, emit))


def _ffn1_layer(x, p, prm):
    t_len, b_len, _ = x.shape
    tt, bb = _tiles(b_len, t_len, FFN_ROWS)
    return pl.pallas_call(
        functools.partial(_ffn1_kernel, tt=tt, bb=bb),
        grid=(b_len // bb, t_len // tt),
        in_specs=[
            pl.BlockSpec((tt, bb, D_MODEL), lambda j, i: (i, j, 0)),
            pl.BlockSpec((None, bb, tt, p.shape[-1]), lambda j, i: (1, j, i, 0)),
        ] + _ffn_weight_specs(prm, 1) + [_const_spec((1, D_MODEL))],
        out_specs=pl.BlockSpec((bb, tt, D_MODEL), lambda j, i: (j, i, 0)),
        out_shape=jax.ShapeDtypeStruct((b_len, t_len, D_MODEL), F32),
        scratch_shapes=_ffn_scratch(tt * bb, prm['ffn_w_gate'].shape[-1], p.shape[-1]),
        compiler_params=_params(),
        name="layer1_ffn",
    )(x, p, *_ffn_weights(prm), prm['g_final'])


def _run_trunk(x, p, start, pool_state, ssm_re, ssm_im, prm):
    b_len = x.shape[0]
    xt, pool_new = _layer0(x, pool_state, p, prm, start=start)
    xt, s_re, s_im = _ssm_layer(xt, ssm_re.reshape(b_len, SSM_FLAT), ssm_im.reshape(b_len, SSM_FLAT), prm)
    y = _ffn1_layer(xt, p, prm)
    s_re = s_re.reshape(1, b_len, SSM_GROUPS, SSM_STATE)
    s_im = s_im.reshape(1, b_len, SSM_GROUPS, SSM_STATE)
    return y, pool_new[None], s_re, s_im


def kernel(x_prompt, x_sample, state_pool, state_ssm_re, state_ssm_im, p_prompt, p_sample, g_mix, g_ffn, g_ple, g_final, pool_w, pool_scale, ssm_lambda_re, ssm_lambda_im, ssm_log_dt, ssm_b_re, ssm_b_im, ssm_c_re, ssm_c_im, ssm_d, ssm_w_glu, ffn_w_gate, ffn_w_up, ffn_w_down, ple_w_in, ple_w_gate):
    depth = g_mix.shape[0]
    a2_re, a2_im, ssm_in, ssm_out, ssm_direct = _ssm_prep(
        ssm_lambda_re[0], ssm_lambda_im[0], ssm_log_dt[0], ssm_b_re[0], ssm_b_im[0], ssm_c_re[0], ssm_c_im[0])
    per_layer_vec = lambda a: a.reshape(depth, 1, D_MODEL)
    prm = dict(
        g_mix=per_layer_vec(g_mix), g_ffn=per_layer_vec(g_ffn), g_ple=per_layer_vec(g_ple),
        g_final=g_final.reshape(1, D_MODEL),
        pool_w=pool_w[0].astype(BF16), pool_scale=pool_scale,
        a2_re=a2_re, a2_im=a2_im,
        ssm_in=ssm_in.astype(BF16), ssm_out=ssm_out.astype(BF16), ssm_direct=ssm_direct.astype(BF16),
        ssm_d=ssm_d, ssm_w_glu=ssm_w_glu[0].astype(BF16),
        ffn_w_gate=ffn_w_gate.astype(BF16), ffn_w_up=ffn_w_up.astype(BF16),
        ffn_w_down=ffn_w_down.astype(BF16),
        ple_w_in=ple_w_in.astype(BF16), ple_w_gate=ple_w_gate.astype(BF16),
    )
    b_p = x_prompt.shape[0]
    zeros_pool = jnp.zeros((b_p, POOL_BUF, D_MODEL), F32)
    zeros_ssm = jnp.zeros((b_p, SSM_GROUPS, SSM_STATE), F32)
    y_p, pool_p, re_p, im_p = _run_trunk(x_prompt, p_prompt, 0, zeros_pool, zeros_ssm, zeros_ssm, prm)
    y_s, pool_s, re_s, im_s = _run_trunk(x_sample, p_sample, PAST_LEN, state_pool[0],
                                         state_ssm_re[0], state_ssm_im[0], prm)
    return (y_p, y_s, pool_p, pool_s, re_p, im_p, re_s, im_s)
```

```python
import functools
import math

import jax
import jax.numpy as jnp
from jax import lax
from jax.experimental import pallas as pl
from jax.experimental.pallas import tpu as pltpu

D_MODEL = 1024
POOL_WINDOWS = (2, 4, 8, 16)
POOL_GROUP_DIM = D_MODEL // len(POOL_WINDOWS)
POOL_BUF = max(POOL_WINDOWS) - 1
SSM_GROUP_DIM = 16
SSM_GROUPS = D_MODEL // SSM_GROUP_DIM
SSM_STATE = 64
SSM_FLAT = SSM_GROUPS * SSM_STATE
PAST_LEN = 16384
EPS = 1e-6

V7X_SUBLANES = 8
V7X_LANES = 128
V7X_MXU_DIM = 256
V7X_VMEM_LIMIT_BYTES = 60 * 1024 * 1024

SSM_COL_BLOCK = V7X_MXU_DIM
SSM_GROUPS_PER_BLOCK = SSM_COL_BLOCK // SSM_GROUP_DIM
SSM_STATE_BLOCK = SSM_GROUPS_PER_BLOCK * SSM_STATE
SSM_N_BLOCKS = D_MODEL // SSM_COL_BLOCK
SSM_LANE_GROUPS = V7X_LANES // SSM_GROUP_DIM
SSM_LANE_STATES = SSM_LANE_GROUPS * SSM_STATE
SSM_N_LANE_BLOCKS = D_MODEL // V7X_LANES
SSM_PAIR_COLS = 2 * V7X_LANES
SSM_PAIR_STATES = 2 * SSM_LANE_STATES

POOL_LAYER_ROWS = 512
SSM_ROWS = 512
FFN_ROWS = 1024
CAST_ROWS = 512
FFN_COLS = 2 * V7X_MXU_DIM
OUT_COLS = 2 * V7X_MXU_DIM

BF16 = jnp.bfloat16
F32 = jnp.float32


def _rmsnorm(x, g):
    return x * lax.rsqrt(jnp.mean(x * x, axis=-1, keepdims=True) + EPS) * g


def _dot(a, b):
    return jnp.dot(a, b, preferred_element_type=F32)


def _dot_nt_f32(a, b):
    return lax.dot_general(a, b, (((1,), (1,)), ((), ())), precision=lax.Precision.HIGHEST,
                           preferred_element_type=F32)


def _time_major(block):
    bb, tt, c = block.shape
    return jnp.transpose(block, (1, 0, 2)).reshape(tt * bb, c)


def _batch_major(rows, tt, bb):
    return jnp.transpose(rows.reshape(tt, bb, rows.shape[-1]), (1, 0, 2))


def _const_spec(shape):
    zeros = (0,) * len(shape)
    return pl.BlockSpec(shape, lambda *_: zeros, pipeline_mode=pl.Buffered(1))


def _layer_spec(shape, layer):
    tail = (0,) * len(shape)
    return pl.BlockSpec((None,) + tuple(shape), lambda *_: (layer,) + tail,
                        pipeline_mode=pl.Buffered(1))


def _params():
    return pltpu.CompilerParams(dimension_semantics=("arbitrary", "arbitrary"),
                                vmem_limit_bytes=V7X_VMEM_LIMIT_BYTES)


def _tiles(b_len, t_len, rows):
    bb = min(b_len, max(V7X_SUBLANES, rows // t_len))
    tt = min(t_len, rows // bb)
    assert t_len % tt == 0 and b_len % bb == 0 and bb % V7X_SUBLANES == 0
    return tt, bb


class _Schedule:
    def __init__(self, b_len, t_len, rows):
        self.tt, self.bb = _tiles(b_len, t_len, rows)
        self.nt = t_len // self.tt
        self.lag = 1 if self.nt > 1 else 0
        self.grid = (b_len // self.bb, self.nt + self.lag)

    def mixer_block(self, i):
        return jnp.minimum(i, self.nt - 1)

    def ffn_block(self, i):
        return jnp.maximum(i - self.lag, 0)


def _to_bf16_kernel(w_ref, o_ref):
    o_ref[...] = w_ref[...].astype(BF16)


def _to_bf16(w):
    layers, k, n = w.shape
    rows = math.gcd(k, CAST_ROWS)
    spec = pl.BlockSpec((None, rows, n), lambda l, i: (l, i, 0))
    return pl.pallas_call(
        _to_bf16_kernel,
        grid=(layers, k // rows),
        in_specs=[spec], out_specs=spec,
        out_shape=jax.ShapeDtypeStruct(w.shape, BF16),
        compiler_params=_params(),
        name="to_bf16",
    )(w)


def _ssm_prep_kernel(lam_re_ref, lam_im_ref, log_dt_ref, b_re_ref, b_im_ref, c_re_ref, c_im_ref,
                     a2_re_ref, a2_im_ref, in_ref, out_ref, direct_ref,
                     lb_re_row, lb_im_row, wb_re, wb_im, wc_re, wc_im):
    lam_re = lam_re_ref[...]
    lam_im = lam_im_ref[...]
    dt = jnp.exp(log_dt_ref[...])
    mag = jnp.exp(lam_re * dt)
    lb_re = mag * jnp.cos(lam_im * dt)
    lb_im = mag * jnp.sin(lam_im * dt)
    den = lam_re * lam_re + lam_im * lam_im
    f_re = ((lb_re - 1.0) * lam_re + lb_im * lam_im) / den
    f_im = (lb_im * lam_re - (lb_re - 1.0) * lam_im) / den
    b_re = jnp.swapaxes(b_re_ref[...], 1, 2)
    b_im = jnp.swapaxes(b_im_ref[...], 1, 2)
    fr = f_re[:, None, :]
    fi = f_im[:, None, :]
    bb_re = fr * b_re - fi * b_im
    bb_im = fr * b_im + fi * b_re
    c_re = c_re_ref[...]
    c_im_neg = -c_im_ref[...]

    for ref in (wb_re, wb_im, wc_re, wc_im):
        ref[...] = jnp.zeros(ref.shape, ref.dtype)
    for g in range(SSM_GROUPS):
        cb, gl = divmod(g, SSM_GROUPS_PER_BLOCK)
        cols = slice(gl * SSM_GROUP_DIM, (gl + 1) * SSM_GROUP_DIM)
        states = slice(gl * SSM_STATE, (gl + 1) * SSM_STATE)
        wb_re[cb, cols, states] = bb_re[g]
        wb_im[cb, cols, states] = bb_im[g]
        wc_re[cb, cols, states] = c_re[g]
        wc_im[cb, cols, states] = c_im_neg[g]
        flat = slice(g * SSM_STATE, (g + 1) * SSM_STATE)
        lb_re_row[:, flat] = lb_re[g:g + 1, :]
        lb_im_row[:, flat] = lb_im[g:g + 1, :]

    lr_all = lb_re_row[...]
    li_all = lb_im_row[...]
    a2_re_ref[...] = lr_all * lr_all - li_all * li_all
    a2_im_ref[...] = 2.0 * lr_all * li_all

    halves = SSM_COL_BLOCK // V7X_LANES
    for cb in range(SSM_N_BLOCKS):
        flat = slice(cb * SSM_STATE_BLOCK, (cb + 1) * SSM_STATE_BLOCK)
        lr, li = lb_re_row[:, flat], lb_im_row[:, flat]
        l2r, l2i = a2_re_ref[:, flat], a2_im_ref[:, flat]
        w_r, w_i = wb_re[cb], wb_im[cb]
        wl_r, wl_i = w_r * lr - w_i * li, w_r * li + w_i * lr
        c_r, c_n = wc_re[cb], wc_im[cb]
        o1_r, o1_i = c_r * lr + c_n * li, c_n * lr - c_r * li
        o2_r, o2_i = c_r * l2r + c_n * l2i, c_n * l2r - c_r * l2i
        for half in range(halves):
            j = cb * halves + half
            r = slice(half * V7X_LANES, (half + 1) * V7X_LANES)
            s = slice(half * SSM_LANE_STATES, (half + 1) * SSM_LANE_STATES)
            t0, t1 = slice(0, V7X_LANES), slice(V7X_LANES, SSM_PAIR_COLS)
            re, im = slice(0, SSM_LANE_STATES), slice(SSM_LANE_STATES, SSM_PAIR_STATES)
            in_ref[j, t0, re] = wl_r[r, s].astype(BF16)
            in_ref[j, t0, im] = wl_i[r, s].astype(BF16)
            in_ref[j, t1, re] = w_r[r, s].astype(BF16)
            in_ref[j, t1, im] = w_i[r, s].astype(BF16)
            k0 = _dot_nt_f32(w_r[r, s], c_r[r, s]) + _dot_nt_f32(w_i[r, s], c_n[r, s])
            k1 = _dot_nt_f32(wl_r[r, s], c_r[r, s]) + _dot_nt_f32(wl_i[r, s], c_n[r, s])
            direct_ref[j, t0, t0] = k0.astype(BF16)
            direct_ref[j, t0, t1] = k1.astype(BF16)
            direct_ref[j, t1, t0] = jnp.zeros((V7X_LANES, V7X_LANES), BF16)
            direct_ref[j, t1, t1] = k0.astype(BF16)
            out_ref[j, re, t0] = o1_r[r, s].T.astype(BF16)
            out_ref[j, im, t0] = o1_i[r, s].T.astype(BF16)
            out_ref[j, re, t1] = o2_r[r, s].T.astype(BF16)
            out_ref[j, im, t1] = o2_i[r, s].T.astype(BF16)


def _ssm_prep(lam_re, lam_im, log_dt, b_re, b_im, c_re, c_im):
    g, p, _ = b_re.shape
    row = jax.ShapeDtypeStruct((1, SSM_FLAT), F32)
    n = SSM_N_LANE_BLOCKS
    block_diag = pltpu.VMEM((SSM_N_BLOCKS, SSM_COL_BLOCK, SSM_STATE_BLOCK), F32)
    return pl.pallas_call(
        _ssm_prep_kernel,
        out_shape=(row, row,
                   jax.ShapeDtypeStruct((n, SSM_PAIR_COLS, SSM_PAIR_STATES), BF16),
                   jax.ShapeDtypeStruct((n, SSM_PAIR_STATES, SSM_PAIR_COLS), BF16),
                   jax.ShapeDtypeStruct((n, SSM_PAIR_COLS, SSM_PAIR_COLS), BF16)),
        scratch_shapes=[pltpu.VMEM((1, SSM_FLAT), F32), pltpu.VMEM((1, SSM_FLAT), F32),
                        block_diag, block_diag, block_diag, block_diag],
        compiler_params=pltpu.CompilerParams(vmem_limit_bytes=V7X_VMEM_LIMIT_BYTES),
        name="ssm_prep",
    )(lam_re, lam_im, log_dt.reshape(g, 1), b_re, b_im, c_re, c_im)


def _pool_pieces(x_ref, dst_ref, ext_ref, w_ref, scale_ref, g, *, tt, bb, first_pos):
    tm = tt * bb
    halo = POOL_BUF * bb
    dst_ref[...] = _time_major(x_ref[...])
    ext_ref[halo:halo + tm, :] = _rmsnorm(dst_ref[...], g)
    yield
    row = lax.broadcasted_iota(jnp.int32, (tm, 1), 0)
    pos = first_pos + lax.shift_right_logical(row, int(math.log2(bb)))
    for gi, w in enumerate(POOL_WINDOWS):
        cols = slice(gi * POOL_GROUP_DIM, (gi + 1) * POOL_GROUP_DIM)
        acc = ext_ref[halo:halo + tm, cols]
        for k in range(1, w):
            acc = acc + ext_ref[halo - k * bb:halo - k * bb + tm, cols]
        inv_cnt = 1.0 / jnp.minimum(w, pos + 1).astype(F32)
        diff = acc * inv_cnt - ext_ref[halo:halo + tm, cols]
        dst_ref[:, cols] += _dot(diff.astype(BF16), w_ref[gi]) * scale_ref[:, cols]
        yield


def _ffn_ple_pieces(rows_ref, h_in_ref, p_ref, w_gate_ref, w_up_ref, w_down_ref, g_ple, ple_in_ref, ple_gate_ref,
                    h_ref, pt_ref, act_ref, release_h_in, emit):
    d_ff = act_ref.shape[1]
    pt_ref[...] = _time_major(p_ref[...]).astype(BF16)
    for c0 in range(0, d_ff, FFN_COLS):
        cols = slice(c0, min(c0 + FFN_COLS, d_ff))
        gate = _dot(h_in_ref[...], w_gate_ref[:, cols])
        up = _dot(h_in_ref[...], w_up_ref[:, cols])
        act_ref[:, cols] = (jax.nn.silu(gate) * up).astype(BF16)
        if cols.stop == d_ff:
            release_h_in()
        yield
    for n0 in range(0, D_MODEL, OUT_COLS):
        cols = slice(n0, n0 + OUT_COLS)
        rows_ref[:, cols] += _dot(act_ref[...], w_down_ref[:, cols])
        yield
    h_ref[...] = _rmsnorm(rows_ref[...], g_ple).astype(BF16)
    for n0 in range(0, D_MODEL, OUT_COLS):
        cols = slice(n0, n0 + OUT_COLS)
        ple_gate = jax.nn.sigmoid(_dot(h_ref[...], ple_gate_ref[:, cols]))
        rows_ref[:, cols] += _dot(pt_ref[...], ple_in_ref[:, cols]) * ple_gate
        yield
    emit()
    yield


_DONE = object()


def _interleave(*stages):
    live = list(stages)
    while live:
        for stage in list(live):
            if next(stage, _DONE) is _DONE:
                live.remove(stage)


def _then(stage, last_piece):
    yield from stage
    last_piece()
    yield


def _run_halves(sched, g_ffn, mid_ref, hmid_ref, rows_ref, mixer, ffn):
    h_in_reads_issued = []

    def mixer_into(dst_ref):
        def norm_for_ffn():
            assert h_in_reads_issued or not sched.lag, "hmid_ref rewritten before the FFN half read it"
            hmid_ref[...] = _rmsnorm(dst_ref[...], g_ffn).astype(BF16)
        return _then(mixer(dst_ref), norm_for_ffn)

    def release_h_in():
        h_in_reads_issued.append(True)

    if sched.lag:
        rows_ref[...] = mid_ref[...]
        _interleave(ffn(hmid_ref, release_h_in), mixer_into(mid_ref))
    else:
        _interleave(mixer_into(rows_ref))
        _interleave(ffn(hmid_ref, release_h_in))


def _init_handoff(sched, mid_ref, hmid_ref):
    if sched.lag:
        mid_ref[...] = jnp.zeros(mid_ref.shape, mid_ref.dtype)
        hmid_ref[...] = jnp.zeros(hmid_ref.shape, hmid_ref.dtype)


def _ffn_scratch(tm, d_ff, ple_dim):
    return [
        pltpu.VMEM((tm, D_MODEL), BF16),
        pltpu.VMEM((tm, D_MODEL), F32),
        pltpu.VMEM((tm, D_MODEL), BF16),
        pltpu.VMEM((tm, ple_dim), BF16),
        pltpu.VMEM((tm, d_ff), BF16),
    ]


def _layer0_kernel(x_ref, prev_ref, p_ref, g_mix_ref, pool_w_ref, pool_scale_ref,
                   g_ffn_ref, w_gate_ref, w_up_ref, w_down_ref, g_ple_ref, ple_in_ref, ple_gate_ref,
                   o_ref, state_ref, ext_ref, mid_ref, hmid_ref, rows_ref, h_ref, pt_ref, act_ref,
                   *, sched, start):
    tt, bb = sched.tt, sched.bb
    tm = tt * bb
    halo = POOL_BUF * bb
    i = pl.program_id(1)

    @pl.when(i == 0)
    def _():
        ext_ref[0:halo, :] = _time_major(prev_ref[...])
        _init_handoff(sched, mid_ref, hmid_ref)

    @pl.when(i > 0)
    def _():
        ext_ref[0:halo, :] = ext_ref[tm:tm + halo, :]

    def mixer(dst_ref):
        return _pool_pieces(x_ref, dst_ref, ext_ref, pool_w_ref, pool_scale_ref, g_mix_ref[...],
                            tt=tt, bb=bb, first_pos=start + sched.mixer_block(i) * tt)

    def emit():
        o_ref[...] = rows_ref[...].reshape(tt, bb, D_MODEL)

    def ffn(h_in_ref, release_h_in):
        return _ffn_ple_pieces(rows_ref, h_in_ref, p_ref, w_gate_ref, w_up_ref, w_down_ref,
                               g_ple_ref[...], ple_in_ref, ple_gate_ref, h_ref, pt_ref, act_ref,
                               release_h_in, emit)

    _run_halves(sched, g_ffn_ref[...], mid_ref, hmid_ref, rows_ref, mixer, ffn)

    @pl.when(i == sched.nt - 1)
    def _():
        state_ref[...] = _batch_major(ext_ref[tm:tm + halo, :], POOL_BUF, bb)


def _ffn_weight_specs(prm, layer):
    return [
        _layer_spec((1, D_MODEL), layer),
        _layer_spec(prm['ffn_w_gate'].shape[1:], layer), _layer_spec(prm['ffn_w_up'].shape[1:], layer),
        _layer_spec(prm['ffn_w_down'].shape[1:], layer),
        _layer_spec((1, D_MODEL), layer),
        _layer_spec(prm['ple_w_in'].shape[1:], layer), _layer_spec(prm['ple_w_gate'].shape[1:], layer),
    ]


def _ffn_weights(prm):
    return (prm['g_ffn'], prm['ffn_w_gate'], prm['ffn_w_up'], prm['ffn_w_down'],
            prm['g_ple'], prm['ple_w_in'], prm['ple_w_gate'])


def _layer0(x, prev, p, prm, *, start):
    b_len, t_len, _ = x.shape
    sched = _Schedule(b_len, t_len, POOL_LAYER_ROWS)
    tt, bb = sched.tt, sched.bb
    tm = tt * bb
    state_spec = pl.BlockSpec((None, bb, POOL_BUF, D_MODEL), lambda j, i: (0, j, 0, 0))
    mid_rows = tm if sched.lag else 2 * V7X_SUBLANES
    return pl.pallas_call(
        functools.partial(_layer0_kernel, sched=sched, start=start),
        grid=sched.grid,
        in_specs=[
            pl.BlockSpec((bb, tt, D_MODEL), lambda j, i: (j, sched.mixer_block(i), 0)),
            state_spec,
            pl.BlockSpec((None, bb, tt, p.shape[-1]), lambda j, i: (0, j, sched.ffn_block(i), 0)),
            _layer_spec((1, D_MODEL), 0),
            _const_spec(prm['pool_w'].shape),
            _const_spec((1, D_MODEL)),
        ] + _ffn_weight_specs(prm, 0),
        out_specs=[pl.BlockSpec((tt, bb, D_MODEL), lambda j, i: (sched.ffn_block(i), j, 0)), state_spec],
        out_shape=[
            jax.ShapeDtypeStruct((t_len, b_len, D_MODEL), F32),
            jax.ShapeDtypeStruct((1, b_len, POOL_BUF, D_MODEL), F32),
        ],
        scratch_shapes=([pltpu.VMEM(((POOL_BUF + tt) * bb, D_MODEL), F32),
                         pltpu.VMEM((mid_rows, D_MODEL), F32)]
                        + _ffn_scratch(tm, prm['ffn_w_gate'].shape[-1], p.shape[-1])),
        compiler_params=_params(),
        name="layer0_pool_ffn",
    )(x, prev, p, prm['g_mix'], prm['pool_w'], prm['pool_scale'], *_ffn_weights(prm))


def _ssm_kernel(x_ref, h0_re_ref, h0_im_ref, g_ref, a2_re_ref, a2_im_ref, in_ref, out_ref, direct_ref,
                d_ref, wglu_ref, o_ref, s_re_ref, s_im_ref,
                st_re, st_im, u_ref, xj_ref, sbuf_ref, z_ref, *, pairs, bb):
    m = pairs * bb
    i = pl.program_id(1)

    @pl.when(i == 0)
    def _():
        st_re[...] = h0_re_ref[...]
        st_im[...] = h0_im_ref[...]

    for t in range(2):
        u_ref[t] = _rmsnorm(x_ref[:, t].reshape(m, D_MODEL), g_ref[...])

    def contributions(j):
        cols = slice(j * V7X_LANES, (j + 1) * V7X_LANES)
        both = jnp.concatenate([u_ref[0, :, cols], u_ref[1, :, cols]], axis=-1)
        xj_ref[j] = both.astype(BF16)
        sbuf_ref[:, j * SSM_PAIR_STATES:(j + 1) * SSM_PAIR_STATES] = _dot(xj_ref[j], in_ref[j])

    def scan(j):
        states = slice(j * SSM_LANE_STATES, (j + 1) * SSM_LANE_STATES)
        re = slice(j * SSM_PAIR_STATES, j * SSM_PAIR_STATES + SSM_LANE_STATES)
        im = slice(j * SSM_PAIR_STATES + SSM_LANE_STATES, (j + 1) * SSM_PAIR_STATES)
        ar = jnp.broadcast_to(a2_re_ref[:, states], (V7X_SUBLANES, SSM_LANE_STATES))
        ai = jnp.broadcast_to(a2_im_ref[:, states], (V7X_SUBLANES, SSM_LANE_STATES))
        for r0 in range(0, bb, V7X_SUBLANES):
            sr = st_re[r0:r0 + V7X_SUBLANES, states]
            si = st_im[r0:r0 + V7X_SUBLANES, states]
            for c in range(pairs):
                rows = slice(c * bb + r0, c * bb + r0 + V7X_SUBLANES)
                pr, pi = sbuf_ref[rows, re], sbuf_ref[rows, im]
                sbuf_ref[rows, re] = sr
                sbuf_ref[rows, im] = si
                sr, si = ar * sr - ai * si + pr, ar * si + ai * sr + pi
            st_re[r0:r0 + V7X_SUBLANES, states] = sr
            st_im[r0:r0 + V7X_SUBLANES, states] = si

    def outputs(j):
        cols = slice(j * V7X_LANES, (j + 1) * V7X_LANES)
        entering = sbuf_ref[:, j * SSM_PAIR_STATES:(j + 1) * SSM_PAIR_STATES].astype(BF16)
        y = _dot(entering, out_ref[j]) + _dot(xj_ref[j], direct_ref[j])
        for t in range(2):
            yt = y[:, t * V7X_LANES:(t + 1) * V7X_LANES] + d_ref[:, cols] * u_ref[t, :, cols]
            z_ref[t, :, cols] = jax.nn.gelu(yt).astype(BF16)

    for k in range(SSM_N_LANE_BLOCKS + 2):
        if 0 <= k - 2:
            outputs(k - 2)
        if k < SSM_N_LANE_BLOCKS:
            contributions(k)
        if 0 <= k - 1 < SSM_N_LANE_BLOCKS:
            scan(k - 1)

    for t in range(2):
        for n0 in range(0, D_MODEL, OUT_COLS):
            cols = slice(n0, n0 + OUT_COLS)
            a = _dot(z_ref[t], wglu_ref[:, cols])
            gate = _dot(z_ref[t], wglu_ref[:, D_MODEL + n0:D_MODEL + n0 + OUT_COLS])
            o_ref[:, t, :, cols] = x_ref[:, t, :, cols] + (a * jax.nn.sigmoid(gate)).reshape(pairs, bb, OUT_COLS)

    s_re_ref[...] = st_re[...]
    s_im_ref[...] = st_im[...]


def _ssm_layer(x, h0_re, h0_im, prm):
    t_len, b_len, _ = x.shape
    tt, bb = _tiles(b_len, t_len, SSM_ROWS)
    assert tt % 2 == 0
    pairs = tt // 2
    m = pairs * bb
    x_spec = pl.BlockSpec((pairs, 2, bb, D_MODEL), lambda j, i: (i, 0, j, 0))
    st_spec = pl.BlockSpec((bb, SSM_FLAT), lambda j, i: (j, 0))
    out, s_re, s_im = pl.pallas_call(
        functools.partial(_ssm_kernel, pairs=pairs, bb=bb),
        grid=(b_len // bb, t_len // tt),
        in_specs=[
            x_spec, st_spec, st_spec,
            _layer_spec((1, D_MODEL), 1),
            _const_spec((1, SSM_FLAT)), _const_spec((1, SSM_FLAT)),
            _const_spec(prm['ssm_in'].shape), _const_spec(prm['ssm_out'].shape),
            _const_spec(prm['ssm_direct'].shape),
            _const_spec((1, D_MODEL)),
            _const_spec(prm['ssm_w_glu'].shape),
        ],
        out_specs=[x_spec, st_spec, st_spec],
        out_shape=[
            jax.ShapeDtypeStruct((t_len // 2, 2, b_len, D_MODEL), F32),
            jax.ShapeDtypeStruct((b_len, SSM_FLAT), F32),
            jax.ShapeDtypeStruct((b_len, SSM_FLAT), F32),
        ],
        scratch_shapes=[
            pltpu.VMEM((bb, SSM_FLAT), F32), pltpu.VMEM((bb, SSM_FLAT), F32),
            pltpu.VMEM((2, m, D_MODEL), F32),
            pltpu.VMEM((SSM_N_LANE_BLOCKS, m, SSM_PAIR_COLS), BF16),
            pltpu.VMEM((m, SSM_N_LANE_BLOCKS * SSM_PAIR_STATES), F32),
            pltpu.VMEM((2, m, D_MODEL), BF16),
        ],
        compiler_params=_params(),
        name="ssm_mixer",
    )(x.reshape(t_len // 2, 2, b_len, D_MODEL), h0_re, h0_im, prm['g_mix'], prm['a2_re'], prm['a2_im'],
      prm['ssm_in'], prm['ssm_out'], prm['ssm_direct'], prm['ssm_d'], prm['ssm_w_glu'])
    return out.reshape(t_len, b_len, D_MODEL), s_re, s_im


def _ffn1_kernel(x_ref, p_ref, g_ffn_ref, w_gate_ref, w_up_ref, w_down_ref, g_ple_ref, ple_in_ref, ple_gate_ref,
                 g_final_ref, o_ref, h_in_ref, rows_ref, h_ref, pt_ref, act_ref, *, tt, bb):
    rows_ref[...] = x_ref[...].reshape(tt * bb, D_MODEL)
    h_in_ref[...] = _rmsnorm(rows_ref[...], g_ffn_ref[...]).astype(BF16)

    def emit():
        o_ref[...] = _batch_major(_rmsnorm(rows_ref[...], g_final_ref[...]), tt, bb)

    _interleave(_ffn_ple_pieces(rows_ref, h_in_ref, p_ref, w_gate_ref, w_up_ref, w_down_ref,
                                g_ple_ref[...], ple_in_ref, ple_gate_ref, h_ref, pt_ref, act_ref,
                                lambda: None, emit))


def _ffn1_layer(x, p, prm):
    t_len, b_len, _ = x.shape
    tt, bb = _tiles(b_len, t_len, FFN_ROWS)
    return pl.pallas_call(
        functools.partial(_ffn1_kernel, tt=tt, bb=bb),
        grid=(b_len // bb, t_len // tt),
        in_specs=[
            pl.BlockSpec((tt, bb, D_MODEL), lambda j, i: (i, j, 0)),
            pl.BlockSpec((None, bb, tt, p.shape[-1]), lambda j, i: (1, j, i, 0)),
        ] + _ffn_weight_specs(prm, 1) + [_const_spec((1, D_MODEL))],
        out_specs=pl.BlockSpec((bb, tt, D_MODEL), lambda j, i: (j, i, 0)),
        out_shape=jax.ShapeDtypeStruct((b_len, t_len, D_MODEL), F32),
        scratch_shapes=_ffn_scratch(tt * bb, prm['ffn_w_gate'].shape[-1], p.shape[-1]),
        compiler_params=_params(),
        name="layer1_ffn",
    )(x, p, *_ffn_weights(prm), prm['g_final'])


def _run_trunk(x, p, start, pool_state, ssm_re, ssm_im, prm):
    b_len = x.shape[0]
    xt, pool_new = _layer0(x, pool_state, p, prm, start=start)
    xt, s_re, s_im = _ssm_layer(xt, ssm_re.reshape(b_len, SSM_FLAT), ssm_im.reshape(b_len, SSM_FLAT), prm)
    y = _ffn1_layer(xt, p, prm)
    s_re = s_re.reshape(1, b_len, SSM_GROUPS, SSM_STATE)
    s_im = s_im.reshape(1, b_len, SSM_GROUPS, SSM_STATE)
    return y, pool_new, s_re, s_im


def kernel(x_prompt, x_sample, state_pool, state_ssm_re, state_ssm_im, p_prompt, p_sample, g_mix, g_ffn, g_ple, g_final, pool_w, pool_scale, ssm_lambda_re, ssm_lambda_im, ssm_log_dt, ssm_b_re, ssm_b_im, ssm_c_re, ssm_c_im, ssm_d, ssm_w_glu, ffn_w_gate, ffn_w_up, ffn_w_down, ple_w_in, ple_w_gate):
    depth = g_mix.shape[0]
    a2_re, a2_im, ssm_in, ssm_out, ssm_direct = _ssm_prep(
        ssm_lambda_re[0], ssm_lambda_im[0], ssm_log_dt[0], ssm_b_re[0], ssm_b_im[0], ssm_c_re[0], ssm_c_im[0])
    per_layer_vec = lambda a: a.reshape(depth, 1, D_MODEL)
    prm = dict(
        g_mix=per_layer_vec(g_mix), g_ffn=per_layer_vec(g_ffn), g_ple=per_layer_vec(g_ple),
        g_final=g_final.reshape(1, D_MODEL),
        pool_w=pool_w[0].astype(BF16), pool_scale=pool_scale,
        a2_re=a2_re, a2_im=a2_im,
        ssm_in=ssm_in, ssm_out=ssm_out, ssm_direct=ssm_direct,
        ssm_d=ssm_d, ssm_w_glu=ssm_w_glu[0].astype(BF16),
        ffn_w_gate=_to_bf16(ffn_w_gate), ffn_w_up=_to_bf16(ffn_w_up), ffn_w_down=_to_bf16(ffn_w_down),
        ple_w_in=_to_bf16(ple_w_in), ple_w_gate=_to_bf16(ple_w_gate),
    )
    b_p = x_prompt.shape[0]
    zeros_pool = jnp.zeros((1, b_p, POOL_BUF, D_MODEL), F32)
    zeros_ssm = jnp.zeros((b_p, SSM_GROUPS, SSM_STATE), F32)
    y_p, pool_p, re_p, im_p = _run_trunk(x_prompt, p_prompt, 0, zeros_pool, zeros_ssm, zeros_ssm, prm)
    y_s, pool_s, re_s, im_s = _run_trunk(x_sample, p_sample, PAST_LEN, state_pool,
                                         state_ssm_re[0], state_ssm_im[0], prm)
    return (y_p, y_s, pool_p, pool_s, re_p, im_p, re_s, im_s)
```

```python
import functools
import math

import jax
import jax.numpy as jnp
from jax import lax
from jax.experimental import pallas as pl
from jax.experimental.pallas import tpu as pltpu

D_MODEL = 1024
POOL_WINDOWS = (2, 4, 8, 16)
POOL_GROUP_DIM = D_MODEL // len(POOL_WINDOWS)
POOL_BUF = max(POOL_WINDOWS) - 1
SSM_GROUP_DIM = 16
SSM_GROUPS = D_MODEL // SSM_GROUP_DIM
SSM_STATE = 64
SSM_FLAT = SSM_GROUPS * SSM_STATE
PAST_LEN = 16384
EPS = 1e-6

V7X_SUBLANES = 8
V7X_LANES = 128
V7X_MXU_DIM = 256
V7X_VMEM_LIMIT_BYTES = 60 * 1024 * 1024

SSM_COL_BLOCK = V7X_MXU_DIM
SSM_GROUPS_PER_BLOCK = SSM_COL_BLOCK // SSM_GROUP_DIM
SSM_STATE_BLOCK = SSM_GROUPS_PER_BLOCK * SSM_STATE
SSM_N_BLOCKS = D_MODEL // SSM_COL_BLOCK
SSM_LANE_GROUPS = V7X_LANES // SSM_GROUP_DIM
SSM_LANE_STATES = SSM_LANE_GROUPS * SSM_STATE
SSM_N_LANE_BLOCKS = D_MODEL // V7X_LANES
SSM_PAIR_COLS = 2 * V7X_LANES
SSM_PAIR_STATES = 2 * SSM_LANE_STATES

POOL_LAYER_ROWS = 512
SSM_ROWS = 512
FFN_ROWS = 1024
CAST_BLOCK_BYTES = 6 * 1024 * 1024
FFN_COLS = 2 * V7X_MXU_DIM
OUT_COLS = 2 * V7X_MXU_DIM

BF16 = jnp.bfloat16
F32 = jnp.float32


def _rmsnorm(x, g):
    return x * lax.rsqrt(jnp.mean(x * x, axis=-1, keepdims=True) + EPS) * g


def _dot(a, b):
    return jnp.dot(a, b, preferred_element_type=F32)


def _dot_nt_f32(a, b):
    return lax.dot_general(a, b, (((1,), (1,)), ((), ())), precision=lax.Precision.HIGHEST,
                           preferred_element_type=F32)


def _time_major(block):
    bb, tt, c = block.shape
    return jnp.transpose(block, (1, 0, 2)).reshape(tt * bb, c)


def _batch_major(rows, tt, bb):
    return jnp.transpose(rows.reshape(tt, bb, rows.shape[-1]), (1, 0, 2))


def _const_spec(shape):
    zeros = (0,) * len(shape)
    return pl.BlockSpec(shape, lambda *_: zeros, pipeline_mode=pl.Buffered(1))


def _layer_spec(shape, layer):
    tail = (0,) * len(shape)
    return pl.BlockSpec((None,) + tuple(shape), lambda *_: (layer,) + tail,
                        pipeline_mode=pl.Buffered(1))


def _params():
    return pltpu.CompilerParams(dimension_semantics=("arbitrary", "arbitrary"),
                                vmem_limit_bytes=V7X_VMEM_LIMIT_BYTES)


def _tiles(b_len, t_len, rows):
    bb = min(b_len, max(V7X_SUBLANES, rows // t_len))
    tt = min(t_len, rows // bb)
    assert t_len % tt == 0 and b_len % bb == 0 and bb % V7X_SUBLANES == 0
    return tt, bb


class _Schedule:
    def __init__(self, b_len, t_len, rows):
        self.tt, self.bb = _tiles(b_len, t_len, rows)
        self.nt = t_len // self.tt
        self.lag = 1 if self.nt > 1 else 0
        self.grid = (b_len // self.bb, self.nt + self.lag)

    def mixer_block(self, i):
        return jnp.minimum(i, self.nt - 1)

    def ffn_block(self, i):
        return jnp.maximum(i - self.lag, 0)


def _to_bf16_kernel(w_ref, o_ref):
    o_ref[...] = w_ref[...].astype(BF16)


def _to_bf16(w):
    layers, k, n = w.shape
    packed_rows = 2 * V7X_SUBLANES
    rows = max(r for r in range(packed_rows, k + 1, packed_rows)
               if k % r == 0 and r * n * w.dtype.itemsize <= CAST_BLOCK_BYTES)
    spec = pl.BlockSpec((None, rows, n), lambda l, i: (l, i, 0))
    return pl.pallas_call(
        _to_bf16_kernel,
        grid=(layers, k // rows),
        in_specs=[spec], out_specs=spec,
        out_shape=jax.ShapeDtypeStruct(w.shape, BF16),
        compiler_params=_params(),
        name="to_bf16",
    )(w)


def _ssm_prep_kernel(lam_re_ref, lam_im_ref, log_dt_ref, b_re_ref, b_im_ref, c_re_ref, c_im_ref,
                     a2_re_ref, a2_im_ref, in_ref, out_ref, direct_ref,
                     lb_re_row, lb_im_row, wb_re, wb_im, wc_re, wc_im):
    lam_re = lam_re_ref[...]
    lam_im = lam_im_ref[...]
    dt = jnp.exp(log_dt_ref[...])
    mag = jnp.exp(lam_re * dt)
    lb_re = mag * jnp.cos(lam_im * dt)
    lb_im = mag * jnp.sin(lam_im * dt)
    den = lam_re * lam_re + lam_im * lam_im
    f_re = ((lb_re - 1.0) * lam_re + lb_im * lam_im) / den
    f_im = (lb_im * lam_re - (lb_re - 1.0) * lam_im) / den
    b_re = jnp.swapaxes(b_re_ref[...], 1, 2)
    b_im = jnp.swapaxes(b_im_ref[...], 1, 2)
    fr = f_re[:, None, :]
    fi = f_im[:, None, :]
    bb_re = fr * b_re - fi * b_im
    bb_im = fr * b_im + fi * b_re
    c_re = c_re_ref[...]
    c_im_neg = -c_im_ref[...]

    for ref in (wb_re, wb_im, wc_re, wc_im):
        ref[...] = jnp.zeros(ref.shape, ref.dtype)
    for g in range(SSM_GROUPS):
        cb, gl = divmod(g, SSM_GROUPS_PER_BLOCK)
        cols = slice(gl * SSM_GROUP_DIM, (gl + 1) * SSM_GROUP_DIM)
        states = slice(gl * SSM_STATE, (gl + 1) * SSM_STATE)
        wb_re[cb, cols, states] = bb_re[g]
        wb_im[cb, cols, states] = bb_im[g]
        wc_re[cb, cols, states] = c_re[g]
        wc_im[cb, cols, states] = c_im_neg[g]
        flat = slice(g * SSM_STATE, (g + 1) * SSM_STATE)
        lb_re_row[:, flat] = lb_re[g:g + 1, :]
        lb_im_row[:, flat] = lb_im[g:g + 1, :]

    lr_all = lb_re_row[...]
    li_all = lb_im_row[...]
    a2_re_ref[...] = lr_all * lr_all - li_all * li_all
    a2_im_ref[...] = 2.0 * lr_all * li_all

    halves = SSM_COL_BLOCK // V7X_LANES
    for cb in range(SSM_N_BLOCKS):
        flat = slice(cb * SSM_STATE_BLOCK, (cb + 1) * SSM_STATE_BLOCK)
        lr, li = lb_re_row[:, flat], lb_im_row[:, flat]
        l2r, l2i = a2_re_ref[:, flat], a2_im_ref[:, flat]
        w_r, w_i = wb_re[cb], wb_im[cb]
        wl_r, wl_i = w_r * lr - w_i * li, w_r * li + w_i * lr
        c_r, c_n = wc_re[cb], wc_im[cb]
        o1_r, o1_i = c_r * lr + c_n * li, c_n * lr - c_r * li
        o2_r, o2_i = c_r * l2r + c_n * l2i, c_n * l2r - c_r * l2i
        for half in range(halves):
            j = cb * halves + half
            r = slice(half * V7X_LANES, (half + 1) * V7X_LANES)
            s = slice(half * SSM_LANE_STATES, (half + 1) * SSM_LANE_STATES)
            t0, t1 = slice(0, V7X_LANES), slice(V7X_LANES, SSM_PAIR_COLS)
            re, im = slice(0, SSM_LANE_STATES), slice(SSM_LANE_STATES, SSM_PAIR_STATES)
            in_ref[j, t0, re] = wl_r[r, s].astype(BF16)
            in_ref[j, t0, im] = wl_i[r, s].astype(BF16)
            in_ref[j, t1, re] = w_r[r, s].astype(BF16)
            in_ref[j, t1, im] = w_i[r, s].astype(BF16)
            k0 = _dot_nt_f32(w_r[r, s], c_r[r, s]) + _dot_nt_f32(w_i[r, s], c_n[r, s])
            k1 = _dot_nt_f32(wl_r[r, s], c_r[r, s]) + _dot_nt_f32(wl_i[r, s], c_n[r, s])
            direct_ref[j, t0, t0] = k0.astype(BF16)
            direct_ref[j, t0, t1] = k1.astype(BF16)
            direct_ref[j, t1, t0] = jnp.zeros((V7X_LANES, V7X_LANES), BF16)
            direct_ref[j, t1, t1] = k0.astype(BF16)
            out_ref[j, re, t0] = o1_r[r, s].T.astype(BF16)
            out_ref[j, im, t0] = o1_i[r, s].T.astype(BF16)
            out_ref[j, re, t1] = o2_r[r, s].T.astype(BF16)
            out_ref[j, im, t1] = o2_i[r, s].T.astype(BF16)


def _ssm_prep(lam_re, lam_im, log_dt, b_re, b_im, c_re, c_im):
    g, p, _ = b_re.shape
    row = jax.ShapeDtypeStruct((1, SSM_FLAT), F32)
    n = SSM_N_LANE_BLOCKS
    block_diag = pltpu.VMEM((SSM_N_BLOCKS, SSM_COL_BLOCK, SSM_STATE_BLOCK), F32)
    return pl.pallas_call(
        _ssm_prep_kernel,
        out_shape=(row, row,
                   jax.ShapeDtypeStruct((n, SSM_PAIR_COLS, SSM_PAIR_STATES), BF16),
                   jax.ShapeDtypeStruct((n, SSM_PAIR_STATES, SSM_PAIR_COLS), BF16),
                   jax.ShapeDtypeStruct((n, SSM_PAIR_COLS, SSM_PAIR_COLS), BF16)),
        scratch_shapes=[pltpu.VMEM((1, SSM_FLAT), F32), pltpu.VMEM((1, SSM_FLAT), F32),
                        block_diag, block_diag, block_diag, block_diag],
        compiler_params=pltpu.CompilerParams(vmem_limit_bytes=V7X_VMEM_LIMIT_BYTES),
        name="ssm_prep",
    )(lam_re, lam_im, log_dt.reshape(g, 1), b_re, b_im, c_re, c_im)


def _pool_pieces(x_ref, dst_ref, ext_ref, w_ref, scale_ref, g, *, tt, bb, first_pos):
    tm = tt * bb
    halo = POOL_BUF * bb
    dst_ref[...] = _time_major(x_ref[...])
    ext_ref[halo:halo + tm, :] = _rmsnorm(dst_ref[...], g)
    yield
    row = lax.broadcasted_iota(jnp.int32, (tm, 1), 0)
    pos = first_pos + lax.shift_right_logical(row, int(math.log2(bb)))
    for gi, w in enumerate(POOL_WINDOWS):
        cols = slice(gi * POOL_GROUP_DIM, (gi + 1) * POOL_GROUP_DIM)
        acc = ext_ref[halo:halo + tm, cols]
        for k in range(1, w):
            acc = acc + ext_ref[halo - k * bb:halo - k * bb + tm, cols]
        inv_cnt = 1.0 / jnp.minimum(w, pos + 1).astype(F32)
        diff = acc * inv_cnt - ext_ref[halo:halo + tm, cols]
        dst_ref[:, cols] += _dot(diff.astype(BF16), w_ref[gi]) * scale_ref[:, cols]
        yield


def _ffn_ple_pieces(rows_ref, h_in_ref, p_ref, w_gate_ref, w_up_ref, w_down_ref, g_ple, ple_in_ref, ple_gate_ref,
                    h_ref, pt_ref, act_ref, release_h_in, emit):
    d_ff = act_ref.shape[1]
    pt_ref[...] = _time_major(p_ref[...]).astype(BF16)
    for c0 in range(0, d_ff, FFN_COLS):
        cols = slice(c0, min(c0 + FFN_COLS, d_ff))
        gate = _dot(h_in_ref[...], w_gate_ref[:, cols])
        up = _dot(h_in_ref[...], w_up_ref[:, cols])
        act_ref[:, cols] = (jax.nn.silu(gate) * up).astype(BF16)
        if cols.stop == d_ff:
            release_h_in()
        yield
    for n0 in range(0, D_MODEL, OUT_COLS):
        cols = slice(n0, n0 + OUT_COLS)
        rows_ref[:, cols] += _dot(act_ref[...], w_down_ref[:, cols])
        yield
    h_ref[...] = _rmsnorm(rows_ref[...], g_ple).astype(BF16)
    for n0 in range(0, D_MODEL, OUT_COLS):
        cols = slice(n0, n0 + OUT_COLS)
        ple_gate = jax.nn.sigmoid(_dot(h_ref[...], ple_gate_ref[:, cols]))
        rows_ref[:, cols] += _dot(pt_ref[...], ple_in_ref[:, cols]) * ple_gate
        yield
    emit()
    yield


_DONE = object()


def _interleave(*stages):
    live = list(stages)
    while live:
        for stage in list(live):
            if next(stage, _DONE) is _DONE:
                live.remove(stage)


def _then(stage, last_piece):
    yield from stage
    last_piece()
    yield


def _run_halves(sched, g_ffn, mid_ref, hmid_ref, rows_ref, mixer, ffn):
    h_in_reads_issued = []

    def mixer_into(dst_ref):
        def norm_for_ffn():
            assert h_in_reads_issued or not sched.lag, "hmid_ref rewritten before the FFN half read it"
            hmid_ref[...] = _rmsnorm(dst_ref[...], g_ffn).astype(BF16)
        return _then(mixer(dst_ref), norm_for_ffn)

    def release_h_in():
        h_in_reads_issued.append(True)

    if sched.lag:
        rows_ref[...] = mid_ref[...]
        _interleave(ffn(hmid_ref, release_h_in), mixer_into(mid_ref))
    else:
        _interleave(mixer_into(rows_ref))
        _interleave(ffn(hmid_ref, release_h_in))


def _init_handoff(sched, mid_ref, hmid_ref):
    if sched.lag:
        mid_ref[...] = jnp.zeros(mid_ref.shape, mid_ref.dtype)
        hmid_ref[...] = jnp.zeros(hmid_ref.shape, hmid_ref.dtype)


def _ffn_scratch(tm, d_ff, ple_dim):
    return [
        pltpu.VMEM((tm, D_MODEL), BF16),
        pltpu.VMEM((tm, D_MODEL), F32),
        pltpu.VMEM((tm, D_MODEL), BF16),
        pltpu.VMEM((tm, ple_dim), BF16),
        pltpu.VMEM((tm, d_ff), BF16),
    ]


def _layer0_kernel(x_ref, prev_ref, p_ref, g_mix_ref, pool_w_ref, pool_scale_ref,
                   g_ffn_ref, w_gate_ref, w_up_ref, w_down_ref, g_ple_ref, ple_in_ref, ple_gate_ref,
                   o_ref, state_ref, ext_ref, mid_ref, hmid_ref, rows_ref, h_ref, pt_ref, act_ref,
                   *, sched, start):
    tt, bb = sched.tt, sched.bb
    tm = tt * bb
    halo = POOL_BUF * bb
    i = pl.program_id(1)

    @pl.when(i == 0)
    def _():
        ext_ref[0:halo, :] = _time_major(prev_ref[...])
        _init_handoff(sched, mid_ref, hmid_ref)

    @pl.when(i > 0)
    def _():
        ext_ref[0:halo, :] = ext_ref[tm:tm + halo, :]

    def mixer(dst_ref):
        return _pool_pieces(x_ref, dst_ref, ext_ref, pool_w_ref, pool_scale_ref, g_mix_ref[...],
                            tt=tt, bb=bb, first_pos=start + sched.mixer_block(i) * tt)

    def emit():
        o_ref[...] = rows_ref[...].reshape(tt, bb, D_MODEL)

    def ffn(h_in_ref, release_h_in):
        return _ffn_ple_pieces(rows_ref, h_in_ref, p_ref, w_gate_ref, w_up_ref, w_down_ref,
                               g_ple_ref[...], ple_in_ref, ple_gate_ref, h_ref, pt_ref, act_ref,
                               release_h_in, emit)

    _run_halves(sched, g_ffn_ref[...], mid_ref, hmid_ref, rows_ref, mixer, ffn)

    @pl.when(i == sched.nt - 1)
    def _():
        state_ref[...] = _batch_major(ext_ref[tm:tm + halo, :], POOL_BUF, bb)


def _ffn_weight_specs(prm, layer):
    return [
        _layer_spec((1, D_MODEL), layer),
        _layer_spec(prm['ffn_w_gate'].shape[1:], layer), _layer_spec(prm['ffn_w_up'].shape[1:], layer),
        _layer_spec(prm['ffn_w_down'].shape[1:], layer),
        _layer_spec((1, D_MODEL), layer),
        _layer_spec(prm['ple_w_in'].shape[1:], layer), _layer_spec(prm['ple_w_gate'].shape[1:], layer),
    ]


def _ffn_weights(prm):
    return (prm['g_ffn'], prm['ffn_w_gate'], prm['ffn_w_up'], prm['ffn_w_down'],
            prm['g_ple'], prm['ple_w_in'], prm['ple_w_gate'])


def _layer0(x, prev, p, prm, *, start):
    b_len, t_len, _ = x.shape
    sched = _Schedule(b_len, t_len, POOL_LAYER_ROWS)
    tt, bb = sched.tt, sched.bb
    tm = tt * bb
    state_spec = pl.BlockSpec((bb, POOL_BUF, D_MODEL), lambda j, i: (j, 0, 0))
    mid_rows = tm if sched.lag else 2 * V7X_SUBLANES
    return pl.pallas_call(
        functools.partial(_layer0_kernel, sched=sched, start=start),
        grid=sched.grid,
        in_specs=[
            pl.BlockSpec((bb, tt, D_MODEL), lambda j, i: (j, sched.mixer_block(i), 0)),
            state_spec,
            pl.BlockSpec((None, bb, tt, p.shape[-1]), lambda j, i: (0, j, sched.ffn_block(i), 0)),
            _layer_spec((1, D_MODEL), 0),
            _const_spec(prm['pool_w'].shape),
            _const_spec((1, D_MODEL)),
        ] + _ffn_weight_specs(prm, 0),
        out_specs=[pl.BlockSpec((tt, bb, D_MODEL), lambda j, i: (sched.ffn_block(i), j, 0)), state_spec],
        out_shape=[
            jax.ShapeDtypeStruct((t_len, b_len, D_MODEL), F32),
            jax.ShapeDtypeStruct((b_len, POOL_BUF, D_MODEL), F32),
        ],
        scratch_shapes=([pltpu.VMEM(((POOL_BUF + tt) * bb, D_MODEL), F32),
                         pltpu.VMEM((mid_rows, D_MODEL), F32)]
                        + _ffn_scratch(tm, prm['ffn_w_gate'].shape[-1], p.shape[-1])),
        compiler_params=_params(),
        name="layer0_pool_ffn",
    )(x, prev, p, prm['g_mix'], prm['pool_w'], prm['pool_scale'], *_ffn_weights(prm))


def _ssm_kernel(x_ref, h0_re_ref, h0_im_ref, g_ref, a2_re_ref, a2_im_ref, in_ref, out_ref, direct_ref,
                d_ref, wglu_ref, o_ref, s_re_ref, s_im_ref,
                st_re, st_im, u_ref, xj_ref, sbuf_ref, z_ref, *, pairs, bb):
    m = pairs * bb
    i = pl.program_id(1)

    @pl.when(i == 0)
    def _():
        st_re[...] = h0_re_ref[...]
        st_im[...] = h0_im_ref[...]

    for t in range(2):
        u_ref[t] = _rmsnorm(x_ref[:, t].reshape(m, D_MODEL), g_ref[...])

    def contributions(j):
        cols = slice(j * V7X_LANES, (j + 1) * V7X_LANES)
        both = jnp.concatenate([u_ref[0, :, cols], u_ref[1, :, cols]], axis=-1)
        xj_ref[j] = both.astype(BF16)
        sbuf_ref[:, j * SSM_PAIR_STATES:(j + 1) * SSM_PAIR_STATES] = _dot(xj_ref[j], in_ref[j])

    def scan(j):
        states = slice(j * SSM_LANE_STATES, (j + 1) * SSM_LANE_STATES)
        re = slice(j * SSM_PAIR_STATES, j * SSM_PAIR_STATES + SSM_LANE_STATES)
        im = slice(j * SSM_PAIR_STATES + SSM_LANE_STATES, (j + 1) * SSM_PAIR_STATES)
        ar = jnp.broadcast_to(a2_re_ref[:, states], (V7X_SUBLANES, SSM_LANE_STATES))
        ai = jnp.broadcast_to(a2_im_ref[:, states], (V7X_SUBLANES, SSM_LANE_STATES))
        for r0 in range(0, bb, V7X_SUBLANES):
            sr = st_re[r0:r0 + V7X_SUBLANES, states]
            si = st_im[r0:r0 + V7X_SUBLANES, states]
            for c in range(pairs):
                rows = slice(c * bb + r0, c * bb + r0 + V7X_SUBLANES)
                pr, pi = sbuf_ref[rows, re], sbuf_ref[rows, im]
                sbuf_ref[rows, re] = sr
                sbuf_ref[rows, im] = si
                sr, si = ar * sr - ai * si + pr, ar * si + ai * sr + pi
            st_re[r0:r0 + V7X_SUBLANES, states] = sr
            st_im[r0:r0 + V7X_SUBLANES, states] = si

    def outputs(j):
        cols = slice(j * V7X_LANES, (j + 1) * V7X_LANES)
        entering = sbuf_ref[:, j * SSM_PAIR_STATES:(j + 1) * SSM_PAIR_STATES].astype(BF16)
        y = _dot(entering, out_ref[j]) + _dot(xj_ref[j], direct_ref[j])
        for t in range(2):
            yt = y[:, t * V7X_LANES:(t + 1) * V7X_LANES] + d_ref[:, cols] * u_ref[t, :, cols]
            z_ref[t, :, cols] = jax.nn.gelu(yt).astype(BF16)

    for k in range(SSM_N_LANE_BLOCKS + 2):
        if 0 <= k - 2:
            outputs(k - 2)
        if k < SSM_N_LANE_BLOCKS:
            contributions(k)
        if 0 <= k - 1 < SSM_N_LANE_BLOCKS:
            scan(k - 1)

    for t in range(2):
        for n0 in range(0, D_MODEL, OUT_COLS):
            cols = slice(n0, n0 + OUT_COLS)
            a = _dot(z_ref[t], wglu_ref[:, cols])
            gate = _dot(z_ref[t], wglu_ref[:, D_MODEL + n0:D_MODEL + n0 + OUT_COLS])
            o_ref[:, t, :, cols] = x_ref[:, t, :, cols] + (a * jax.nn.sigmoid(gate)).reshape(pairs, bb, OUT_COLS)

    s_re_ref[...] = st_re[...]
    s_im_ref[...] = st_im[...]


def _ssm_layer(x, h0_re, h0_im, prm):
    t_len, b_len, _ = x.shape
    tt, bb = _tiles(b_len, t_len, SSM_ROWS)
    assert tt % 2 == 0
    pairs = tt // 2
    m = pairs * bb
    x_spec = pl.BlockSpec((pairs, 2, bb, D_MODEL), lambda j, i: (i, 0, j, 0))
    st_spec = pl.BlockSpec((bb, SSM_FLAT), lambda j, i: (j, 0))
    out, s_re, s_im = pl.pallas_call(
        functools.partial(_ssm_kernel, pairs=pairs, bb=bb),
        grid=(b_len // bb, t_len // tt),
        in_specs=[
            x_spec, st_spec, st_spec,
            _layer_spec((1, D_MODEL), 1),
            _const_spec((1, SSM_FLAT)), _const_spec((1, SSM_FLAT)),
            _const_spec(prm['ssm_in'].shape), _const_spec(prm['ssm_out'].shape),
            _const_spec(prm['ssm_direct'].shape),
            _const_spec((1, D_MODEL)),
            _const_spec(prm['ssm_w_glu'].shape),
        ],
        out_specs=[x_spec, st_spec, st_spec],
        out_shape=[
            jax.ShapeDtypeStruct((t_len // 2, 2, b_len, D_MODEL), F32),
            jax.ShapeDtypeStruct((b_len, SSM_FLAT), F32),
            jax.ShapeDtypeStruct((b_len, SSM_FLAT), F32),
        ],
        scratch_shapes=[
            pltpu.VMEM((bb, SSM_FLAT), F32), pltpu.VMEM((bb, SSM_FLAT), F32),
            pltpu.VMEM((2, m, D_MODEL), F32),
            pltpu.VMEM((SSM_N_LANE_BLOCKS, m, SSM_PAIR_COLS), BF16),
            pltpu.VMEM((m, SSM_N_LANE_BLOCKS * SSM_PAIR_STATES), F32),
            pltpu.VMEM((2, m, D_MODEL), BF16),
        ],
        compiler_params=_params(),
        name="ssm_mixer",
    )(x.reshape(t_len // 2, 2, b_len, D_MODEL), h0_re, h0_im, prm['g_mix'], prm['a2_re'], prm['a2_im'],
      prm['ssm_in'], prm['ssm_out'], prm['ssm_direct'], prm['ssm_d'], prm['ssm_w_glu'])
    return out.reshape(t_len, b_len, D_MODEL), s_re, s_im


def _ffn1_kernel(x_ref, p_ref, g_ffn_ref, w_gate_ref, w_up_ref, w_down_ref, g_ple_ref, ple_in_ref, ple_gate_ref,
                 g_final_ref, o_ref, h_in_ref, rows_ref, h_ref, pt_ref, act_ref, *, tt, bb):
    rows_ref[...] = x_ref[...].reshape(tt * bb, D_MODEL)
    h_in_ref[...] = _rmsnorm(rows_ref[...], g_ffn_ref[...]).astype(BF16)

    def emit():
        o_ref[...] = _batch_major(_rmsnorm(rows_ref[...], g_final_ref[...]), tt, bb)

    _interleave(_ffn_ple_pieces(rows_ref, h_in_ref, p_ref, w_gate_ref, w_up_ref, w_down_ref,
                                g_ple_ref[...], ple_in_ref, ple_gate_ref, h_ref, pt_ref, act_ref,
                                lambda: None, emit))


def _ffn1_layer(x, p, prm):
    t_len, b_len, _ = x.shape
    tt, bb = _tiles(b_len, t_len, FFN_ROWS)
    return pl.pallas_call(
        functools.partial(_ffn1_kernel, tt=tt, bb=bb),
        grid=(b_len // bb, t_len // tt),
        in_specs=[
            pl.BlockSpec((tt, bb, D_MODEL), lambda j, i: (i, j, 0)),
            pl.BlockSpec((None, bb, tt, p.shape[-1]), lambda j, i: (1, j, i, 0)),
        ] + _ffn_weight_specs(prm, 1) + [_const_spec((1, D_MODEL))],
        out_specs=pl.BlockSpec((bb, tt, D_MODEL), lambda j, i: (j, i, 0)),
        out_shape=jax.ShapeDtypeStruct((b_len, t_len, D_MODEL), F32),
        scratch_shapes=_ffn_scratch(tt * bb, prm['ffn_w_gate'].shape[-1], p.shape[-1]),
        compiler_params=_params(),
        name="layer1_ffn",
    )(x, p, *_ffn_weights(prm), prm['g_final'])


def _run_trunk(x, p, start, pool_state, ssm_re, ssm_im, prm):
    b_len = x.shape[0]
    xt, pool_new = _layer0(x, pool_state, p, prm, start=start)
    xt, s_re, s_im = _ssm_layer(xt, ssm_re.reshape(b_len, SSM_FLAT), ssm_im.reshape(b_len, SSM_FLAT), prm)
    y = _ffn1_layer(xt, p, prm)
    s_re = s_re.reshape(1, b_len, SSM_GROUPS, SSM_STATE)
    s_im = s_im.reshape(1, b_len, SSM_GROUPS, SSM_STATE)
    return y, pool_new[None], s_re, s_im


def kernel(x_prompt, x_sample, state_pool, state_ssm_re, state_ssm_im, p_prompt, p_sample, g_mix, g_ffn, g_ple, g_final, pool_w, pool_scale, ssm_lambda_re, ssm_lambda_im, ssm_log_dt, ssm_b_re, ssm_b_im, ssm_c_re, ssm_c_im, ssm_d, ssm_w_glu, ffn_w_gate, ffn_w_up, ffn_w_down, ple_w_in, ple_w_gate):
    depth = g_mix.shape[0]
    a2_re, a2_im, ssm_in, ssm_out, ssm_direct = _ssm_prep(
        ssm_lambda_re[0], ssm_lambda_im[0], ssm_log_dt[0], ssm_b_re[0], ssm_b_im[0], ssm_c_re[0], ssm_c_im[0])
    per_layer_vec = lambda a: a.reshape(depth, 1, D_MODEL)
    prm = dict(
        g_mix=per_layer_vec(g_mix), g_ffn=per_layer_vec(g_ffn), g_ple=per_layer_vec(g_ple),
        g_final=g_final.reshape(1, D_MODEL),
        pool_w=pool_w[0].astype(BF16), pool_scale=pool_scale,
        a2_re=a2_re, a2_im=a2_im,
        ssm_in=ssm_in, ssm_out=ssm_out, ssm_direct=ssm_direct,
        ssm_d=ssm_d, ssm_w_glu=ssm_w_glu[0].astype(BF16),
        ffn_w_gate=_to_bf16(ffn_w_gate), ffn_w_up=_to_bf16(ffn_w_up), ffn_w_down=_to_bf16(ffn_w_down),
        ple_w_in=_to_bf16(ple_w_in), ple_w_gate=_to_bf16(ple_w_gate),
    )
    b_p = x_prompt.shape[0]
    zeros_pool = jnp.zeros((b_p, POOL_BUF, D_MODEL), F32)
    zeros_ssm = jnp.zeros((b_p, SSM_GROUPS, SSM_STATE), F32)
    y_p, pool_p, re_p, im_p = _run_trunk(x_prompt, p_prompt, 0, zeros_pool, zeros_ssm, zeros_ssm, prm)
    y_s, pool_s, re_s, im_s = _run_trunk(x_sample, p_sample, PAST_LEN, state_pool[0],
                                         state_ssm_re[0], state_ssm_im[0], prm)
    return (y_p, y_s, pool_p, pool_s, re_p, im_p, re_s, im_s)
```

```python
import functools
import math

import jax
import jax.numpy as jnp
from jax import lax
from jax.experimental import pallas as pl
from jax.experimental.pallas import tpu as pltpu

D_MODEL = 1024
POOL_WINDOWS = (2, 4, 8, 16)
POOL_GROUP_DIM = D_MODEL // len(POOL_WINDOWS)
POOL_BUF = max(POOL_WINDOWS) - 1
SSM_GROUP_DIM = 16
SSM_GROUPS = D_MODEL // SSM_GROUP_DIM
SSM_STATE = 64
SSM_FLAT = SSM_GROUPS * SSM_STATE
PAST_LEN = 16384
EPS = 1e-6

V7X_SUBLANES = 8
V7X_LANES = 128
V7X_MXU_DIM = 256
V7X_VMEM_LIMIT_BYTES = 60 * 1024 * 1024

SSM_COL_BLOCK = V7X_MXU_DIM
SSM_GROUPS_PER_BLOCK = SSM_COL_BLOCK // SSM_GROUP_DIM
SSM_STATE_BLOCK = SSM_GROUPS_PER_BLOCK * SSM_STATE
SSM_N_BLOCKS = D_MODEL // SSM_COL_BLOCK
SSM_LANE_GROUPS = V7X_LANES // SSM_GROUP_DIM
SSM_LANE_STATES = SSM_LANE_GROUPS * SSM_STATE
SSM_N_LANE_BLOCKS = D_MODEL // V7X_LANES
SSM_PAIR_COLS = 2 * V7X_LANES
SSM_PAIR_STATES = 2 * SSM_LANE_STATES

POOL_LAYER_ROWS = 512
SSM_ROWS = 512
FFN_ROWS = 1024
FFN_COLS = 2 * V7X_MXU_DIM
OUT_COLS = 2 * V7X_MXU_DIM

BF16 = jnp.bfloat16
F32 = jnp.float32


def _rmsnorm(x, g):
    return x * lax.rsqrt(jnp.mean(x * x, axis=-1, keepdims=True) + EPS) * g


def _dot(a, b):
    return jnp.dot(a, b, preferred_element_type=F32)


def _dot_nt_f32(a, b):
    return lax.dot_general(a, b, (((1,), (1,)), ((), ())), precision=lax.Precision.HIGHEST,
                           preferred_element_type=F32)


def _time_major(block):
    bb, tt, c = block.shape
    return jnp.transpose(block, (1, 0, 2)).reshape(tt * bb, c)


def _batch_major(rows, tt, bb):
    return jnp.transpose(rows.reshape(tt, bb, rows.shape[-1]), (1, 0, 2))


def _const_spec(shape):
    zeros = (0,) * len(shape)
    return pl.BlockSpec(shape, lambda *_: zeros, pipeline_mode=pl.Buffered(1))


def _layer_spec(shape, layer):
    tail = (0,) * len(shape)
    return pl.BlockSpec((None,) + tuple(shape), lambda *_: (layer,) + tail,
                        pipeline_mode=pl.Buffered(1))


def _params():
    return pltpu.CompilerParams(dimension_semantics=("arbitrary", "arbitrary"),
                                vmem_limit_bytes=V7X_VMEM_LIMIT_BYTES)


def _tiles(b_len, t_len, rows):
    bb = min(b_len, max(V7X_SUBLANES, rows // t_len))
    tt = min(t_len, rows // bb)
    assert t_len % tt == 0 and b_len % bb == 0 and bb % V7X_SUBLANES == 0
    return tt, bb


class _Schedule:
    def __init__(self, b_len, t_len, rows):
        self.tt, self.bb = _tiles(b_len, t_len, rows)
        self.nt = t_len // self.tt
        self.lag = 1 if self.nt > 1 else 0
        self.grid = (b_len // self.bb, self.nt + self.lag)

    def mixer_block(self, i):
        return jnp.minimum(i, self.nt - 1)

    def ffn_block(self, i):
        return jnp.maximum(i - self.lag, 0)


def _ssm_prep_kernel(lam_re_ref, lam_im_ref, log_dt_ref, b_re_ref, b_im_ref, c_re_ref, c_im_ref,
                     a2_re_ref, a2_im_ref, in_ref, out_ref, direct_ref,
                     lb_re_row, lb_im_row, wb_re, wb_im, wc_re, wc_im):
    lam_re = lam_re_ref[...]
    lam_im = lam_im_ref[...]
    dt = jnp.exp(log_dt_ref[...])
    mag = jnp.exp(lam_re * dt)
    lb_re = mag * jnp.cos(lam_im * dt)
    lb_im = mag * jnp.sin(lam_im * dt)
    den = lam_re * lam_re + lam_im * lam_im
    f_re = ((lb_re - 1.0) * lam_re + lb_im * lam_im) / den
    f_im = (lb_im * lam_re - (lb_re - 1.0) * lam_im) / den
    b_re = jnp.swapaxes(b_re_ref[...], 1, 2)
    b_im = jnp.swapaxes(b_im_ref[...], 1, 2)
    fr = f_re[:, None, :]
    fi = f_im[:, None, :]
    bb_re = fr * b_re - fi * b_im
    bb_im = fr * b_im + fi * b_re
    c_re = c_re_ref[...]
    c_im_neg = -c_im_ref[...]

    for ref in (wb_re, wb_im, wc_re, wc_im):
        ref[...] = jnp.zeros(ref.shape, ref.dtype)
    for g in range(SSM_GROUPS):
        cb, gl = divmod(g, SSM_GROUPS_PER_BLOCK)
        cols = slice(gl * SSM_GROUP_DIM, (gl + 1) * SSM_GROUP_DIM)
        states = slice(gl * SSM_STATE, (gl + 1) * SSM_STATE)
        wb_re[cb, cols, states] = bb_re[g]
        wb_im[cb, cols, states] = bb_im[g]
        wc_re[cb, cols, states] = c_re[g]
        wc_im[cb, cols, states] = c_im_neg[g]
        flat = slice(g * SSM_STATE, (g + 1) * SSM_STATE)
        lb_re_row[:, flat] = lb_re[g:g + 1, :]
        lb_im_row[:, flat] = lb_im[g:g + 1, :]

    lr_all = lb_re_row[...]
    li_all = lb_im_row[...]
    a2_re_ref[...] = lr_all * lr_all - li_all * li_all
    a2_im_ref[...] = 2.0 * lr_all * li_all

    halves = SSM_COL_BLOCK // V7X_LANES
    for cb in range(SSM_N_BLOCKS):
        flat = slice(cb * SSM_STATE_BLOCK, (cb + 1) * SSM_STATE_BLOCK)
        lr, li = lb_re_row[:, flat], lb_im_row[:, flat]
        l2r, l2i = a2_re_ref[:, flat], a2_im_ref[:, flat]
        w_r, w_i = wb_re[cb], wb_im[cb]
        wl_r, wl_i = w_r * lr - w_i * li, w_r * li + w_i * lr
        c_r, c_n = wc_re[cb], wc_im[cb]
        o1_r, o1_i = c_r * lr + c_n * li, c_n * lr - c_r * li
        o2_r, o2_i = c_r * l2r + c_n * l2i, c_n * l2r - c_r * l2i
        for half in range(halves):
            j = cb * halves + half
            r = slice(half * V7X_LANES, (half + 1) * V7X_LANES)
            s = slice(half * SSM_LANE_STATES, (half + 1) * SSM_LANE_STATES)
            t0, t1 = slice(0, V7X_LANES), slice(V7X_LANES, SSM_PAIR_COLS)
            re, im = slice(0, SSM_LANE_STATES), slice(SSM_LANE_STATES, SSM_PAIR_STATES)
            in_ref[j, t0, re] = wl_r[r, s].astype(BF16)
            in_ref[j, t0, im] = wl_i[r, s].astype(BF16)
            in_ref[j, t1, re] = w_r[r, s].astype(BF16)
            in_ref[j, t1, im] = w_i[r, s].astype(BF16)
            k0 = _dot_nt_f32(w_r[r, s], c_r[r, s]) + _dot_nt_f32(w_i[r, s], c_n[r, s])
            k1 = _dot_nt_f32(wl_r[r, s], c_r[r, s]) + _dot_nt_f32(wl_i[r, s], c_n[r, s])
            direct_ref[j, t0, t0] = k0.astype(BF16)
            direct_ref[j, t0, t1] = k1.astype(BF16)
            direct_ref[j, t1, t0] = jnp.zeros((V7X_LANES, V7X_LANES), BF16)
            direct_ref[j, t1, t1] = k0.astype(BF16)
            out_ref[j, re, t0] = o1_r[r, s].T.astype(BF16)
            out_ref[j, im, t0] = o1_i[r, s].T.astype(BF16)
            out_ref[j, re, t1] = o2_r[r, s].T.astype(BF16)
            out_ref[j, im, t1] = o2_i[r, s].T.astype(BF16)


def _ssm_prep(lam_re, lam_im, log_dt, b_re, b_im, c_re, c_im):
    g, p, _ = b_re.shape
    row = jax.ShapeDtypeStruct((1, SSM_FLAT), F32)
    n = SSM_N_LANE_BLOCKS
    block_diag = pltpu.VMEM((SSM_N_BLOCKS, SSM_COL_BLOCK, SSM_STATE_BLOCK), F32)
    return pl.pallas_call(
        _ssm_prep_kernel,
        out_shape=(row, row,
                   jax.ShapeDtypeStruct((n, SSM_PAIR_COLS, SSM_PAIR_STATES), BF16),
                   jax.ShapeDtypeStruct((n, SSM_PAIR_STATES, SSM_PAIR_COLS), BF16),
                   jax.ShapeDtypeStruct((n, SSM_PAIR_COLS, SSM_PAIR_COLS), BF16)),
        scratch_shapes=[pltpu.VMEM((1, SSM_FLAT), F32), pltpu.VMEM((1, SSM_FLAT), F32),
                        block_diag, block_diag, block_diag, block_diag],
        compiler_params=pltpu.CompilerParams(vmem_limit_bytes=V7X_VMEM_LIMIT_BYTES),
        name="ssm_prep",
    )(lam_re, lam_im, log_dt.reshape(g, 1), b_re, b_im, c_re, c_im)


def _pool_pieces(x_ref, dst_ref, ext_ref, w_ref, scale_ref, g, *, tt, bb, first_pos):
    tm = tt * bb
    halo = POOL_BUF * bb
    dst_ref[...] = _time_major(x_ref[...])
    ext_ref[halo:halo + tm, :] = _rmsnorm(dst_ref[...], g)
    yield
    row = lax.broadcasted_iota(jnp.int32, (tm, 1), 0)
    pos = first_pos + lax.shift_right_logical(row, int(math.log2(bb)))
    for gi, w in enumerate(POOL_WINDOWS):
        cols = slice(gi * POOL_GROUP_DIM, (gi + 1) * POOL_GROUP_DIM)
        acc = ext_ref[halo:halo + tm, cols]
        for k in range(1, w):
            acc = acc + ext_ref[halo - k * bb:halo - k * bb + tm, cols]
        inv_cnt = 1.0 / jnp.minimum(w, pos + 1).astype(F32)
        diff = acc * inv_cnt - ext_ref[halo:halo + tm, cols]
        dst_ref[:, cols] += _dot(diff.astype(BF16), w_ref[gi]) * scale_ref[:, cols]
        yield


def _ffn_ple_pieces(rows_ref, h_in_ref, p_ref, w_gate_ref, w_up_ref, w_down_ref, g_ple, ple_in_ref, ple_gate_ref,
                    h_ref, pt_ref, act_ref, release_h_in, emit):
    d_ff = act_ref.shape[1]
    pt_ref[...] = _time_major(p_ref[...]).astype(BF16)
    for c0 in range(0, d_ff, FFN_COLS):
        cols = slice(c0, min(c0 + FFN_COLS, d_ff))
        gate = _dot(h_in_ref[...], w_gate_ref[:, cols])
        up = _dot(h_in_ref[...], w_up_ref[:, cols])
        act_ref[:, cols] = (jax.nn.silu(gate) * up).astype(BF16)
        if cols.stop == d_ff:
            release_h_in()
        yield
    for n0 in range(0, D_MODEL, OUT_COLS):
        cols = slice(n0, n0 + OUT_COLS)
        rows_ref[:, cols] += _dot(act_ref[...], w_down_ref[:, cols])
        yield
    h_ref[...] = _rmsnorm(rows_ref[...], g_ple).astype(BF16)
    for n0 in range(0, D_MODEL, OUT_COLS):
        cols = slice(n0, n0 + OUT_COLS)
        ple_gate = jax.nn.sigmoid(_dot(h_ref[...], ple_gate_ref[:, cols]))
        rows_ref[:, cols] += _dot(pt_ref[...], ple_in_ref[:, cols]) * ple_gate
        yield
    emit()
    yield


_DONE = object()


def _interleave(*stages):
    live = list(stages)
    while live:
        for stage in list(live):
            if next(stage, _DONE) is _DONE:
                live.remove(stage)


def _then(stage, last_piece):
    yield from stage
    last_piece()
    yield


def _run_halves(sched, g_ffn, mid_ref, hmid_ref, rows_ref, mixer, ffn):
    h_in_reads_issued = []

    def mixer_into(dst_ref):
        def norm_for_ffn():
            assert h_in_reads_issued or not sched.lag, "hmid_ref rewritten before the FFN half read it"
            hmid_ref[...] = _rmsnorm(dst_ref[...], g_ffn).astype(BF16)
        return _then(mixer(dst_ref), norm_for_ffn)

    def release_h_in():
        h_in_reads_issued.append(True)

    if sched.lag:
        rows_ref[...] = mid_ref[...]
        _interleave(ffn(hmid_ref, release_h_in), mixer_into(mid_ref))
    else:
        _interleave(mixer_into(rows_ref))
        _interleave(ffn(hmid_ref, release_h_in))


def _init_handoff(sched, mid_ref, hmid_ref):
    if sched.lag:
        mid_ref[...] = jnp.zeros(mid_ref.shape, mid_ref.dtype)
        hmid_ref[...] = jnp.zeros(hmid_ref.shape, hmid_ref.dtype)


def _ffn_scratch(tm, d_ff, ple_dim):
    return [
        pltpu.VMEM((tm, D_MODEL), BF16),
        pltpu.VMEM((tm, D_MODEL), F32),
        pltpu.VMEM((tm, D_MODEL), BF16),
        pltpu.VMEM((tm, ple_dim), BF16),
        pltpu.VMEM((tm, d_ff), BF16),
    ]


def _layer0_kernel(x_ref, prev_ref, p_ref, g_mix_ref, pool_w_ref, pool_scale_ref,
                   g_ffn_ref, w_gate_ref, w_up_ref, w_down_ref, g_ple_ref, ple_in_ref, ple_gate_ref,
                   o_ref, state_ref, ext_ref, mid_ref, hmid_ref, rows_ref, h_ref, pt_ref, act_ref,
                   *, sched, start):
    tt, bb = sched.tt, sched.bb
    tm = tt * bb
    halo = POOL_BUF * bb
    i = pl.program_id(1)

    @pl.when(i == 0)
    def _():
        ext_ref[0:halo, :] = _time_major(prev_ref[...])
        _init_handoff(sched, mid_ref, hmid_ref)

    @pl.when(i > 0)
    def _():
        ext_ref[0:halo, :] = ext_ref[tm:tm + halo, :]

    def mixer(dst_ref):
        return _pool_pieces(x_ref, dst_ref, ext_ref, pool_w_ref, pool_scale_ref, g_mix_ref[...],
                            tt=tt, bb=bb, first_pos=start + sched.mixer_block(i) * tt)

    def emit():
        o_ref[...] = rows_ref[...].reshape(tt, bb, D_MODEL)

    def ffn(h_in_ref, release_h_in):
        return _ffn_ple_pieces(rows_ref, h_in_ref, p_ref, w_gate_ref, w_up_ref, w_down_ref,
                               g_ple_ref[...], ple_in_ref, ple_gate_ref, h_ref, pt_ref, act_ref,
                               release_h_in, emit)

    _run_halves(sched, g_ffn_ref[...], mid_ref, hmid_ref, rows_ref, mixer, ffn)

    @pl.when(i == sched.nt - 1)
    def _():
        state_ref[...] = _batch_major(ext_ref[tm:tm + halo, :], POOL_BUF, bb)


def _ffn_weight_specs(prm, layer):
    return [
        _layer_spec((1, D_MODEL), layer),
        _layer_spec(prm['ffn_w_gate'].shape[1:], layer), _layer_spec(prm['ffn_w_up'].shape[1:], layer),
        _layer_spec(prm['ffn_w_down'].shape[1:], layer),
        _layer_spec((1, D_MODEL), layer),
        _layer_spec(prm['ple_w_in'].shape[1:], layer), _layer_spec(prm['ple_w_gate'].shape[1:], layer),
    ]


def _ffn_weights(prm):
    return (prm['g_ffn'], prm['ffn_w_gate'], prm['ffn_w_up'], prm['ffn_w_down'],
            prm['g_ple'], prm['ple_w_in'], prm['ple_w_gate'])


def _layer0(x, prev, p, prm, *, start):
    b_len, t_len, _ = x.shape
    sched = _Schedule(b_len, t_len, POOL_LAYER_ROWS)
    tt, bb = sched.tt, sched.bb
    tm = tt * bb
    state_spec = pl.BlockSpec((bb, POOL_BUF, D_MODEL), lambda j, i: (j, 0, 0))
    mid_rows = tm if sched.lag else 2 * V7X_SUBLANES
    return pl.pallas_call(
        functools.partial(_layer0_kernel, sched=sched, start=start),
        grid=sched.grid,
        in_specs=[
            pl.BlockSpec((bb, tt, D_MODEL), lambda j, i: (j, sched.mixer_block(i), 0)),
            state_spec,
            pl.BlockSpec((None, bb, tt, p.shape[-1]), lambda j, i: (0, j, sched.ffn_block(i), 0)),
            _layer_spec((1, D_MODEL), 0),
            _const_spec(prm['pool_w'].shape),
            _const_spec((1, D_MODEL)),
        ] + _ffn_weight_specs(prm, 0),
        out_specs=[pl.BlockSpec((tt, bb, D_MODEL), lambda j, i: (sched.ffn_block(i), j, 0)), state_spec],
        out_shape=[
            jax.ShapeDtypeStruct((t_len, b_len, D_MODEL), F32),
            jax.ShapeDtypeStruct((b_len, POOL_BUF, D_MODEL), F32),
        ],
        scratch_shapes=([pltpu.VMEM(((POOL_BUF + tt) * bb, D_MODEL), F32),
                         pltpu.VMEM((mid_rows, D_MODEL), F32)]
                        + _ffn_scratch(tm, prm['ffn_w_gate'].shape[-1], p.shape[-1])),
        compiler_params=_params(),
        name="layer0_pool_ffn",
    )(x, prev, p, prm['g_mix'], prm['pool_w'], prm['pool_scale'], *_ffn_weights(prm))


def _ssm_kernel(x_ref, h0_re_ref, h0_im_ref, g_ref, a2_re_ref, a2_im_ref, in_ref, out_ref, direct_ref,
                d_ref, wglu_ref, o_ref, s_re_ref, s_im_ref,
                st_re, st_im, u_ref, xj_ref, sbuf_ref, z_ref, *, pairs, bb):
    m = pairs * bb
    i = pl.program_id(1)

    @pl.when(i == 0)
    def _():
        st_re[...] = h0_re_ref[...]
        st_im[...] = h0_im_ref[...]

    for t in range(2):
        u_ref[t] = _rmsnorm(x_ref[:, t].reshape(m, D_MODEL), g_ref[...])

    def contributions(j):
        cols = slice(j * V7X_LANES, (j + 1) * V7X_LANES)
        both = jnp.concatenate([u_ref[0, :, cols], u_ref[1, :, cols]], axis=-1)
        xj_ref[j] = both.astype(BF16)
        sbuf_ref[:, j * SSM_PAIR_STATES:(j + 1) * SSM_PAIR_STATES] = _dot(xj_ref[j], in_ref[j])

    def scan(j):
        states = slice(j * SSM_LANE_STATES, (j + 1) * SSM_LANE_STATES)
        re = slice(j * SSM_PAIR_STATES, j * SSM_PAIR_STATES + SSM_LANE_STATES)
        im = slice(j * SSM_PAIR_STATES + SSM_LANE_STATES, (j + 1) * SSM_PAIR_STATES)
        ar = jnp.broadcast_to(a2_re_ref[:, states], (V7X_SUBLANES, SSM_LANE_STATES))
        ai = jnp.broadcast_to(a2_im_ref[:, states], (V7X_SUBLANES, SSM_LANE_STATES))
        for r0 in range(0, bb, V7X_SUBLANES):
            sr = st_re[r0:r0 + V7X_SUBLANES, states]
            si = st_im[r0:r0 + V7X_SUBLANES, states]
            for c in range(pairs):
                rows = slice(c * bb + r0, c * bb + r0 + V7X_SUBLANES)
                pr, pi = sbuf_ref[rows, re], sbuf_ref[rows, im]
                sbuf_ref[rows, re] = sr
                sbuf_ref[rows, im] = si
                sr, si = ar * sr - ai * si + pr, ar * si + ai * sr + pi
            st_re[r0:r0 + V7X_SUBLANES, states] = sr
            st_im[r0:r0 + V7X_SUBLANES, states] = si

    def outputs(j):
        cols = slice(j * V7X_LANES, (j + 1) * V7X_LANES)
        entering = sbuf_ref[:, j * SSM_PAIR_STATES:(j + 1) * SSM_PAIR_STATES].astype(BF16)
        y = _dot(entering, out_ref[j]) + _dot(xj_ref[j], direct_ref[j])
        for t in range(2):
            yt = y[:, t * V7X_LANES:(t + 1) * V7X_LANES] + d_ref[:, cols] * u_ref[t, :, cols]
            z_ref[t, :, cols] = jax.nn.gelu(yt).astype(BF16)

    for k in range(SSM_N_LANE_BLOCKS + 2):
        if 0 <= k - 2:
            outputs(k - 2)
        if k < SSM_N_LANE_BLOCKS:
            contributions(k)
        if 0 <= k - 1 < SSM_N_LANE_BLOCKS:
            scan(k - 1)

    for t in range(2):
        for n0 in range(0, D_MODEL, OUT_COLS):
            cols = slice(n0, n0 + OUT_COLS)
            a = _dot(z_ref[t], wglu_ref[:, cols])
            gate = _dot(z_ref[t], wglu_ref[:, D_MODEL + n0:D_MODEL + n0 + OUT_COLS])
            o_ref[:, t, :, cols] = x_ref[:, t, :, cols] + (a * jax.nn.sigmoid(gate)).reshape(pairs, bb, OUT_COLS)

    s_re_ref[...] = st_re[...]
    s_im_ref[...] = st_im[...]


def _ssm_layer(x, h0_re, h0_im, prm):
    t_len, b_len, _ = x.shape
    tt, bb = _tiles(b_len, t_len, SSM_ROWS)
    assert tt % 2 == 0
    pairs = tt // 2
    m = pairs * bb
    x_spec = pl.BlockSpec((pairs, 2, bb, D_MODEL), lambda j, i: (i, 0, j, 0))
    st_spec = pl.BlockSpec((bb, SSM_FLAT), lambda j, i: (j, 0))
    out, s_re, s_im = pl.pallas_call(
        functools.partial(_ssm_kernel, pairs=pairs, bb=bb),
        grid=(b_len // bb, t_len // tt),
        in_specs=[
            x_spec, st_spec, st_spec,
            _layer_spec((1, D_MODEL), 1),
            _const_spec((1, SSM_FLAT)), _const_spec((1, SSM_FLAT)),
            _const_spec(prm['ssm_in'].shape), _const_spec(prm['ssm_out'].shape),
            _const_spec(prm['ssm_direct'].shape),
            _const_spec((1, D_MODEL)),
            _const_spec(prm['ssm_w_glu'].shape),
        ],
        out_specs=[x_spec, st_spec, st_spec],
        out_shape=[
            jax.ShapeDtypeStruct((t_len // 2, 2, b_len, D_MODEL), F32),
            jax.ShapeDtypeStruct((b_len, SSM_FLAT), F32),
            jax.ShapeDtypeStruct((b_len, SSM_FLAT), F32),
        ],
        scratch_shapes=[
            pltpu.VMEM((bb, SSM_FLAT), F32), pltpu.VMEM((bb, SSM_FLAT), F32),
            pltpu.VMEM((2, m, D_MODEL), F32),
            pltpu.VMEM((SSM_N_LANE_BLOCKS, m, SSM_PAIR_COLS), BF16),
            pltpu.VMEM((m, SSM_N_LANE_BLOCKS * SSM_PAIR_STATES), F32),
            pltpu.VMEM((2, m, D_MODEL), BF16),
        ],
        compiler_params=_params(),
        name="ssm_mixer",
    )(x.reshape(t_len // 2, 2, b_len, D_MODEL), h0_re, h0_im, prm['g_mix'], prm['a2_re'], prm['a2_im'],
      prm['ssm_in'], prm['ssm_out'], prm['ssm_direct'], prm['ssm_d'], prm['ssm_w_glu'])
    return out.reshape(t_len, b_len, D_MODEL), s_re, s_im


def _ffn1_kernel(x_ref, p_ref, g_ffn_ref, w_gate_ref, w_up_ref, w_down_ref, g_ple_ref, ple_in_ref, ple_gate_ref,
                 g_final_ref, o_ref, h_in_ref, rows_ref, h_ref, pt_ref, act_ref, *, tt, bb):
    rows_ref[...] = x_ref[...].reshape(tt * bb, D_MODEL)
    h_in_ref[...] = _rmsnorm(rows_ref[...], g_ffn_ref[...]).astype(BF16)

    def emit():
        o_ref[...] = _batch_major(_rmsnorm(rows_ref[...], g_final_ref[...]), tt, bb)

    _interleave(_ffn_ple_pieces(rows_ref, h_in_ref, p_ref, w_gate_ref, w_up_ref, w_down_ref,
                                g_ple_ref[...], ple_in_ref, ple_gate_ref, h_ref, pt_ref, act_ref,
                                lambda: None, emit))


def _ffn1_layer(x, p, prm):
    t_len, b_len, _ = x.shape
    tt, bb = _tiles(b_len, t_len, FFN_ROWS)
    return pl.pallas_call(
        functools.partial(_ffn1_kernel, tt=tt, bb=bb),
        grid=(b_len // bb, t_len // tt),
        in_specs=[
            pl.BlockSpec((tt, bb, D_MODEL), lambda j, i: (i, j, 0)),
            pl.BlockSpec((None, bb, tt, p.shape[-1]), lambda j, i: (1, j, i, 0)),
        ] + _ffn_weight_specs(prm, 1) + [_const_spec((1, D_MODEL))],
        out_specs=pl.BlockSpec((bb, tt, D_MODEL), lambda j, i: (j, i, 0)),
        out_shape=jax.ShapeDtypeStruct((b_len, t_len, D_MODEL), F32),
        scratch_shapes=_ffn_scratch(tt * bb, prm['ffn_w_gate'].shape[-1], p.shape[-1]),
        compiler_params=_params(),
        name="layer1_ffn",
    )(x, p, *_ffn_weights(prm), prm['g_final'])


def _run_trunk(x, p, start, pool_state, ssm_re, ssm_im, prm):
    b_len = x.shape[0]
    xt, pool_new = _layer0(x, pool_state, p, prm, start=start)
    xt, s_re, s_im = _ssm_layer(xt, ssm_re.reshape(b_len, SSM_FLAT), ssm_im.reshape(b_len, SSM_FLAT), prm)
    y = _ffn1_layer(xt, p, prm)
    s_re = s_re.reshape(1, b_len, SSM_GROUPS, SSM_STATE)
    s_im = s_im.reshape(1, b_len, SSM_GROUPS, SSM_STATE)
    return y, pool_new[None], s_re, s_im


def kernel(x_prompt, x_sample, state_pool, state_ssm_re, state_ssm_im, p_prompt, p_sample, g_mix, g_ffn, g_ple, g_final, pool_w, pool_scale, ssm_lambda_re, ssm_lambda_im, ssm_log_dt, ssm_b_re, ssm_b_im, ssm_c_re, ssm_c_im, ssm_d, ssm_w_glu, ffn_w_gate, ffn_w_up, ffn_w_down, ple_w_in, ple_w_gate):
    depth = g_mix.shape[0]
    a2_re, a2_im, ssm_in, ssm_out, ssm_direct = _ssm_prep(
        ssm_lambda_re[0], ssm_lambda_im[0], ssm_log_dt[0], ssm_b_re[0], ssm_b_im[0], ssm_c_re[0], ssm_c_im[0])
    per_layer_vec = lambda a: a.reshape(depth, 1, D_MODEL)
    prm = dict(
        g_mix=per_layer_vec(g_mix), g_ffn=per_layer_vec(g_ffn), g_ple=per_layer_vec(g_ple),
        g_final=g_final.reshape(1, D_MODEL),
        pool_w=pool_w[0].astype(BF16), pool_scale=pool_scale,
        a2_re=a2_re, a2_im=a2_im,
        ssm_in=ssm_in, ssm_out=ssm_out, ssm_direct=ssm_direct,
        ssm_d=ssm_d, ssm_w_glu=ssm_w_glu[0].astype(BF16),
        ffn_w_gate=ffn_w_gate.astype(BF16), ffn_w_up=ffn_w_up.astype(BF16),
        ffn_w_down=ffn_w_down.astype(BF16),
        ple_w_in=ple_w_in.astype(BF16), ple_w_gate=ple_w_gate.astype(BF16),
    )
    b_p = x_prompt.shape[0]
    zeros_pool = jnp.zeros((b_p, POOL_BUF, D_MODEL), F32)
    zeros_ssm = jnp.zeros((b_p, SSM_GROUPS, SSM_STATE), F32)
    y_p, pool_p, re_p, im_p = _run_trunk(x_prompt, p_prompt, 0, zeros_pool, zeros_ssm, zeros_ssm, prm)
    y_s, pool_s, re_s, im_s = _run_trunk(x_sample, p_sample, PAST_LEN, state_pool[0],
                                         state_ssm_re[0], state_ssm_im[0], prm)
    return (y_p, y_s, pool_p, pool_s, re_p, im_p, re_s, im_s)
```

```python
import functools
import math

import jax
import jax.numpy as jnp
from jax import lax
from jax.experimental import pallas as pl
from jax.experimental.pallas import tpu as pltpu

D_MODEL = 1024
POOL_WINDOWS = (2, 4, 8, 16)
POOL_GROUP_DIM = D_MODEL // len(POOL_WINDOWS)
POOL_BUF = max(POOL_WINDOWS) - 1
SSM_GROUP_DIM = 16
SSM_GROUPS = D_MODEL // SSM_GROUP_DIM
SSM_STATE = 64
SSM_FLAT = SSM_GROUPS * SSM_STATE
PAST_LEN = 16384
EPS = 1e-6

V7X_SUBLANES = 8
V7X_LANES = 128
V7X_MXU_DIM = 256
V7X_VMEM_LIMIT_BYTES = 60 * 1024 * 1024

SSM_COL_BLOCK = V7X_MXU_DIM
SSM_GROUPS_PER_BLOCK = SSM_COL_BLOCK // SSM_GROUP_DIM
SSM_STATE_BLOCK = SSM_GROUPS_PER_BLOCK * SSM_STATE
SSM_N_BLOCKS = D_MODEL // SSM_COL_BLOCK
SSM_LANE_GROUPS = V7X_LANES // SSM_GROUP_DIM
SSM_LANE_STATES = SSM_LANE_GROUPS * SSM_STATE
SSM_N_LANE_BLOCKS = D_MODEL // V7X_LANES
SSM_PAIR_COLS = 2 * V7X_LANES
SSM_PAIR_STATES = 2 * SSM_LANE_STATES

POOL_LAYER_ROWS = 512
SSM_ROWS = 512
FFN_ROWS = 1024
FFN_COLS = 2 * V7X_MXU_DIM
OUT_COLS = 2 * V7X_MXU_DIM

BF16 = jnp.bfloat16
F32 = jnp.float32


def _rmsnorm(x, g):
    return x * lax.rsqrt(jnp.mean(x * x, axis=-1, keepdims=True) + EPS) * g


def _dot(a, b):
    return jnp.dot(a, b, preferred_element_type=F32)


def _dot_nt_f32(a, b):
    return lax.dot_general(a, b, (((1,), (1,)), ((), ())), precision=lax.Precision.HIGHEST,
                           preferred_element_type=F32)


def _time_major(block):
    bb, tt, c = block.shape
    return jnp.transpose(block, (1, 0, 2)).reshape(tt * bb, c)


def _batch_major(rows, tt, bb):
    return jnp.transpose(rows.reshape(tt, bb, rows.shape[-1]), (1, 0, 2))


def _const_spec(shape):
    zeros = (0,) * len(shape)
    return pl.BlockSpec(shape, lambda *_: zeros, pipeline_mode=pl.Buffered(1))


def _layer_spec(shape, layer):
    tail = (0,) * len(shape)
    return pl.BlockSpec((None,) + tuple(shape), lambda *_: (layer,) + tail,
                        pipeline_mode=pl.Buffered(1))


def _params():
    return pltpu.CompilerParams(dimension_semantics=("arbitrary", "arbitrary"),
                                vmem_limit_bytes=V7X_VMEM_LIMIT_BYTES)


def _tiles(b_len, t_len, rows):
    bb = min(b_len, max(V7X_SUBLANES, rows // t_len))
    tt = min(t_len, rows // bb)
    assert t_len % tt == 0 and b_len % bb == 0 and bb % V7X_SUBLANES == 0
    return tt, bb


class _Schedule:
    def __init__(self, b_len, t_len, rows):
        self.tt, self.bb = _tiles(b_len, t_len, rows)
        self.nt = t_len // self.tt
        self.lag = 1 if self.nt > 1 else 0
        self.grid = (b_len // self.bb, self.nt + self.lag)

    def mixer_block(self, i):
        return jnp.minimum(i, self.nt - 1)

    def ffn_block(self, i):
        return jnp.maximum(i - self.lag, 0)


def _ssm_prep_kernel(lam_re_ref, lam_im_ref, log_dt_ref, b_re_ref, b_im_ref, c_re_ref, c_im_ref,
                     a2_re_ref, a2_im_ref, in_ref, out_ref, direct_ref,
                     lb_re_row, lb_im_row, wb_re, wb_im, wc_re, wc_im):
    lam_re = lam_re_ref[...]
    lam_im = lam_im_ref[...]
    dt = jnp.exp(log_dt_ref[...])
    mag = jnp.exp(lam_re * dt)
    lb_re = mag * jnp.cos(lam_im * dt)
    lb_im = mag * jnp.sin(lam_im * dt)
    den = lam_re * lam_re + lam_im * lam_im
    f_re = ((lb_re - 1.0) * lam_re + lb_im * lam_im) / den
    f_im = (lb_im * lam_re - (lb_re - 1.0) * lam_im) / den
    b_re = jnp.swapaxes(b_re_ref[...], 1, 2)
    b_im = jnp.swapaxes(b_im_ref[...], 1, 2)
    fr = f_re[:, None, :]
    fi = f_im[:, None, :]
    bb_re = fr * b_re - fi * b_im
    bb_im = fr * b_im + fi * b_re
    c_re = c_re_ref[...]
    c_im_neg = -c_im_ref[...]

    for ref in (wb_re, wb_im, wc_re, wc_im):
        ref[...] = jnp.zeros(ref.shape, ref.dtype)
    for g in range(SSM_GROUPS):
        cb, gl = divmod(g, SSM_GROUPS_PER_BLOCK)
        cols = slice(gl * SSM_GROUP_DIM, (gl + 1) * SSM_GROUP_DIM)
        states = slice(gl * SSM_STATE, (gl + 1) * SSM_STATE)
        wb_re[cb, cols, states] = bb_re[g]
        wb_im[cb, cols, states] = bb_im[g]
        wc_re[cb, cols, states] = c_re[g]
        wc_im[cb, cols, states] = c_im_neg[g]
        flat = slice(g * SSM_STATE, (g + 1) * SSM_STATE)
        lb_re_row[:, flat] = lb_re[g:g + 1, :]
        lb_im_row[:, flat] = lb_im[g:g + 1, :]

    lr_all = lb_re_row[...]
    li_all = lb_im_row[...]
    a2_re_ref[...] = lr_all * lr_all - li_all * li_all
    a2_im_ref[...] = 2.0 * lr_all * li_all

    halves = SSM_COL_BLOCK // V7X_LANES
    for cb in range(SSM_N_BLOCKS):
        flat = slice(cb * SSM_STATE_BLOCK, (cb + 1) * SSM_STATE_BLOCK)
        lr, li = lb_re_row[:, flat], lb_im_row[:, flat]
        l2r, l2i = a2_re_ref[:, flat], a2_im_ref[:, flat]
        w_r, w_i = wb_re[cb], wb_im[cb]
        wl_r, wl_i = w_r * lr - w_i * li, w_r * li + w_i * lr
        c_r, c_n = wc_re[cb], wc_im[cb]
        o1_r, o1_i = c_r * lr + c_n * li, c_n * lr - c_r * li
        o2_r, o2_i = c_r * l2r + c_n * l2i, c_n * l2r - c_r * l2i
        for half in range(halves):
            j = cb * halves + half
            r = slice(half * V7X_LANES, (half + 1) * V7X_LANES)
            s = slice(half * SSM_LANE_STATES, (half + 1) * SSM_LANE_STATES)
            t0, t1 = slice(0, V7X_LANES), slice(V7X_LANES, SSM_PAIR_COLS)
            re, im = slice(0, SSM_LANE_STATES), slice(SSM_LANE_STATES, SSM_PAIR_STATES)
            in_ref[j, t0, re] = wl_r[r, s].astype(BF16)
            in_ref[j, t0, im] = wl_i[r, s].astype(BF16)
            in_ref[j, t1, re] = w_r[r, s].astype(BF16)
            in_ref[j, t1, im] = w_i[r, s].astype(BF16)
            k0 = _dot_nt_f32(w_r[r, s], c_r[r, s]) + _dot_nt_f32(w_i[r, s], c_n[r, s])
            k1 = _dot_nt_f32(wl_r[r, s], c_r[r, s]) + _dot_nt_f32(wl_i[r, s], c_n[r, s])
            direct_ref[j, t0, t0] = k0.astype(BF16)
            direct_ref[j, t0, t1] = k1.astype(BF16)
            direct_ref[j, t1, t0] = jnp.zeros((V7X_LANES, V7X_LANES), BF16)
            direct_ref[j, t1, t1] = k0.astype(BF16)
            out_ref[j, re, t0] = o1_r[r, s].T.astype(BF16)
            out_ref[j, im, t0] = o1_i[r, s].T.astype(BF16)
            out_ref[j, re, t1] = o2_r[r, s].T.astype(BF16)
            out_ref[j, im, t1] = o2_i[r, s].T.astype(BF16)


def _ssm_prep(lam_re, lam_im, log_dt, b_re, b_im, c_re, c_im):
    g, p, _ = b_re.shape
    row = jax.ShapeDtypeStruct((1, SSM_FLAT), F32)
    n = SSM_N_LANE_BLOCKS
    block_diag = pltpu.VMEM((SSM_N_BLOCKS, SSM_COL_BLOCK, SSM_STATE_BLOCK), F32)
    return pl.pallas_call(
        _ssm_prep_kernel,
        out_shape=(row, row,
                   jax.ShapeDtypeStruct((n, SSM_PAIR_COLS, SSM_PAIR_STATES), BF16),
                   jax.ShapeDtypeStruct((n, SSM_PAIR_STATES, SSM_PAIR_COLS), BF16),
                   jax.ShapeDtypeStruct((n, SSM_PAIR_COLS, SSM_PAIR_COLS), BF16)),
        scratch_shapes=[pltpu.VMEM((1, SSM_FLAT), F32), pltpu.VMEM((1, SSM_FLAT), F32),
                        block_diag, block_diag, block_diag, block_diag],
        compiler_params=pltpu.CompilerParams(vmem_limit_bytes=V7X_VMEM_LIMIT_BYTES),
        name="ssm_prep",
    )(lam_re, lam_im, log_dt.reshape(g, 1), b_re, b_im, c_re, c_im)


def _pool_pieces(x_ref, dst_ref, ext_ref, w_ref, scale_ref, g, *, tt, bb, first_pos):
    tm = tt * bb
    halo = POOL_BUF * bb
    dst_ref[...] = _time_major(x_ref[...])
    ext_ref[halo:halo + tm, :] = _rmsnorm(dst_ref[...], g)
    yield
    row = lax.broadcasted_iota(jnp.int32, (tm, 1), 0)
    pos = first_pos + lax.shift_right_logical(row, int(math.log2(bb)))
    for gi, w in enumerate(POOL_WINDOWS):
        cols = slice(gi * POOL_GROUP_DIM, (gi + 1) * POOL_GROUP_DIM)
        acc = ext_ref[halo:halo + tm, cols]
        for k in range(1, w):
            acc = acc + ext_ref[halo - k * bb:halo - k * bb + tm, cols]
        inv_cnt = 1.0 / jnp.minimum(w, pos + 1).astype(F32)
        diff = acc * inv_cnt - ext_ref[halo:halo + tm, cols]
        dst_ref[:, cols] += _dot(diff.astype(BF16), w_ref[gi]) * scale_ref[:, cols]
        yield


def _ffn_ple_pieces(rows_ref, h_in_ref, p_ref, w_gate_ref, w_up_ref, w_down_ref, g_ple, ple_in_ref, ple_gate_ref,
                    h_ref, pt_ref, act_ref, release_h_in, emit):
    d_ff = act_ref.shape[1]
    pt_ref[...] = _time_major(p_ref[...]).astype(BF16)
    for c0 in range(0, d_ff, FFN_COLS):
        cols = slice(c0, min(c0 + FFN_COLS, d_ff))
        gate = _dot(h_in_ref[...], w_gate_ref[:, cols])
        up = _dot(h_in_ref[...], w_up_ref[:, cols])
        act_ref[:, cols] = (jax.nn.silu(gate) * up).astype(BF16)
        if cols.stop == d_ff:
            release_h_in()
        yield
    for n0 in range(0, D_MODEL, OUT_COLS):
        cols = slice(n0, n0 + OUT_COLS)
        rows_ref[:, cols] += _dot(act_ref[...], w_down_ref[:, cols])
        yield
    h_ref[...] = _rmsnorm(rows_ref[...], g_ple).astype(BF16)
    for n0 in range(0, D_MODEL, OUT_COLS):
        cols = slice(n0, n0 + OUT_COLS)
        ple_gate = jax.nn.sigmoid(_dot(h_ref[...], ple_gate_ref[:, cols]))
        rows_ref[:, cols] += _dot(pt_ref[...], ple_in_ref[:, cols]) * ple_gate
        yield
    emit()
    yield


_DONE = object()


def _interleave(*stages):
    live = list(stages)
    while live:
        for stage in list(live):
            if next(stage, _DONE) is _DONE:
                live.remove(stage)


def _then(stage, last_piece):
    yield from stage
    last_piece()
    yield


def _run_halves(sched, i, g_ffn, mid_ref, hmid_ref, rows_ref, mixer, ffn):
    def mixer_into(dst_ref, h_in_reads_issued):
        def norm_for_ffn():
            assert h_in_reads_issued, "hmid_ref rewritten before the FFN half read it"
            hmid_ref[...] = _rmsnorm(dst_ref[...], g_ffn).astype(BF16)
        return _then(mixer(dst_ref), norm_for_ffn)

    if not sched.lag:
        _interleave(mixer_into(rows_ref, [True]))
        _interleave(ffn(hmid_ref, lambda: None))
        return

    @pl.when(i == 0)
    def _():
        _interleave(mixer_into(mid_ref, [True]))

    @pl.when(jnp.logical_and(i > 0, i < sched.nt))
    def _():
        h_in_reads_issued = []
        rows_ref[...] = mid_ref[...]
        _interleave(ffn(hmid_ref, lambda: h_in_reads_issued.append(True)),
                    mixer_into(mid_ref, h_in_reads_issued))

    @pl.when(i == sched.nt)
    def _():
        rows_ref[...] = mid_ref[...]
        _interleave(ffn(hmid_ref, lambda: None))


def _ffn_scratch(tm, d_ff, ple_dim):
    return [
        pltpu.VMEM((tm, D_MODEL), BF16),
        pltpu.VMEM((tm, D_MODEL), F32),
        pltpu.VMEM((tm, D_MODEL), BF16),
        pltpu.VMEM((tm, ple_dim), BF16),
        pltpu.VMEM((tm, d_ff), BF16),
    ]


def _layer0_kernel(x_ref, prev_ref, p_ref, g_mix_ref, pool_w_ref, pool_scale_ref,
                   g_ffn_ref, w_gate_ref, w_up_ref, w_down_ref, g_ple_ref, ple_in_ref, ple_gate_ref,
                   o_ref, state_ref, ext_ref, mid_ref, hmid_ref, rows_ref, h_ref, pt_ref, act_ref,
                   *, sched, start):
    tt, bb = sched.tt, sched.bb
    tm = tt * bb
    halo = POOL_BUF * bb
    i = pl.program_id(1)

    @pl.when(i == 0)
    def _():
        ext_ref[0:halo, :] = _time_major(prev_ref[...])

    @pl.when(i > 0)
    def _():
        ext_ref[0:halo, :] = ext_ref[tm:tm + halo, :]

    def mixer(dst_ref):
        return _pool_pieces(x_ref, dst_ref, ext_ref, pool_w_ref, pool_scale_ref, g_mix_ref[...],
                            tt=tt, bb=bb, first_pos=start + sched.mixer_block(i) * tt)

    def emit():
        o_ref[...] = rows_ref[...].reshape(tt, bb, D_MODEL)

    def ffn(h_in_ref, release_h_in):
        return _ffn_ple_pieces(rows_ref, h_in_ref, p_ref, w_gate_ref, w_up_ref, w_down_ref,
                               g_ple_ref[...], ple_in_ref, ple_gate_ref, h_ref, pt_ref, act_ref,
                               release_h_in, emit)

    _run_halves(sched, i, g_ffn_ref[...], mid_ref, hmid_ref, rows_ref, mixer, ffn)

    @pl.when(i == sched.nt - 1)
    def _():
        state_ref[...] = _batch_major(ext_ref[tm:tm + halo, :], POOL_BUF, bb)


def _ffn_weight_specs(prm, layer):
    return [
        _layer_spec((1, D_MODEL), layer),
        _layer_spec(prm['ffn_w_gate'].shape[1:], layer), _layer_spec(prm['ffn_w_up'].shape[1:], layer),
        _layer_spec(prm['ffn_w_down'].shape[1:], layer),
        _layer_spec((1, D_MODEL), layer),
        _layer_spec(prm['ple_w_in'].shape[1:], layer), _layer_spec(prm['ple_w_gate'].shape[1:], layer),
    ]


def _ffn_weights(prm):
    return (prm['g_ffn'], prm['ffn_w_gate'], prm['ffn_w_up'], prm['ffn_w_down'],
            prm['g_ple'], prm['ple_w_in'], prm['ple_w_gate'])


def _layer0(x, prev, p, prm, *, start):
    b_len, t_len, _ = x.shape
    sched = _Schedule(b_len, t_len, POOL_LAYER_ROWS)
    tt, bb = sched.tt, sched.bb
    tm = tt * bb
    state_spec = pl.BlockSpec((bb, POOL_BUF, D_MODEL), lambda j, i: (j, 0, 0))
    mid_rows = tm if sched.lag else 2 * V7X_SUBLANES
    return pl.pallas_call(
        functools.partial(_layer0_kernel, sched=sched, start=start),
        grid=sched.grid,
        in_specs=[
            pl.BlockSpec((bb, tt, D_MODEL), lambda j, i: (j, sched.mixer_block(i), 0)),
            state_spec,
            pl.BlockSpec((None, bb, tt, p.shape[-1]), lambda j, i: (0, j, sched.ffn_block(i), 0)),
            _layer_spec((1, D_MODEL), 0),
            _const_spec(prm['pool_w'].shape),
            _const_spec((1, D_MODEL)),
        ] + _ffn_weight_specs(prm, 0),
        out_specs=[pl.BlockSpec((tt, bb, D_MODEL), lambda j, i: (sched.ffn_block(i), j, 0)), state_spec],
        out_shape=[
            jax.ShapeDtypeStruct((t_len, b_len, D_MODEL), F32),
            jax.ShapeDtypeStruct((b_len, POOL_BUF, D_MODEL), F32),
        ],
        scratch_shapes=([pltpu.VMEM(((POOL_BUF + tt) * bb, D_MODEL), F32),
                         pltpu.VMEM((mid_rows, D_MODEL), F32)]
                        + _ffn_scratch(tm, prm['ffn_w_gate'].shape[-1], p.shape[-1])),
        compiler_params=_params(),
        name="layer0_pool_ffn",
    )(x, prev, p, prm['g_mix'], prm['pool_w'], prm['pool_scale'], *_ffn_weights(prm))


def _ssm_kernel(x_ref, h0_re_ref, h0_im_ref, g_ref, a2_re_ref, a2_im_ref, in_ref, out_ref, direct_ref,
                d_ref, wglu_ref, o_ref, s_re_ref, s_im_ref,
                st_re, st_im, u_ref, xj_ref, sbuf_ref, z_ref, *, pairs, bb):
    m = pairs * bb
    i = pl.program_id(1)

    @pl.when(i == 0)
    def _():
        st_re[...] = h0_re_ref[...]
        st_im[...] = h0_im_ref[...]

    for t in range(2):
        u_ref[t] = _rmsnorm(x_ref[:, t].reshape(m, D_MODEL), g_ref[...])

    def contributions(j):
        cols = slice(j * V7X_LANES, (j + 1) * V7X_LANES)
        both = jnp.concatenate([u_ref[0, :, cols], u_ref[1, :, cols]], axis=-1)
        xj_ref[j] = both.astype(BF16)
        sbuf_ref[:, j * SSM_PAIR_STATES:(j + 1) * SSM_PAIR_STATES] = _dot(xj_ref[j], in_ref[j])

    def scan(j):
        states = slice(j * SSM_LANE_STATES, (j + 1) * SSM_LANE_STATES)
        re = slice(j * SSM_PAIR_STATES, j * SSM_PAIR_STATES + SSM_LANE_STATES)
        im = slice(j * SSM_PAIR_STATES + SSM_LANE_STATES, (j + 1) * SSM_PAIR_STATES)
        ar = jnp.broadcast_to(a2_re_ref[:, states], (V7X_SUBLANES, SSM_LANE_STATES))
        ai = jnp.broadcast_to(a2_im_ref[:, states], (V7X_SUBLANES, SSM_LANE_STATES))
        for r0 in range(0, bb, V7X_SUBLANES):
            sr = st_re[r0:r0 + V7X_SUBLANES, states]
            si = st_im[r0:r0 + V7X_SUBLANES, states]
            for c in range(pairs):
                rows = slice(c * bb + r0, c * bb + r0 + V7X_SUBLANES)
                pr, pi = sbuf_ref[rows, re], sbuf_ref[rows, im]
                sbuf_ref[rows, re] = sr
                sbuf_ref[rows, im] = si
                sr, si = ar * sr - ai * si + pr, ar * si + ai * sr + pi
            st_re[r0:r0 + V7X_SUBLANES, states] = sr
            st_im[r0:r0 + V7X_SUBLANES, states] = si

    def outputs(j):
        cols = slice(j * V7X_LANES, (j + 1) * V7X_LANES)
        entering = sbuf_ref[:, j * SSM_PAIR_STATES:(j + 1) * SSM_PAIR_STATES].astype(BF16)
        y = _dot(entering, out_ref[j]) + _dot(xj_ref[j], direct_ref[j])
        for t in range(2):
            yt = y[:, t * V7X_LANES:(t + 1) * V7X_LANES] + d_ref[:, cols] * u_ref[t, :, cols]
            z_ref[t, :, cols] = jax.nn.gelu(yt).astype(BF16)

    for k in range(SSM_N_LANE_BLOCKS + 2):
        if 0 <= k - 2:
            outputs(k - 2)
        if k < SSM_N_LANE_BLOCKS:
            contributions(k)
        if 0 <= k - 1 < SSM_N_LANE_BLOCKS:
            scan(k - 1)

    for t in range(2):
        for n0 in range(0, D_MODEL, OUT_COLS):
            cols = slice(n0, n0 + OUT_COLS)
            a = _dot(z_ref[t], wglu_ref[:, cols])
            gate = _dot(z_ref[t], wglu_ref[:, D_MODEL + n0:D_MODEL + n0 + OUT_COLS])
            o_ref[:, t, :, cols] = x_ref[:, t, :, cols] + (a * jax.nn.sigmoid(gate)).reshape(pairs, bb, OUT_COLS)

    s_re_ref[...] = st_re[...]
    s_im_ref[...] = st_im[...]


def _ssm_layer(x, h0_re, h0_im, prm):
    t_len, b_len, _ = x.shape
    tt, bb = _tiles(b_len, t_len, SSM_ROWS)
    assert tt % 2 == 0
    pairs = tt // 2
    m = pairs * bb
    x_spec = pl.BlockSpec((pairs, 2, bb, D_MODEL), lambda j, i: (i, 0, j, 0))
    st_spec = pl.BlockSpec((bb, SSM_FLAT), lambda j, i: (j, 0))
    out, s_re, s_im = pl.pallas_call(
        functools.partial(_ssm_kernel, pairs=pairs, bb=bb),
        grid=(b_len // bb, t_len // tt),
        in_specs=[
            x_spec, st_spec, st_spec,
            _layer_spec((1, D_MODEL), 1),
            _const_spec((1, SSM_FLAT)), _const_spec((1, SSM_FLAT)),
            _const_spec(prm['ssm_in'].shape), _const_spec(prm['ssm_out'].shape),
            _const_spec(prm['ssm_direct'].shape),
            _const_spec((1, D_MODEL)),
            _const_spec(prm['ssm_w_glu'].shape),
        ],
        out_specs=[x_spec, st_spec, st_spec],
        out_shape=[
            jax.ShapeDtypeStruct((t_len // 2, 2, b_len, D_MODEL), F32),
            jax.ShapeDtypeStruct((b_len, SSM_FLAT), F32),
            jax.ShapeDtypeStruct((b_len, SSM_FLAT), F32),
        ],
        scratch_shapes=[
            pltpu.VMEM((bb, SSM_FLAT), F32), pltpu.VMEM((bb, SSM_FLAT), F32),
            pltpu.VMEM((2, m, D_MODEL), F32),
            pltpu.VMEM((SSM_N_LANE_BLOCKS, m, SSM_PAIR_COLS), BF16),
            pltpu.VMEM((m, SSM_N_LANE_BLOCKS * SSM_PAIR_STATES), F32),
            pltpu.VMEM((2, m, D_MODEL), BF16),
        ],
        compiler_params=_params(),
        name="ssm_mixer",
    )(x.reshape(t_len // 2, 2, b_len, D_MODEL), h0_re, h0_im, prm['g_mix'], prm['a2_re'], prm['a2_im'],
      prm['ssm_in'], prm['ssm_out'], prm['ssm_direct'], prm['ssm_d'], prm['ssm_w_glu'])
    return out.reshape(t_len, b_len, D_MODEL), s_re, s_im


def _ffn1_kernel(x_ref, p_ref, g_ffn_ref, w_gate_ref, w_up_ref, w_down_ref, g_ple_ref, ple_in_ref, ple_gate_ref,
                 g_final_ref, o_ref, h_in_ref, rows_ref, h_ref, pt_ref, act_ref, *, tt, bb):
    rows_ref[...] = x_ref[...].reshape(tt * bb, D_MODEL)
    h_in_ref[...] = _rmsnorm(rows_ref[...], g_ffn_ref[...]).astype(BF16)

    def emit():
        o_ref[...] = _batch_major(_rmsnorm(rows_ref[...], g_final_ref[...]), tt, bb)

    _interleave(_ffn_ple_pieces(rows_ref, h_in_ref, p_ref, w_gate_ref, w_up_ref, w_down_ref,
                                g_ple_ref[...], ple_in_ref, ple_gate_ref, h_ref, pt_ref, act_ref,
                                lambda: None, emit))


def _ffn1_layer(x, p, prm):
    t_len, b_len, _ = x.shape
    tt, bb = _tiles(b_len, t_len, FFN_ROWS)
    return pl.pallas_call(
        functools.partial(_ffn1_kernel, tt=tt, bb=bb),
        grid=(b_len // bb, t_len // tt),
        in_specs=[
            pl.BlockSpec((tt, bb, D_MODEL), lambda j, i: (i, j, 0)),
            pl.BlockSpec((None, bb, tt, p.shape[-1]), lambda j, i: (1, j, i, 0)),
        ] + _ffn_weight_specs(prm, 1) + [_const_spec((1, D_MODEL))],
        out_specs=pl.BlockSpec((bb, tt, D_MODEL), lambda j, i: (j, i, 0)),
        out_shape=jax.ShapeDtypeStruct((b_len, t_len, D_MODEL), F32),
        scratch_shapes=_ffn_scratch(tt * bb, prm['ffn_w_gate'].shape[-1], p.shape[-1]),
        compiler_params=_params(),
        name="layer1_ffn",
    )(x, p, *_ffn_weights(prm), prm['g_final'])


def _run_trunk(x, p, start, pool_state, ssm_re, ssm_im, prm):
    b_len = x.shape[0]
    xt, pool_new = _layer0(x, pool_state, p, prm, start=start)
    xt, s_re, s_im = _ssm_layer(xt, ssm_re.reshape(b_len, SSM_FLAT), ssm_im.reshape(b_len, SSM_FLAT), prm)
    y = _ffn1_layer(xt, p, prm)
    s_re = s_re.reshape(1, b_len, SSM_GROUPS, SSM_STATE)
    s_im = s_im.reshape(1, b_len, SSM_GROUPS, SSM_STATE)
    return y, pool_new[None], s_re, s_im


def kernel(x_prompt, x_sample, state_pool, state_ssm_re, state_ssm_im, p_prompt, p_sample, g_mix, g_ffn, g_ple, g_final, pool_w, pool_scale, ssm_lambda_re, ssm_lambda_im, ssm_log_dt, ssm_b_re, ssm_b_im, ssm_c_re, ssm_c_im, ssm_d, ssm_w_glu, ffn_w_gate, ffn_w_up, ffn_w_down, ple_w_in, ple_w_gate):
    depth = g_mix.shape[0]
    a2_re, a2_im, ssm_in, ssm_out, ssm_direct = _ssm_prep(
        ssm_lambda_re[0], ssm_lambda_im[0], ssm_log_dt[0], ssm_b_re[0], ssm_b_im[0], ssm_c_re[0], ssm_c_im[0])
    per_layer_vec = lambda a: a.reshape(depth, 1, D_MODEL)
    prm = dict(
        g_mix=per_layer_vec(g_mix), g_ffn=per_layer_vec(g_ffn), g_ple=per_layer_vec(g_ple),
        g_final=g_final.reshape(1, D_MODEL),
        pool_w=pool_w[0].astype(BF16), pool_scale=pool_scale,
        a2_re=a2_re, a2_im=a2_im,
        ssm_in=ssm_in, ssm_out=ssm_out, ssm_direct=ssm_direct,
        ssm_d=ssm_d, ssm_w_glu=ssm_w_glu[0].astype(BF16),
        ffn_w_gate=ffn_w_gate.astype(BF16), ffn_w_up=ffn_w_up.astype(BF16),
        ffn_w_down=ffn_w_down.astype(BF16),
        ple_w_in=ple_w_in.astype(BF16), ple_w_gate=ple_w_gate.astype(BF16),
    )
    b_p = x_prompt.shape[0]
    zeros_pool = jnp.zeros((b_p, POOL_BUF, D_MODEL), F32)
    zeros_ssm = jnp.zeros((b_p, SSM_GROUPS, SSM_STATE), F32)
    y_p, pool_p, re_p, im_p = _run_trunk(x_prompt, p_prompt, 0, zeros_pool, zeros_ssm, zeros_ssm, prm)
    y_s, pool_s, re_s, im_s = _run_trunk(x_sample, p_sample, PAST_LEN, state_pool[0],
                                         state_ssm_re[0], state_ssm_im[0], prm)
    return (y_p, y_s, pool_p, pool_s, re_p, im_p, re_s, im_s)
```

```python
import functools
import math

import jax
import jax.numpy as jnp
from jax import lax
from jax.experimental import pallas as pl
from jax.experimental.pallas import tpu as pltpu

D_MODEL = 1024
POOL_WINDOWS = (2, 4, 8, 16)
POOL_GROUP_DIM = D_MODEL // len(POOL_WINDOWS)
POOL_BUF = max(POOL_WINDOWS) - 1
SSM_GROUP_DIM = 16
SSM_GROUPS = D_MODEL // SSM_GROUP_DIM
SSM_STATE = 64
SSM_FLAT = SSM_GROUPS * SSM_STATE
PAST_LEN = 16384
EPS = 1e-6

V7X_SUBLANES = 8
V7X_LANES = 128
V7X_MXU_DIM = 256
V7X_VMEM_LIMIT_BYTES = 60 * 1024 * 1024

SSM_COL_BLOCK = V7X_MXU_DIM
SSM_GROUPS_PER_BLOCK = SSM_COL_BLOCK // SSM_GROUP_DIM
SSM_STATE_BLOCK = SSM_GROUPS_PER_BLOCK * SSM_STATE
SSM_N_BLOCKS = D_MODEL // SSM_COL_BLOCK
SSM_LANE_GROUPS = V7X_LANES // SSM_GROUP_DIM
SSM_LANE_STATES = SSM_LANE_GROUPS * SSM_STATE
SSM_N_LANE_BLOCKS = D_MODEL // V7X_LANES
SSM_PAIR_COLS = 2 * V7X_LANES
SSM_PAIR_STATES = 2 * SSM_LANE_STATES

POOL_LAYER_ROWS = 256
SSM_ROWS = 512
FFN_ROWS = 1024
FFN_COLS = 2 * V7X_MXU_DIM
OUT_COLS = 2 * V7X_MXU_DIM

BF16 = jnp.bfloat16
F32 = jnp.float32


def _rmsnorm(x, g):
    return x * lax.rsqrt(jnp.mean(x * x, axis=-1, keepdims=True) + EPS) * g


def _dot(a, b):
    return jnp.dot(a, b, preferred_element_type=F32)


def _dot_nt_f32(a, b):
    return lax.dot_general(a, b, (((1,), (1,)), ((), ())), precision=lax.Precision.HIGHEST,
                           preferred_element_type=F32)


def _time_major(block):
    bb, tt, c = block.shape
    return jnp.transpose(block, (1, 0, 2)).reshape(tt * bb, c)


def _batch_major(rows, tt, bb):
    return jnp.transpose(rows.reshape(tt, bb, rows.shape[-1]), (1, 0, 2))


def _const_spec(shape):
    zeros = (0,) * len(shape)
    return pl.BlockSpec(shape, lambda *_: zeros, pipeline_mode=pl.Buffered(1))


def _layer_spec(shape, layer):
    tail = (0,) * len(shape)
    return pl.BlockSpec((None,) + tuple(shape), lambda *_: (layer,) + tail,
                        pipeline_mode=pl.Buffered(1))


def _params():
    return pltpu.CompilerParams(dimension_semantics=("arbitrary", "arbitrary"),
                                vmem_limit_bytes=V7X_VMEM_LIMIT_BYTES)


def _tiles(b_len, t_len, rows):
    bb = min(b_len, max(V7X_SUBLANES, rows // t_len))
    tt = min(t_len, rows // bb)
    assert t_len % tt == 0 and b_len % bb == 0 and bb % V7X_SUBLANES == 0
    return tt, bb


class _Schedule:
    def __init__(self, b_len, t_len, rows):
        self.tt, self.bb = _tiles(b_len, t_len, rows)
        self.nt = t_len // self.tt
        self.lag = 1 if self.nt > 1 else 0
        self.grid = (b_len // self.bb, self.nt + self.lag)

    def mixer_block(self, i):
        return jnp.minimum(i, self.nt - 1)

    def ffn_block(self, i):
        return jnp.maximum(i - self.lag, 0)


def _ssm_prep_kernel(lam_re_ref, lam_im_ref, log_dt_ref, b_re_ref, b_im_ref, c_re_ref, c_im_ref,
                     a2_re_ref, a2_im_ref, in_ref, out_ref, direct_ref,
                     lb_re_row, lb_im_row, wb_re, wb_im, wc_re, wc_im):
    lam_re = lam_re_ref[...]
    lam_im = lam_im_ref[...]
    dt = jnp.exp(log_dt_ref[...])
    mag = jnp.exp(lam_re * dt)
    lb_re = mag * jnp.cos(lam_im * dt)
    lb_im = mag * jnp.sin(lam_im * dt)
    den = lam_re * lam_re + lam_im * lam_im
    f_re = ((lb_re - 1.0) * lam_re + lb_im * lam_im) / den
    f_im = (lb_im * lam_re - (lb_re - 1.0) * lam_im) / den
    b_re = jnp.swapaxes(b_re_ref[...], 1, 2)
    b_im = jnp.swapaxes(b_im_ref[...], 1, 2)
    fr = f_re[:, None, :]
    fi = f_im[:, None, :]
    bb_re = fr * b_re - fi * b_im
    bb_im = fr * b_im + fi * b_re
    c_re = c_re_ref[...]
    c_im_neg = -c_im_ref[...]

    for ref in (wb_re, wb_im, wc_re, wc_im):
        ref[...] = jnp.zeros(ref.shape, ref.dtype)
    for g in range(SSM_GROUPS):
        cb, gl = divmod(g, SSM_GROUPS_PER_BLOCK)
        cols = slice(gl * SSM_GROUP_DIM, (gl + 1) * SSM_GROUP_DIM)
        states = slice(gl * SSM_STATE, (gl + 1) * SSM_STATE)
        wb_re[cb, cols, states] = bb_re[g]
        wb_im[cb, cols, states] = bb_im[g]
        wc_re[cb, cols, states] = c_re[g]
        wc_im[cb, cols, states] = c_im_neg[g]
        flat = slice(g * SSM_STATE, (g + 1) * SSM_STATE)
        lb_re_row[:, flat] = lb_re[g:g + 1, :]
        lb_im_row[:, flat] = lb_im[g:g + 1, :]

    lr_all = lb_re_row[...]
    li_all = lb_im_row[...]
    a2_re_ref[...] = lr_all * lr_all - li_all * li_all
    a2_im_ref[...] = 2.0 * lr_all * li_all

    halves = SSM_COL_BLOCK // V7X_LANES
    for cb in range(SSM_N_BLOCKS):
        flat = slice(cb * SSM_STATE_BLOCK, (cb + 1) * SSM_STATE_BLOCK)
        lr, li = lb_re_row[:, flat], lb_im_row[:, flat]
        l2r, l2i = a2_re_ref[:, flat], a2_im_ref[:, flat]
        w_r, w_i = wb_re[cb], wb_im[cb]
        wl_r, wl_i = w_r * lr - w_i * li, w_r * li + w_i * lr
        c_r, c_n = wc_re[cb], wc_im[cb]
        o1_r, o1_i = c_r * lr + c_n * li, c_n * lr - c_r * li
        o2_r, o2_i = c_r * l2r + c_n * l2i, c_n * l2r - c_r * l2i
        for half in range(halves):
            j = cb * halves + half
            r = slice(half * V7X_LANES, (half + 1) * V7X_LANES)
            s = slice(half * SSM_LANE_STATES, (half + 1) * SSM_LANE_STATES)
            t0, t1 = slice(0, V7X_LANES), slice(V7X_LANES, SSM_PAIR_COLS)
            re, im = slice(0, SSM_LANE_STATES), slice(SSM_LANE_STATES, SSM_PAIR_STATES)
            in_ref[j, t0, re] = wl_r[r, s].astype(BF16)
            in_ref[j, t0, im] = wl_i[r, s].astype(BF16)
            in_ref[j, t1, re] = w_r[r, s].astype(BF16)
            in_ref[j, t1, im] = w_i[r, s].astype(BF16)
            k0 = _dot_nt_f32(w_r[r, s], c_r[r, s]) + _dot_nt_f32(w_i[r, s], c_n[r, s])
            k1 = _dot_nt_f32(wl_r[r, s], c_r[r, s]) + _dot_nt_f32(wl_i[r, s], c_n[r, s])
            direct_ref[j, t0, t0] = k0.astype(BF16)
            direct_ref[j, t0, t1] = k1.astype(BF16)
            direct_ref[j, t1, t0] = jnp.zeros((V7X_LANES, V7X_LANES), BF16)
            direct_ref[j, t1, t1] = k0.astype(BF16)
            out_ref[j, re, t0] = o1_r[r, s].T.astype(BF16)
            out_ref[j, im, t0] = o1_i[r, s].T.astype(BF16)
            out_ref[j, re, t1] = o2_r[r, s].T.astype(BF16)
            out_ref[j, im, t1] = o2_i[r, s].T.astype(BF16)


def _ssm_prep(lam_re, lam_im, log_dt, b_re, b_im, c_re, c_im):
    g, p, _ = b_re.shape
    row = jax.ShapeDtypeStruct((1, SSM_FLAT), F32)
    n = SSM_N_LANE_BLOCKS
    block_diag = pltpu.VMEM((SSM_N_BLOCKS, SSM_COL_BLOCK, SSM_STATE_BLOCK), F32)
    return pl.pallas_call(
        _ssm_prep_kernel,
        out_shape=(row, row,
                   jax.ShapeDtypeStruct((n, SSM_PAIR_COLS, SSM_PAIR_STATES), BF16),
                   jax.ShapeDtypeStruct((n, SSM_PAIR_STATES, SSM_PAIR_COLS), BF16),
                   jax.ShapeDtypeStruct((n, SSM_PAIR_COLS, SSM_PAIR_COLS), BF16)),
        scratch_shapes=[pltpu.VMEM((1, SSM_FLAT), F32), pltpu.VMEM((1, SSM_FLAT), F32),
                        block_diag, block_diag, block_diag, block_diag],
        compiler_params=pltpu.CompilerParams(vmem_limit_bytes=V7X_VMEM_LIMIT_BYTES),
        name="ssm_prep",
    )(lam_re, lam_im, log_dt.reshape(g, 1), b_re, b_im, c_re, c_im)


def _pool_pieces(x_ref, dst_ref, ext_ref, w_ref, scale_ref, g, *, tt, bb, first_pos):
    tm = tt * bb
    halo = POOL_BUF * bb
    dst_ref[...] = _time_major(x_ref[...])
    ext_ref[halo:halo + tm, :] = _rmsnorm(dst_ref[...], g)
    yield
    row = lax.broadcasted_iota(jnp.int32, (tm, 1), 0)
    pos = first_pos + lax.shift_right_logical(row, int(math.log2(bb)))
    for gi, w in enumerate(POOL_WINDOWS):
        cols = slice(gi * POOL_GROUP_DIM, (gi + 1) * POOL_GROUP_DIM)
        acc = ext_ref[halo:halo + tm, cols]
        for k in range(1, w):
            acc = acc + ext_ref[halo - k * bb:halo - k * bb + tm, cols]
        inv_cnt = 1.0 / jnp.minimum(w, pos + 1).astype(F32)
        diff = acc * inv_cnt - ext_ref[halo:halo + tm, cols]
        dst_ref[:, cols] += _dot(diff.astype(BF16), w_ref[gi]) * scale_ref[:, cols]
        yield


def _ffn_ple_pieces(rows_ref, h_in_ref, p_ref, w_gate_ref, w_up_ref, w_down_ref, g_ple, ple_in_ref, ple_gate_ref,
                    h_ref, pt_ref, act_ref, release_h_in, emit):
    d_ff = act_ref.shape[1]
    pt_ref[...] = _time_major(p_ref[...]).astype(BF16)
    for c0 in range(0, d_ff, FFN_COLS):
        cols = slice(c0, min(c0 + FFN_COLS, d_ff))
        gate = _dot(h_in_ref[...], w_gate_ref[:, cols])
        up = _dot(h_in_ref[...], w_up_ref[:, cols])
        act_ref[:, cols] = (jax.nn.silu(gate) * up).astype(BF16)
        if cols.stop == d_ff:
            release_h_in()
        yield
    for n0 in range(0, D_MODEL, OUT_COLS):
        cols = slice(n0, n0 + OUT_COLS)
        rows_ref[:, cols] += _dot(act_ref[...], w_down_ref[:, cols])
        yield
    h_ref[...] = _rmsnorm(rows_ref[...], g_ple).astype(BF16)
    for n0 in range(0, D_MODEL, OUT_COLS):
        cols = slice(n0, n0 + OUT_COLS)
        ple_gate = jax.nn.sigmoid(_dot(h_ref[...], ple_gate_ref[:, cols]))
        rows_ref[:, cols] += _dot(pt_ref[...], ple_in_ref[:, cols]) * ple_gate
        yield
    emit()
    yield


_DONE = object()


def _interleave(*stages):
    live = list(stages)
    while live:
        for stage in list(live):
            if next(stage, _DONE) is _DONE:
                live.remove(stage)


def _then(stage, last_piece):
    yield from stage
    last_piece()
    yield


def _run_halves(sched, g_ffn, mid_ref, hmid_ref, rows_ref, mixer, ffn):
    h_in_reads_issued = []

    def mixer_into(dst_ref):
        def norm_for_ffn():
            assert h_in_reads_issued or not sched.lag, "hmid_ref rewritten before the FFN half read it"
            hmid_ref[...] = _rmsnorm(dst_ref[...], g_ffn).astype(BF16)
        return _then(mixer(dst_ref), norm_for_ffn)

    def release_h_in():
        h_in_reads_issued.append(True)

    if sched.lag:
        rows_ref[...] = mid_ref[...]
        _interleave(ffn(hmid_ref, release_h_in), mixer_into(mid_ref))
    else:
        _interleave(mixer_into(rows_ref))
        _interleave(ffn(hmid_ref, release_h_in))


def _init_handoff(sched, mid_ref, hmid_ref):
    if sched.lag:
        mid_ref[...] = jnp.zeros(mid_ref.shape, mid_ref.dtype)
        hmid_ref[...] = jnp.zeros(hmid_ref.shape, hmid_ref.dtype)


def _ffn_scratch(tm, d_ff, ple_dim):
    return [
        pltpu.VMEM((tm, D_MODEL), BF16),
        pltpu.VMEM((tm, D_MODEL), F32),
        pltpu.VMEM((tm, D_MODEL), BF16),
        pltpu.VMEM((tm, ple_dim), BF16),
        pltpu.VMEM((tm, d_ff), BF16),
    ]


def _layer0_kernel(x_ref, prev_ref, p_ref, g_mix_ref, pool_w_ref, pool_scale_ref,
                   g_ffn_ref, w_gate_ref, w_up_ref, w_down_ref, g_ple_ref, ple_in_ref, ple_gate_ref,
                   o_ref, state_ref, ext_ref, mid_ref, hmid_ref, rows_ref, h_ref, pt_ref, act_ref,
                   *, sched, start):
    tt, bb = sched.tt, sched.bb
    tm = tt * bb
    halo = POOL_BUF * bb
    i = pl.program_id(1)

    @pl.when(i == 0)
    def _():
        ext_ref[0:halo, :] = _time_major(prev_ref[...])
        _init_handoff(sched, mid_ref, hmid_ref)

    @pl.when(i > 0)
    def _():
        ext_ref[0:halo, :] = ext_ref[tm:tm + halo, :]

    def mixer(dst_ref):
        return _pool_pieces(x_ref, dst_ref, ext_ref, pool_w_ref, pool_scale_ref, g_mix_ref[...],
                            tt=tt, bb=bb, first_pos=start + sched.mixer_block(i) * tt)

    def emit():
        o_ref[...] = rows_ref[...].reshape(tt, bb, D_MODEL)

    def ffn(h_in_ref, release_h_in):
        return _ffn_ple_pieces(rows_ref, h_in_ref, p_ref, w_gate_ref, w_up_ref, w_down_ref,
                               g_ple_ref[...], ple_in_ref, ple_gate_ref, h_ref, pt_ref, act_ref,
                               release_h_in, emit)

    _run_halves(sched, g_ffn_ref[...], mid_ref, hmid_ref, rows_ref, mixer, ffn)

    @pl.when(i == sched.nt - 1)
    def _():
        state_ref[...] = _batch_major(ext_ref[tm:tm + halo, :], POOL_BUF, bb)


def _ffn_weight_specs(prm, layer):
    return [
        _layer_spec((1, D_MODEL), layer),
        _layer_spec(prm['ffn_w_gate'].shape[1:], layer), _layer_spec(prm['ffn_w_up'].shape[1:], layer),
        _layer_spec(prm['ffn_w_down'].shape[1:], layer),
        _layer_spec((1, D_MODEL), layer),
        _layer_spec(prm['ple_w_in'].shape[1:], layer), _layer_spec(prm['ple_w_gate'].shape[1:], layer),
    ]


def _ffn_weights(prm):
    return (prm['g_ffn'], prm['ffn_w_gate'], prm['ffn_w_up'], prm['ffn_w_down'],
            prm['g_ple'], prm['ple_w_in'], prm['ple_w_gate'])


def _layer0(x, prev, p, prm, *, start):
    b_len, t_len, _ = x.shape
    sched = _Schedule(b_len, t_len, POOL_LAYER_ROWS)
    tt, bb = sched.tt, sched.bb
    tm = tt * bb
    state_spec = pl.BlockSpec((bb, POOL_BUF, D_MODEL), lambda j, i: (j, 0, 0))
    mid_rows = tm if sched.lag else 2 * V7X_SUBLANES
    return pl.pallas_call(
        functools.partial(_layer0_kernel, sched=sched, start=start),
        grid=sched.grid,
        in_specs=[
            pl.BlockSpec((bb, tt, D_MODEL), lambda j, i: (j, sched.mixer_block(i), 0)),
            state_spec,
            pl.BlockSpec((None, bb, tt, p.shape[-1]), lambda j, i: (0, j, sched.ffn_block(i), 0)),
            _layer_spec((1, D_MODEL), 0),
            _const_spec(prm['pool_w'].shape),
            _const_spec((1, D_MODEL)),
        ] + _ffn_weight_specs(prm, 0),
        out_specs=[pl.BlockSpec((tt, bb, D_MODEL), lambda j, i: (sched.ffn_block(i), j, 0)), state_spec],
        out_shape=[
            jax.ShapeDtypeStruct((t_len, b_len, D_MODEL), F32),
            jax.ShapeDtypeStruct((b_len, POOL_BUF, D_MODEL), F32),
        ],
        scratch_shapes=([pltpu.VMEM(((POOL_BUF + tt) * bb, D_MODEL), F32),
                         pltpu.VMEM((mid_rows, D_MODEL), F32)]
                        + _ffn_scratch(tm, prm['ffn_w_gate'].shape[-1], p.shape[-1])),
        compiler_params=_params(),
        name="layer0_pool_ffn",
    )(x, prev, p, prm['g_mix'], prm['pool_w'], prm['pool_scale'], *_ffn_weights(prm))


def _ssm_kernel(x_ref, h0_re_ref, h0_im_ref, g_ref, a2_re_ref, a2_im_ref, in_ref, out_ref, direct_ref,
                d_ref, wglu_ref, o_ref, s_re_ref, s_im_ref,
                st_re, st_im, u_ref, xj_ref, sbuf_ref, z_ref, *, pairs, bb):
    m = pairs * bb
    i = pl.program_id(1)

    @pl.when(i == 0)
    def _():
        st_re[...] = h0_re_ref[...].reshape(bb, SSM_FLAT)
        st_im[...] = h0_im_ref[...].reshape(bb, SSM_FLAT)

    for t in range(2):
        u_ref[t] = _rmsnorm(x_ref[:, t].reshape(m, D_MODEL), g_ref[...])

    def contributions(j):
        cols = slice(j * V7X_LANES, (j + 1) * V7X_LANES)
        both = jnp.concatenate([u_ref[0, :, cols], u_ref[1, :, cols]], axis=-1)
        xj_ref[j] = both.astype(BF16)
        sbuf_ref[:, j * SSM_PAIR_STATES:(j + 1) * SSM_PAIR_STATES] = _dot(xj_ref[j], in_ref[j])

    def scan(j):
        states = slice(j * SSM_LANE_STATES, (j + 1) * SSM_LANE_STATES)
        re = slice(j * SSM_PAIR_STATES, j * SSM_PAIR_STATES + SSM_LANE_STATES)
        im = slice(j * SSM_PAIR_STATES + SSM_LANE_STATES, (j + 1) * SSM_PAIR_STATES)
        ar = jnp.broadcast_to(a2_re_ref[:, states], (V7X_SUBLANES, SSM_LANE_STATES))
        ai = jnp.broadcast_to(a2_im_ref[:, states], (V7X_SUBLANES, SSM_LANE_STATES))
        for r0 in range(0, bb, V7X_SUBLANES):
            sr = st_re[r0:r0 + V7X_SUBLANES, states]
            si = st_im[r0:r0 + V7X_SUBLANES, states]
            for c in range(pairs):
                rows = slice(c * bb + r0, c * bb + r0 + V7X_SUBLANES)
                pr, pi = sbuf_ref[rows, re], sbuf_ref[rows, im]
                sbuf_ref[rows, re] = sr
                sbuf_ref[rows, im] = si
                sr, si = ar * sr - ai * si + pr, ar * si + ai * sr + pi
            st_re[r0:r0 + V7X_SUBLANES, states] = sr
            st_im[r0:r0 + V7X_SUBLANES, states] = si

    def outputs(j):
        cols = slice(j * V7X_LANES, (j + 1) * V7X_LANES)
        entering = sbuf_ref[:, j * SSM_PAIR_STATES:(j + 1) * SSM_PAIR_STATES].astype(BF16)
        y = _dot(entering, out_ref[j]) + _dot(xj_ref[j], direct_ref[j])
        for t in range(2):
            yt = y[:, t * V7X_LANES:(t + 1) * V7X_LANES] + d_ref[:, cols] * u_ref[t, :, cols]
            z_ref[t, :, cols] = jax.nn.gelu(yt).astype(BF16)

    for k in range(SSM_N_LANE_BLOCKS + 2):
        if 0 <= k - 2:
            outputs(k - 2)
        if k < SSM_N_LANE_BLOCKS:
            contributions(k)
        if 0 <= k - 1 < SSM_N_LANE_BLOCKS:
            scan(k - 1)

    for t in range(2):
        for n0 in range(0, D_MODEL, OUT_COLS):
            cols = slice(n0, n0 + OUT_COLS)
            a = _dot(z_ref[t], wglu_ref[:, cols])
            gate = _dot(z_ref[t], wglu_ref[:, D_MODEL + n0:D_MODEL + n0 + OUT_COLS])
            o_ref[:, t, :, cols] = x_ref[:, t, :, cols] + (a * jax.nn.sigmoid(gate)).reshape(pairs, bb, OUT_COLS)

    @pl.when(i == pl.num_programs(1) - 1)
    def _():
        s_re_ref[...] = st_re[...].reshape(bb, SSM_GROUPS, SSM_STATE)
        s_im_ref[...] = st_im[...].reshape(bb, SSM_GROUPS, SSM_STATE)


def _ssm_layer(x, h0_re, h0_im, prm):
    t_len, b_len, _ = x.shape
    tt, bb = _tiles(b_len, t_len, SSM_ROWS)
    assert tt % 2 == 0
    pairs = tt // 2
    m = pairs * bb
    x_spec = pl.BlockSpec((pairs, 2, bb, D_MODEL), lambda j, i: (i, 0, j, 0))
    st_spec = pl.BlockSpec((bb, SSM_GROUPS, SSM_STATE), lambda j, i: (j, 0, 0))
    out, s_re, s_im = pl.pallas_call(
        functools.partial(_ssm_kernel, pairs=pairs, bb=bb),
        grid=(b_len // bb, t_len // tt),
        in_specs=[
            x_spec, st_spec, st_spec,
            _layer_spec((1, D_MODEL), 1),
            _const_spec((1, SSM_FLAT)), _const_spec((1, SSM_FLAT)),
            _const_spec(prm['ssm_in'].shape), _const_spec(prm['ssm_out'].shape),
            _const_spec(prm['ssm_direct'].shape),
            _const_spec((1, D_MODEL)),
            _const_spec(prm['ssm_w_glu'].shape),
        ],
        out_specs=[x_spec, st_spec, st_spec],
        out_shape=[
            jax.ShapeDtypeStruct((t_len // 2, 2, b_len, D_MODEL), F32),
            jax.ShapeDtypeStruct((b_len, SSM_GROUPS, SSM_STATE), F32),
            jax.ShapeDtypeStruct((b_len, SSM_GROUPS, SSM_STATE), F32),
        ],
        scratch_shapes=[
            pltpu.VMEM((bb, SSM_FLAT), F32), pltpu.VMEM((bb, SSM_FLAT), F32),
            pltpu.VMEM((2, m, D_MODEL), F32),
            pltpu.VMEM((SSM_N_LANE_BLOCKS, m, SSM_PAIR_COLS), BF16),
            pltpu.VMEM((m, SSM_N_LANE_BLOCKS * SSM_PAIR_STATES), F32),
            pltpu.VMEM((2, m, D_MODEL), BF16),
        ],
        compiler_params=_params(),
        name="ssm_mixer",
    )(x.reshape(t_len // 2, 2, b_len, D_MODEL), h0_re, h0_im, prm['g_mix'], prm['a2_re'], prm['a2_im'],
      prm['ssm_in'], prm['ssm_out'], prm['ssm_direct'], prm['ssm_d'], prm['ssm_w_glu'])
    return out.reshape(t_len, b_len, D_MODEL), s_re, s_im


def _ffn1_kernel(x_ref, p_ref, g_ffn_ref, w_gate_ref, w_up_ref, w_down_ref, g_ple_ref, ple_in_ref, ple_gate_ref,
                 g_final_ref, o_ref, h_in_ref, rows_ref, h_ref, pt_ref, act_ref, *, tt, bb):
    rows_ref[...] = x_ref[...].reshape(tt * bb, D_MODEL)
    h_in_ref[...] = _rmsnorm(rows_ref[...], g_ffn_ref[...]).astype(BF16)

    def emit():
        o_ref[...] = _batch_major(_rmsnorm(rows_ref[...], g_final_ref[...]), tt, bb)

    _interleave(_ffn_ple_pieces(rows_ref, h_in_ref, p_ref, w_gate_ref, w_up_ref, w_down_ref,
                                g_ple_ref[...], ple_in_ref, ple_gate_ref, h_ref, pt_ref, act_ref,
                                lambda: None, emit))


def _ffn1_layer(x, p, prm):
    t_len, b_len, _ = x.shape
    tt, bb = _tiles(b_len, t_len, FFN_ROWS)
    return pl.pallas_call(
        functools.partial(_ffn1_kernel, tt=tt, bb=bb),
        grid=(b_len // bb, t_len // tt),
        in_specs=[
            pl.BlockSpec((tt, bb, D_MODEL), lambda j, i: (i, j, 0)),
            pl.BlockSpec((None, bb, tt, p.shape[-1]), lambda j, i: (1, j, i, 0)),
        ] + _ffn_weight_specs(prm, 1) + [_const_spec((1, D_MODEL))],
        out_specs=pl.BlockSpec((bb, tt, D_MODEL), lambda j, i: (j, i, 0)),
        out_shape=jax.ShapeDtypeStruct((b_len, t_len, D_MODEL), F32),
        scratch_shapes=_ffn_scratch(tt * bb, prm['ffn_w_gate'].shape[-1], p.shape[-1]),
        compiler_params=_params(),
        name="layer1_ffn",
    )(x, p, *_ffn_weights(prm), prm['g_final'])


def _run_trunk(x, p, start, pool_state, ssm_re, ssm_im, prm):
    xt, pool_new = _layer0(x, pool_state, p, prm, start=start)
    xt, s_re, s_im = _ssm_layer(xt, ssm_re, ssm_im, prm)
    y = _ffn1_layer(xt, p, prm)
    return y, pool_new[None], s_re[None], s_im[None]


def kernel(x_prompt, x_sample, state_pool, state_ssm_re, state_ssm_im, p_prompt, p_sample, g_mix, g_ffn, g_ple, g_final, pool_w, pool_scale, ssm_lambda_re, ssm_lambda_im, ssm_log_dt, ssm_b_re, ssm_b_im, ssm_c_re, ssm_c_im, ssm_d, ssm_w_glu, ffn_w_gate, ffn_w_up, ffn_w_down, ple_w_in, ple_w_gate):
    depth = g_mix.shape[0]
    a2_re, a2_im, ssm_in, ssm_out, ssm_direct = _ssm_prep(
        ssm_lambda_re[0], ssm_lambda_im[0], ssm_log_dt[0], ssm_b_re[0], ssm_b_im[0], ssm_c_re[0], ssm_c_im[0])
    per_layer_vec = lambda a: a.reshape(depth, 1, D_MODEL)
    prm = dict(
        g_mix=per_layer_vec(g_mix), g_ffn=per_layer_vec(g_ffn), g_ple=per_layer_vec(g_ple),
        g_final=g_final.reshape(1, D_MODEL),
        pool_w=pool_w[0].astype(BF16), pool_scale=pool_scale,
        a2_re=a2_re, a2_im=a2_im,
        ssm_in=ssm_in, ssm_out=ssm_out, ssm_direct=ssm_direct,
        ssm_d=ssm_d, ssm_w_glu=ssm_w_glu[0].astype(BF16),
        ffn_w_gate=ffn_w_gate.astype(BF16), ffn_w_up=ffn_w_up.astype(BF16),
        ffn_w_down=ffn_w_down.astype(BF16),
        ple_w_in=ple_w_in.astype(BF16), ple_w_gate=ple_w_gate.astype(BF16),
    )
    b_p = x_prompt.shape[0]
    zeros_pool = jnp.zeros((b_p, POOL_BUF, D_MODEL), F32)
    zeros_ssm = jnp.zeros((b_p, SSM_GROUPS, SSM_STATE), F32)
    y_p, pool_p, re_p, im_p = _run_trunk(x_prompt, p_prompt, 0, zeros_pool, zeros_ssm, zeros_ssm, prm)
    y_s, pool_s, re_s, im_s = _run_trunk(x_sample, p_sample, PAST_LEN, state_pool[0],
                                         state_ssm_re[0], state_ssm_im[0], prm)
    return (y_p, y_s, pool_p, pool_s, re_p, im_p, re_s, im_s)
```

```python
import functools
import math

import jax
import jax.numpy as jnp
from jax import lax
from jax.experimental import pallas as pl
from jax.experimental.pallas import tpu as pltpu

D_MODEL = 1024
POOL_WINDOWS = (2, 4, 8, 16)
POOL_GROUP_DIM = D_MODEL // len(POOL_WINDOWS)
POOL_BUF = max(POOL_WINDOWS) - 1
SSM_GROUP_DIM = 16
SSM_GROUPS = D_MODEL // SSM_GROUP_DIM
SSM_STATE = 64
SSM_FLAT = SSM_GROUPS * SSM_STATE
PAST_LEN = 16384
EPS = 1e-6

V7X_SUBLANES = 8
V7X_LANES = 128
V7X_MXU_DIM = 256
V7X_VMEM_LIMIT_BYTES = 60 * 1024 * 1024

SSM_COL_BLOCK = V7X_MXU_DIM
SSM_GROUPS_PER_BLOCK = SSM_COL_BLOCK // SSM_GROUP_DIM
SSM_STATE_BLOCK = SSM_GROUPS_PER_BLOCK * SSM_STATE
SSM_N_BLOCKS = D_MODEL // SSM_COL_BLOCK
SSM_LANE_GROUPS = V7X_LANES // SSM_GROUP_DIM
SSM_LANE_STATES = SSM_LANE_GROUPS * SSM_STATE
SSM_N_LANE_BLOCKS = D_MODEL // V7X_LANES
SSM_PAIR_COLS = 2 * V7X_LANES
SSM_PAIR_STATES = 2 * SSM_LANE_STATES

POOL_LAYER_ROWS = 512
SSM_ROWS = 1024
FFN_ROWS = 1024
FFN_COLS = 2 * V7X_MXU_DIM
OUT_COLS = 2 * V7X_MXU_DIM

BF16 = jnp.bfloat16
F32 = jnp.float32


def _rmsnorm(x, g):
    return x * lax.rsqrt(jnp.mean(x * x, axis=-1, keepdims=True) + EPS) * g


def _dot(a, b):
    return jnp.dot(a, b, preferred_element_type=F32)


def _dot_nt_f32(a, b):
    return lax.dot_general(a, b, (((1,), (1,)), ((), ())), precision=lax.Precision.HIGHEST,
                           preferred_element_type=F32)


def _time_major(block):
    bb, tt, c = block.shape
    return jnp.transpose(block, (1, 0, 2)).reshape(tt * bb, c)


def _batch_major(rows, tt, bb):
    return jnp.transpose(rows.reshape(tt, bb, rows.shape[-1]), (1, 0, 2))


def _const_spec(shape):
    zeros = (0,) * len(shape)
    return pl.BlockSpec(shape, lambda *_: zeros, pipeline_mode=pl.Buffered(1))


def _layer_spec(shape, layer):
    tail = (0,) * len(shape)
    return pl.BlockSpec((None,) + tuple(shape), lambda *_: (layer,) + tail,
                        pipeline_mode=pl.Buffered(1))


def _params():
    return pltpu.CompilerParams(dimension_semantics=("arbitrary", "arbitrary"),
                                vmem_limit_bytes=V7X_VMEM_LIMIT_BYTES)


def _tiles(b_len, t_len, rows):
    bb = min(b_len, max(V7X_SUBLANES, rows // t_len))
    tt = min(t_len, rows // bb)
    assert t_len % tt == 0 and b_len % bb == 0 and bb % V7X_SUBLANES == 0
    return tt, bb


class _Schedule:
    def __init__(self, b_len, t_len, rows):
        self.tt, self.bb = _tiles(b_len, t_len, rows)
        self.nt = t_len // self.tt
        self.lag = 1 if self.nt > 1 else 0
        self.grid = (b_len // self.bb, self.nt + self.lag)

    def mixer_block(self, i):
        return jnp.minimum(i, self.nt - 1)

    def ffn_block(self, i):
        return jnp.maximum(i - self.lag, 0)


def _ssm_prep_kernel(lam_re_ref, lam_im_ref, log_dt_ref, b_re_ref, b_im_ref, c_re_ref, c_im_ref,
                     a2_re_ref, a2_im_ref, in_ref, out_ref, direct_ref,
                     lb_re_row, lb_im_row, wb_re, wb_im, wc_re, wc_im):
    lam_re = lam_re_ref[...]
    lam_im = lam_im_ref[...]
    dt = jnp.exp(log_dt_ref[...])
    mag = jnp.exp(lam_re * dt)
    lb_re = mag * jnp.cos(lam_im * dt)
    lb_im = mag * jnp.sin(lam_im * dt)
    den = lam_re * lam_re + lam_im * lam_im
    f_re = ((lb_re - 1.0) * lam_re + lb_im * lam_im) / den
    f_im = (lb_im * lam_re - (lb_re - 1.0) * lam_im) / den
    b_re = jnp.swapaxes(b_re_ref[...], 1, 2)
    b_im = jnp.swapaxes(b_im_ref[...], 1, 2)
    fr = f_re[:, None, :]
    fi = f_im[:, None, :]
    bb_re = fr * b_re - fi * b_im
    bb_im = fr * b_im + fi * b_re
    c_re = c_re_ref[...]
    c_im_neg = -c_im_ref[...]

    for ref in (wb_re, wb_im, wc_re, wc_im):
        ref[...] = jnp.zeros(ref.shape, ref.dtype)
    for g in range(SSM_GROUPS):
        cb, gl = divmod(g, SSM_GROUPS_PER_BLOCK)
        cols = slice(gl * SSM_GROUP_DIM, (gl + 1) * SSM_GROUP_DIM)
        states = slice(gl * SSM_STATE, (gl + 1) * SSM_STATE)
        wb_re[cb, cols, states] = bb_re[g]
        wb_im[cb, cols, states] = bb_im[g]
        wc_re[cb, cols, states] = c_re[g]
        wc_im[cb, cols, states] = c_im_neg[g]
        flat = slice(g * SSM_STATE, (g + 1) * SSM_STATE)
        lb_re_row[:, flat] = lb_re[g:g + 1, :]
        lb_im_row[:, flat] = lb_im[g:g + 1, :]

    lr_all = lb_re_row[...]
    li_all = lb_im_row[...]
    a2_re_ref[...] = lr_all * lr_all - li_all * li_all
    a2_im_ref[...] = 2.0 * lr_all * li_all

    halves = SSM_COL_BLOCK // V7X_LANES
    for cb in range(SSM_N_BLOCKS):
        flat = slice(cb * SSM_STATE_BLOCK, (cb + 1) * SSM_STATE_BLOCK)
        lr, li = lb_re_row[:, flat], lb_im_row[:, flat]
        l2r, l2i = a2_re_ref[:, flat], a2_im_ref[:, flat]
        w_r, w_i = wb_re[cb], wb_im[cb]
        wl_r, wl_i = w_r * lr - w_i * li, w_r * li + w_i * lr
        c_r, c_n = wc_re[cb], wc_im[cb]
        o1_r, o1_i = c_r * lr + c_n * li, c_n * lr - c_r * li
        o2_r, o2_i = c_r * l2r + c_n * l2i, c_n * l2r - c_r * l2i
        for half in range(halves):
            j = cb * halves + half
            r = slice(half * V7X_LANES, (half + 1) * V7X_LANES)
            s = slice(half * SSM_LANE_STATES, (half + 1) * SSM_LANE_STATES)
            t0, t1 = slice(0, V7X_LANES), slice(V7X_LANES, SSM_PAIR_COLS)
            re, im = slice(0, SSM_LANE_STATES), slice(SSM_LANE_STATES, SSM_PAIR_STATES)
            in_ref[j, t0, re] = wl_r[r, s].astype(BF16)
            in_ref[j, t0, im] = wl_i[r, s].astype(BF16)
            in_ref[j, t1, re] = w_r[r, s].astype(BF16)
            in_ref[j, t1, im] = w_i[r, s].astype(BF16)
            k0 = _dot_nt_f32(w_r[r, s], c_r[r, s]) + _dot_nt_f32(w_i[r, s], c_n[r, s])
            k1 = _dot_nt_f32(wl_r[r, s], c_r[r, s]) + _dot_nt_f32(wl_i[r, s], c_n[r, s])
            direct_ref[j, t0, t0] = k0.astype(BF16)
            direct_ref[j, t0, t1] = k1.astype(BF16)
            direct_ref[j, t1, t0] = jnp.zeros((V7X_LANES, V7X_LANES), BF16)
            direct_ref[j, t1, t1] = k0.astype(BF16)
            out_ref[j, re, t0] = o1_r[r, s].T.astype(BF16)
            out_ref[j, im, t0] = o1_i[r, s].T.astype(BF16)
            out_ref[j, re, t1] = o2_r[r, s].T.astype(BF16)
            out_ref[j, im, t1] = o2_i[r, s].T.astype(BF16)


def _ssm_prep(lam_re, lam_im, log_dt, b_re, b_im, c_re, c_im):
    g, p, _ = b_re.shape
    row = jax.ShapeDtypeStruct((1, SSM_FLAT), F32)
    n = SSM_N_LANE_BLOCKS
    block_diag = pltpu.VMEM((SSM_N_BLOCKS, SSM_COL_BLOCK, SSM_STATE_BLOCK), F32)
    return pl.pallas_call(
        _ssm_prep_kernel,
        out_shape=(row, row,
                   jax.ShapeDtypeStruct((n, SSM_PAIR_COLS, SSM_PAIR_STATES), BF16),
                   jax.ShapeDtypeStruct((n, SSM_PAIR_STATES, SSM_PAIR_COLS), BF16),
                   jax.ShapeDtypeStruct((n, SSM_PAIR_COLS, SSM_PAIR_COLS), BF16)),
        scratch_shapes=[pltpu.VMEM((1, SSM_FLAT), F32), pltpu.VMEM((1, SSM_FLAT), F32),
                        block_diag, block_diag, block_diag, block_diag],
        compiler_params=pltpu.CompilerParams(vmem_limit_bytes=V7X_VMEM_LIMIT_BYTES),
        name="ssm_prep",
    )(lam_re, lam_im, log_dt.reshape(g, 1), b_re, b_im, c_re, c_im)


def _pool_pieces(x_ref, dst_ref, ext_ref, w_ref, scale_ref, g, *, tt, bb, first_pos):
    tm = tt * bb
    halo = POOL_BUF * bb
    dst_ref[...] = _time_major(x_ref[...])
    ext_ref[halo:halo + tm, :] = _rmsnorm(dst_ref[...], g)
    yield
    row = lax.broadcasted_iota(jnp.int32, (tm, 1), 0)
    pos = first_pos + lax.shift_right_logical(row, int(math.log2(bb)))
    for gi, w in enumerate(POOL_WINDOWS):
        cols = slice(gi * POOL_GROUP_DIM, (gi + 1) * POOL_GROUP_DIM)
        acc = ext_ref[halo:halo + tm, cols]
        for k in range(1, w):
            acc = acc + ext_ref[halo - k * bb:halo - k * bb + tm, cols]
        inv_cnt = 1.0 / jnp.minimum(w, pos + 1).astype(F32)
        diff = acc * inv_cnt - ext_ref[halo:halo + tm, cols]
        dst_ref[:, cols] += _dot(diff.astype(BF16), w_ref[gi]) * scale_ref[:, cols]
        yield


def _ffn_ple_pieces(rows_ref, h_in_ref, p_ref, w_gate_ref, w_up_ref, w_down_ref, g_ple, ple_in_ref, ple_gate_ref,
                    h_ref, pt_ref, act_ref, release_h_in, emit):
    d_ff = act_ref.shape[1]
    pt_ref[...] = _time_major(p_ref[...]).astype(BF16)
    for c0 in range(0, d_ff, FFN_COLS):
        cols = slice(c0, min(c0 + FFN_COLS, d_ff))
        gate = _dot(h_in_ref[...], w_gate_ref[:, cols])
        up = _dot(h_in_ref[...], w_up_ref[:, cols])
        act_ref[:, cols] = (jax.nn.silu(gate) * up).astype(BF16)
        if cols.stop == d_ff:
            release_h_in()
        yield
    for n0 in range(0, D_MODEL, OUT_COLS):
        cols = slice(n0, n0 + OUT_COLS)
        rows_ref[:, cols] += _dot(act_ref[...], w_down_ref[:, cols])
        yield
    h_ref[...] = _rmsnorm(rows_ref[...], g_ple).astype(BF16)
    for n0 in range(0, D_MODEL, OUT_COLS):
        cols = slice(n0, n0 + OUT_COLS)
        ple_gate = jax.nn.sigmoid(_dot(h_ref[...], ple_gate_ref[:, cols]))
        rows_ref[:, cols] += _dot(pt_ref[...], ple_in_ref[:, cols]) * ple_gate
        yield
    emit()
    yield


_DONE = object()


def _interleave(*stages):
    live = list(stages)
    while live:
        for stage in list(live):
            if next(stage, _DONE) is _DONE:
                live.remove(stage)


def _then(stage, last_piece):
    yield from stage
    last_piece()
    yield


def _run_halves(sched, g_ffn, mid_ref, hmid_ref, rows_ref, mixer, ffn):
    h_in_reads_issued = []

    def mixer_into(dst_ref):
        def norm_for_ffn():
            assert h_in_reads_issued or not sched.lag, "hmid_ref rewritten before the FFN half read it"
            hmid_ref[...] = _rmsnorm(dst_ref[...], g_ffn).astype(BF16)
        return _then(mixer(dst_ref), norm_for_ffn)

    def release_h_in():
        h_in_reads_issued.append(True)

    if sched.lag:
        rows_ref[...] = mid_ref[...]
        _interleave(ffn(hmid_ref, release_h_in), mixer_into(mid_ref))
    else:
        _interleave(mixer_into(rows_ref))
        _interleave(ffn(hmid_ref, release_h_in))


def _init_handoff(sched, mid_ref, hmid_ref):
    if sched.lag:
        mid_ref[...] = jnp.zeros(mid_ref.shape, mid_ref.dtype)
        hmid_ref[...] = jnp.zeros(hmid_ref.shape, hmid_ref.dtype)


def _ffn_scratch(tm, d_ff, ple_dim):
    return [
        pltpu.VMEM((tm, D_MODEL), BF16),
        pltpu.VMEM((tm, D_MODEL), F32),
        pltpu.VMEM((tm, D_MODEL), BF16),
        pltpu.VMEM((tm, ple_dim), BF16),
        pltpu.VMEM((tm, d_ff), BF16),
    ]


def _layer0_kernel(x_ref, prev_ref, p_ref, g_mix_ref, pool_w_ref, pool_scale_ref,
                   g_ffn_ref, w_gate_ref, w_up_ref, w_down_ref, g_ple_ref, ple_in_ref, ple_gate_ref,
                   o_ref, state_ref, ext_ref, mid_ref, hmid_ref, rows_ref, h_ref, pt_ref, act_ref,
                   *, sched, start):
    tt, bb = sched.tt, sched.bb
    tm = tt * bb
    halo = POOL_BUF * bb
    i = pl.program_id(1)

    @pl.when(i == 0)
    def _():
        ext_ref[0:halo, :] = _time_major(prev_ref[...])
        _init_handoff(sched, mid_ref, hmid_ref)

    @pl.when(i > 0)
    def _():
        ext_ref[0:halo, :] = ext_ref[tm:tm + halo, :]

    def mixer(dst_ref):
        return _pool_pieces(x_ref, dst_ref, ext_ref, pool_w_ref, pool_scale_ref, g_mix_ref[...],
                            tt=tt, bb=bb, first_pos=start + sched.mixer_block(i) * tt)

    def emit():
        o_ref[...] = rows_ref[...].reshape(tt, bb, D_MODEL)

    def ffn(h_in_ref, release_h_in):
        return _ffn_ple_pieces(rows_ref, h_in_ref, p_ref, w_gate_ref, w_up_ref, w_down_ref,
                               g_ple_ref[...], ple_in_ref, ple_gate_ref, h_ref, pt_ref, act_ref,
                               release_h_in, emit)

    _run_halves(sched, g_ffn_ref[...], mid_ref, hmid_ref, rows_ref, mixer, ffn)

    @pl.when(i == sched.nt - 1)
    def _():
        state_ref[...] = _batch_major(ext_ref[tm:tm + halo, :], POOL_BUF, bb)


def _ffn_weight_specs(prm, layer):
    return [
        _layer_spec((1, D_MODEL), layer),
        _layer_spec(prm['ffn_w_gate'].shape[1:], layer), _layer_spec(prm['ffn_w_up'].shape[1:], layer),
        _layer_spec(prm['ffn_w_down'].shape[1:], layer),
        _layer_spec((1, D_MODEL), layer),
        _layer_spec(prm['ple_w_in'].shape[1:], layer), _layer_spec(prm['ple_w_gate'].shape[1:], layer),
    ]


def _ffn_weights(prm):
    return (prm['g_ffn'], prm['ffn_w_gate'], prm['ffn_w_up'], prm['ffn_w_down'],
            prm['g_ple'], prm['ple_w_in'], prm['ple_w_gate'])


def _layer0(x, prev, p, prm, *, start):
    b_len, t_len, _ = x.shape
    sched = _Schedule(b_len, t_len, POOL_LAYER_ROWS)
    tt, bb = sched.tt, sched.bb
    tm = tt * bb
    state_spec = pl.BlockSpec((bb, POOL_BUF, D_MODEL), lambda j, i: (j, 0, 0))
    mid_rows = tm if sched.lag else 2 * V7X_SUBLANES
    return pl.pallas_call(
        functools.partial(_layer0_kernel, sched=sched, start=start),
        grid=sched.grid,
        in_specs=[
            pl.BlockSpec((bb, tt, D_MODEL), lambda j, i: (j, sched.mixer_block(i), 0)),
            state_spec,
            pl.BlockSpec((None, bb, tt, p.shape[-1]), lambda j, i: (0, j, sched.ffn_block(i), 0)),
            _layer_spec((1, D_MODEL), 0),
            _const_spec(prm['pool_w'].shape),
            _const_spec((1, D_MODEL)),
        ] + _ffn_weight_specs(prm, 0),
        out_specs=[pl.BlockSpec((tt, bb, D_MODEL), lambda j, i: (sched.ffn_block(i), j, 0)), state_spec],
        out_shape=[
            jax.ShapeDtypeStruct((t_len, b_len, D_MODEL), F32),
            jax.ShapeDtypeStruct((b_len, POOL_BUF, D_MODEL), F32),
        ],
        scratch_shapes=([pltpu.VMEM(((POOL_BUF + tt) * bb, D_MODEL), F32),
                         pltpu.VMEM((mid_rows, D_MODEL), F32)]
                        + _ffn_scratch(tm, prm['ffn_w_gate'].shape[-1], p.shape[-1])),
        compiler_params=_params(),
        name="layer0_pool_ffn",
    )(x, prev, p, prm['g_mix'], prm['pool_w'], prm['pool_scale'], *_ffn_weights(prm))


def _ssm_kernel(x_ref, h0_re_ref, h0_im_ref, g_ref, a2_re_ref, a2_im_ref, in_ref, out_ref, direct_ref,
                d_ref, wglu_ref, o_ref, s_re_ref, s_im_ref,
                st_re, st_im, u_ref, xj_ref, sbuf_ref, z_ref, *, pairs, bb):
    m = pairs * bb
    i = pl.program_id(1)

    @pl.when(i == 0)
    def _():
        st_re[...] = h0_re_ref[...]
        st_im[...] = h0_im_ref[...]

    for t in range(2):
        u_ref[t] = _rmsnorm(x_ref[:, t].reshape(m, D_MODEL), g_ref[...])

    def contributions(j):
        cols = slice(j * V7X_LANES, (j + 1) * V7X_LANES)
        both = jnp.concatenate([u_ref[0, :, cols], u_ref[1, :, cols]], axis=-1)
        xj_ref[j] = both.astype(BF16)
        sbuf_ref[:, j * SSM_PAIR_STATES:(j + 1) * SSM_PAIR_STATES] = _dot(xj_ref[j], in_ref[j])

    def scan(j):
        states = slice(j * SSM_LANE_STATES, (j + 1) * SSM_LANE_STATES)
        re = slice(j * SSM_PAIR_STATES, j * SSM_PAIR_STATES + SSM_LANE_STATES)
        im = slice(j * SSM_PAIR_STATES + SSM_LANE_STATES, (j + 1) * SSM_PAIR_STATES)
        ar = jnp.broadcast_to(a2_re_ref[:, states], (V7X_SUBLANES, SSM_LANE_STATES))
        ai = jnp.broadcast_to(a2_im_ref[:, states], (V7X_SUBLANES, SSM_LANE_STATES))
        for r0 in range(0, bb, V7X_SUBLANES):
            sr = st_re[r0:r0 + V7X_SUBLANES, states]
            si = st_im[r0:r0 + V7X_SUBLANES, states]
            for c in range(pairs):
                rows = slice(c * bb + r0, c * bb + r0 + V7X_SUBLANES)
                pr, pi = sbuf_ref[rows, re], sbuf_ref[rows, im]
                sbuf_ref[rows, re] = sr
                sbuf_ref[rows, im] = si
                sr, si = ar * sr - ai * si + pr, ar * si + ai * sr + pi
            st_re[r0:r0 + V7X_SUBLANES, states] = sr
            st_im[r0:r0 + V7X_SUBLANES, states] = si

    def outputs(j):
        cols = slice(j * V7X_LANES, (j + 1) * V7X_LANES)
        entering = sbuf_ref[:, j * SSM_PAIR_STATES:(j + 1) * SSM_PAIR_STATES].astype(BF16)
        y = _dot(entering, out_ref[j]) + _dot(xj_ref[j], direct_ref[j])
        for t in range(2):
            yt = y[:, t * V7X_LANES:(t + 1) * V7X_LANES] + d_ref[:, cols] * u_ref[t, :, cols]
            z_ref[t, :, cols] = jax.nn.gelu(yt).astype(BF16)

    for k in range(SSM_N_LANE_BLOCKS + 2):
        if 0 <= k - 2:
            outputs(k - 2)
        if k < SSM_N_LANE_BLOCKS:
            contributions(k)
        if 0 <= k - 1 < SSM_N_LANE_BLOCKS:
            scan(k - 1)

    for t in range(2):
        for n0 in range(0, D_MODEL, OUT_COLS):
            cols = slice(n0, n0 + OUT_COLS)
            a = _dot(z_ref[t], wglu_ref[:, cols])
            gate = _dot(z_ref[t], wglu_ref[:, D_MODEL + n0:D_MODEL + n0 + OUT_COLS])
            o_ref[:, t, :, cols] = x_ref[:, t, :, cols] + (a * jax.nn.sigmoid(gate)).reshape(pairs, bb, OUT_COLS)

    s_re_ref[...] = st_re[...]
    s_im_ref[...] = st_im[...]


def _ssm_layer(x, h0_re, h0_im, prm):
    t_len, b_len, _ = x.shape
    tt, bb = _tiles(b_len, t_len, SSM_ROWS)
    assert tt % 2 == 0
    pairs = tt // 2
    m = pairs * bb
    x_spec = pl.BlockSpec((pairs, 2, bb, D_MODEL), lambda j, i: (i, 0, j, 0))
    st_spec = pl.BlockSpec((bb, SSM_FLAT), lambda j, i: (j, 0))
    out, s_re, s_im = pl.pallas_call(
        functools.partial(_ssm_kernel, pairs=pairs, bb=bb),
        grid=(b_len // bb, t_len // tt),
        in_specs=[
            x_spec, st_spec, st_spec,
            _layer_spec((1, D_MODEL), 1),
            _const_spec((1, SSM_FLAT)), _const_spec((1, SSM_FLAT)),
            _const_spec(prm['ssm_in'].shape), _const_spec(prm['ssm_out'].shape),
            _const_spec(prm['ssm_direct'].shape),
            _const_spec((1, D_MODEL)),
            _const_spec(prm['ssm_w_glu'].shape),
        ],
        out_specs=[x_spec, st_spec, st_spec],
        out_shape=[
            jax.ShapeDtypeStruct((t_len // 2, 2, b_len, D_MODEL), F32),
            jax.ShapeDtypeStruct((b_len, SSM_FLAT), F32),
            jax.ShapeDtypeStruct((b_len, SSM_FLAT), F32),
        ],
        scratch_shapes=[
            pltpu.VMEM((bb, SSM_FLAT), F32), pltpu.VMEM((bb, SSM_FLAT), F32),
            pltpu.VMEM((2, m, D_MODEL), F32),
            pltpu.VMEM((SSM_N_LANE_BLOCKS, m, SSM_PAIR_COLS), BF16),
            pltpu.VMEM((m, SSM_N_LANE_BLOCKS * SSM_PAIR_STATES), F32),
            pltpu.VMEM((2, m, D_MODEL), BF16),
        ],
        compiler_params=_params(),
        name="ssm_mixer",
    )(x.reshape(t_len // 2, 2, b_len, D_MODEL), h0_re, h0_im, prm['g_mix'], prm['a2_re'], prm['a2_im'],
      prm['ssm_in'], prm['ssm_out'], prm['ssm_direct'], prm['ssm_d'], prm['ssm_w_glu'])
    return out.reshape(t_len, b_len, D_MODEL), s_re, s_im


def _ffn1_kernel(x_ref, p_ref, g_ffn_ref, w_gate_ref, w_up_ref, w_down_ref, g_ple_ref, ple_in_ref, ple_gate_ref,
                 g_final_ref, o_ref, h_in_ref, rows_ref, h_ref, pt_ref, act_ref, *, tt, bb):
    rows_ref[...] = x_ref[...].reshape(tt * bb, D_MODEL)
    h_in_ref[...] = _rmsnorm(rows_ref[...], g_ffn_ref[...]).astype(BF16)

    def emit():
        o_ref[...] = _batch_major(_rmsnorm(rows_ref[...], g_final_ref[...]), tt, bb)

    _interleave(_ffn_ple_pieces(rows_ref, h_in_ref, p_ref, w_gate_ref, w_up_ref, w_down_ref,
                                g_ple_ref[...], ple_in_ref, ple_gate_ref, h_ref, pt_ref, act_ref,
                                lambda: None, emit))


def _ffn1_layer(x, p, prm):
    t_len, b_len, _ = x.shape
    tt, bb = _tiles(b_len, t_len, FFN_ROWS)
    return pl.pallas_call(
        functools.partial(_ffn1_kernel, tt=tt, bb=bb),
        grid=(b_len // bb, t_len // tt),
        in_specs=[
            pl.BlockSpec((tt, bb, D_MODEL), lambda j, i: (i, j, 0)),
            pl.BlockSpec((None, bb, tt, p.shape[-1]), lambda j, i: (1, j, i, 0)),
        ] + _ffn_weight_specs(prm, 1) + [_const_spec((1, D_MODEL))],
        out_specs=pl.BlockSpec((bb, tt, D_MODEL), lambda j, i: (j, i, 0)),
        out_shape=jax.ShapeDtypeStruct((b_len, t_len, D_MODEL), F32),
        scratch_shapes=_ffn_scratch(tt * bb, prm['ffn_w_gate'].shape[-1], p.shape[-1]),
        compiler_params=_params(),
        name="layer1_ffn",
    )(x, p, *_ffn_weights(prm), prm['g_final'])


def _run_trunk(x, p, start, pool_state, ssm_re, ssm_im, prm):
    b_len = x.shape[0]
    xt, pool_new = _layer0(x, pool_state, p, prm, start=start)
    xt, s_re, s_im = _ssm_layer(xt, ssm_re.reshape(b_len, SSM_FLAT), ssm_im.reshape(b_len, SSM_FLAT), prm)
    y = _ffn1_layer(xt, p, prm)
    s_re = s_re.reshape(1, b_len, SSM_GROUPS, SSM_STATE)
    s_im = s_im.reshape(1, b_len, SSM_GROUPS, SSM_STATE)
    return y, pool_new[None], s_re, s_im


def kernel(x_prompt, x_sample, state_pool, state_ssm_re, state_ssm_im, p_prompt, p_sample, g_mix, g_ffn, g_ple, g_final, pool_w, pool_scale, ssm_lambda_re, ssm_lambda_im, ssm_log_dt, ssm_b_re, ssm_b_im, ssm_c_re, ssm_c_im, ssm_d, ssm_w_glu, ffn_w_gate, ffn_w_up, ffn_w_down, ple_w_in, ple_w_gate):
    depth = g_mix.shape[0]
    a2_re, a2_im, ssm_in, ssm_out, ssm_direct = _ssm_prep(
        ssm_lambda_re[0], ssm_lambda_im[0], ssm_log_dt[0], ssm_b_re[0], ssm_b_im[0], ssm_c_re[0], ssm_c_im[0])
    per_layer_vec = lambda a: a.reshape(depth, 1, D_MODEL)
    prm = dict(
        g_mix=per_layer_vec(g_mix), g_ffn=per_layer_vec(g_ffn), g_ple=per_layer_vec(g_ple),
        g_final=g_final.reshape(1, D_MODEL),
        pool_w=pool_w[0].astype(BF16), pool_scale=pool_scale,
        a2_re=a2_re, a2_im=a2_im,
        ssm_in=ssm_in, ssm_out=ssm_out, ssm_direct=ssm_direct,
        ssm_d=ssm_d, ssm_w_glu=ssm_w_glu[0].astype(BF16),
        ffn_w_gate=ffn_w_gate.astype(BF16), ffn_w_up=ffn_w_up.astype(BF16),
        ffn_w_down=ffn_w_down.astype(BF16),
        ple_w_in=ple_w_in.astype(BF16), ple_w_gate=ple_w_gate.astype(BF16),
    )
    b_p = x_prompt.shape[0]
    zeros_pool = jnp.zeros((b_p, POOL_BUF, D_MODEL), F32)
    zeros_ssm = jnp.zeros((b_p, SSM_GROUPS, SSM_STATE), F32)
    y_p, pool_p, re_p, im_p = _run_trunk(x_prompt, p_prompt, 0, zeros_pool, zeros_ssm, zeros_ssm, prm)
    y_s, pool_s, re_s, im_s = _run_trunk(x_sample, p_sample, PAST_LEN, state_pool[0],
                                         state_ssm_re[0], state_ssm_im[0], prm)
    return (y_p, y_s, pool_p, pool_s, re_p, im_p, re_s, im_s)
```

```python
import functools
import math

import jax
import jax.numpy as jnp
from jax import lax
from jax.experimental import pallas as pl
from jax.experimental.pallas import tpu as pltpu

D_MODEL = 1024
POOL_WINDOWS = (2, 4, 8, 16)
POOL_GROUP_DIM = D_MODEL // len(POOL_WINDOWS)
POOL_BUF = max(POOL_WINDOWS) - 1
SSM_GROUP_DIM = 16
SSM_GROUPS = D_MODEL // SSM_GROUP_DIM
SSM_STATE = 64
SSM_FLAT = SSM_GROUPS * SSM_STATE
PAST_LEN = 16384
EPS = 1e-6

V7X_SUBLANES = 8
V7X_LANES = 128
V7X_MXU_DIM = 256
V7X_VMEM_LIMIT_BYTES = 62 * 1024 * 1024

SSM_COL_BLOCK = V7X_MXU_DIM
SSM_GROUPS_PER_BLOCK = SSM_COL_BLOCK // SSM_GROUP_DIM
SSM_STATE_BLOCK = SSM_GROUPS_PER_BLOCK * SSM_STATE
SSM_N_BLOCKS = D_MODEL // SSM_COL_BLOCK
SSM_LANE_GROUPS = V7X_LANES // SSM_GROUP_DIM
SSM_LANE_STATES = SSM_LANE_GROUPS * SSM_STATE
SSM_N_LANE_BLOCKS = D_MODEL // V7X_LANES
SSM_PAIR_COLS = 2 * V7X_LANES
SSM_PAIR_STATES = 2 * SSM_LANE_STATES

POOL_LAYER_ROWS = 1024
SSM_ROWS = 1024
FFN_ROWS = 1024
FFN_COLS = 2 * V7X_MXU_DIM
OUT_COLS = 2 * V7X_MXU_DIM

BF16 = jnp.bfloat16
F32 = jnp.float32


def _rmsnorm(x, g):
    return x * lax.rsqrt(jnp.mean(x * x, axis=-1, keepdims=True) + EPS) * g


def _dot(a, b):
    return jnp.dot(a, b, preferred_element_type=F32)


def _dot_nt_f32(a, b):
    return lax.dot_general(a, b, (((1,), (1,)), ((), ())), precision=lax.Precision.HIGHEST,
                           preferred_element_type=F32)


def _time_major(block):
    bb, tt, c = block.shape
    return jnp.transpose(block, (1, 0, 2)).reshape(tt * bb, c)


def _batch_major(rows, tt, bb):
    return jnp.transpose(rows.reshape(tt, bb, rows.shape[-1]), (1, 0, 2))


def _const_spec(shape):
    zeros = (0,) * len(shape)
    return pl.BlockSpec(shape, lambda *_: zeros, pipeline_mode=pl.Buffered(1))


def _layer_spec(shape, layer):
    tail = (0,) * len(shape)
    return pl.BlockSpec((None,) + tuple(shape), lambda *_: (layer,) + tail,
                        pipeline_mode=pl.Buffered(1))


def _params():
    return pltpu.CompilerParams(dimension_semantics=("arbitrary", "arbitrary"),
                                vmem_limit_bytes=V7X_VMEM_LIMIT_BYTES)


def _tiles(b_len, t_len, rows):
    bb = min(b_len, max(V7X_SUBLANES, rows // t_len))
    tt = min(t_len, rows // bb)
    assert t_len % tt == 0 and b_len % bb == 0 and bb % V7X_SUBLANES == 0
    return tt, bb


class _Schedule:
    def __init__(self, b_len, t_len, rows):
        self.tt, self.bb = _tiles(b_len, t_len, rows)
        self.nt = t_len // self.tt
        self.lag = 1 if self.nt > 1 else 0
        self.grid = (b_len // self.bb, self.nt + self.lag)

    def mixer_block(self, i):
        return jnp.minimum(i, self.nt - 1)

    def ffn_block(self, i):
        return jnp.maximum(i - self.lag, 0)


def _ssm_prep_kernel(lam_re_ref, lam_im_ref, log_dt_ref, b_re_ref, b_im_ref, c_re_ref, c_im_ref,
                     a2_re_ref, a2_im_ref, in_ref, out_ref, direct_ref,
                     lb_re_row, lb_im_row, wb_re, wb_im, wc_re, wc_im):
    lam_re = lam_re_ref[...]
    lam_im = lam_im_ref[...]
    dt = jnp.exp(log_dt_ref[...])
    mag = jnp.exp(lam_re * dt)
    lb_re = mag * jnp.cos(lam_im * dt)
    lb_im = mag * jnp.sin(lam_im * dt)
    den = lam_re * lam_re + lam_im * lam_im
    f_re = ((lb_re - 1.0) * lam_re + lb_im * lam_im) / den
    f_im = (lb_im * lam_re - (lb_re - 1.0) * lam_im) / den
    b_re = jnp.swapaxes(b_re_ref[...], 1, 2)
    b_im = jnp.swapaxes(b_im_ref[...], 1, 2)
    fr = f_re[:, None, :]
    fi = f_im[:, None, :]
    bb_re = fr * b_re - fi * b_im
    bb_im = fr * b_im + fi * b_re
    c_re = c_re_ref[...]
    c_im_neg = -c_im_ref[...]

    for ref in (wb_re, wb_im, wc_re, wc_im):
        ref[...] = jnp.zeros(ref.shape, ref.dtype)
    for g in range(SSM_GROUPS):
        cb, gl = divmod(g, SSM_GROUPS_PER_BLOCK)
        cols = slice(gl * SSM_GROUP_DIM, (gl + 1) * SSM_GROUP_DIM)
        states = slice(gl * SSM_STATE, (gl + 1) * SSM_STATE)
        wb_re[cb, cols, states] = bb_re[g]
        wb_im[cb, cols, states] = bb_im[g]
        wc_re[cb, cols, states] = c_re[g]
        wc_im[cb, cols, states] = c_im_neg[g]
        flat = slice(g * SSM_STATE, (g + 1) * SSM_STATE)
        lb_re_row[:, flat] = lb_re[g:g + 1, :]
        lb_im_row[:, flat] = lb_im[g:g + 1, :]

    lr_all = lb_re_row[...]
    li_all = lb_im_row[...]
    a2_re_ref[...] = lr_all * lr_all - li_all * li_all
    a2_im_ref[...] = 2.0 * lr_all * li_all

    halves = SSM_COL_BLOCK // V7X_LANES
    for cb in range(SSM_N_BLOCKS):
        flat = slice(cb * SSM_STATE_BLOCK, (cb + 1) * SSM_STATE_BLOCK)
        lr, li = lb_re_row[:, flat], lb_im_row[:, flat]
        l2r, l2i = a2_re_ref[:, flat], a2_im_ref[:, flat]
        w_r, w_i = wb_re[cb], wb_im[cb]
        wl_r, wl_i = w_r * lr - w_i * li, w_r * li + w_i * lr
        c_r, c_n = wc_re[cb], wc_im[cb]
        o1_r, o1_i = c_r * lr + c_n * li, c_n * lr - c_r * li
        o2_r, o2_i = c_r * l2r + c_n * l2i, c_n * l2r - c_r * l2i
        for half in range(halves):
            j = cb * halves + half
            r = slice(half * V7X_LANES, (half + 1) * V7X_LANES)
            s = slice(half * SSM_LANE_STATES, (half + 1) * SSM_LANE_STATES)
            t0, t1 = slice(0, V7X_LANES), slice(V7X_LANES, SSM_PAIR_COLS)
            re, im = slice(0, SSM_LANE_STATES), slice(SSM_LANE_STATES, SSM_PAIR_STATES)
            in_ref[j, t0, re] = wl_r[r, s].astype(BF16)
            in_ref[j, t0, im] = wl_i[r, s].astype(BF16)
            in_ref[j, t1, re] = w_r[r, s].astype(BF16)
            in_ref[j, t1, im] = w_i[r, s].astype(BF16)
            k0 = _dot_nt_f32(w_r[r, s], c_r[r, s]) + _dot_nt_f32(w_i[r, s], c_n[r, s])
            k1 = _dot_nt_f32(wl_r[r, s], c_r[r, s]) + _dot_nt_f32(wl_i[r, s], c_n[r, s])
            direct_ref[j, t0, t0] = k0.astype(BF16)
            direct_ref[j, t0, t1] = k1.astype(BF16)
            direct_ref[j, t1, t0] = jnp.zeros((V7X_LANES, V7X_LANES), BF16)
            direct_ref[j, t1, t1] = k0.astype(BF16)
            out_ref[j, re, t0] = o1_r[r, s].T.astype(BF16)
            out_ref[j, im, t0] = o1_i[r, s].T.astype(BF16)
            out_ref[j, re, t1] = o2_r[r, s].T.astype(BF16)
            out_ref[j, im, t1] = o2_i[r, s].T.astype(BF16)


def _ssm_prep(lam_re, lam_im, log_dt, b_re, b_im, c_re, c_im):
    g, p, _ = b_re.shape
    row = jax.ShapeDtypeStruct((1, SSM_FLAT), F32)
    n = SSM_N_LANE_BLOCKS
    block_diag = pltpu.VMEM((SSM_N_BLOCKS, SSM_COL_BLOCK, SSM_STATE_BLOCK), F32)
    return pl.pallas_call(
        _ssm_prep_kernel,
        out_shape=(row, row,
                   jax.ShapeDtypeStruct((n, SSM_PAIR_COLS, SSM_PAIR_STATES), BF16),
                   jax.ShapeDtypeStruct((n, SSM_PAIR_STATES, SSM_PAIR_COLS), BF16),
                   jax.ShapeDtypeStruct((n, SSM_PAIR_COLS, SSM_PAIR_COLS), BF16)),
        scratch_shapes=[pltpu.VMEM((1, SSM_FLAT), F32), pltpu.VMEM((1, SSM_FLAT), F32),
                        block_diag, block_diag, block_diag, block_diag],
        compiler_params=pltpu.CompilerParams(vmem_limit_bytes=V7X_VMEM_LIMIT_BYTES),
        name="ssm_prep",
    )(lam_re, lam_im, log_dt.reshape(g, 1), b_re, b_im, c_re, c_im)


def _pool_pieces(x_ref, dst_ref, ext_ref, w_ref, scale_ref, g, *, tt, bb, first_pos):
    tm = tt * bb
    halo = POOL_BUF * bb
    dst_ref[...] = _time_major(x_ref[...])
    ext_ref[halo:halo + tm, :] = _rmsnorm(dst_ref[...], g)
    yield
    row = lax.broadcasted_iota(jnp.int32, (tm, 1), 0)
    pos = first_pos + lax.shift_right_logical(row, int(math.log2(bb)))
    for gi, w in enumerate(POOL_WINDOWS):
        cols = slice(gi * POOL_GROUP_DIM, (gi + 1) * POOL_GROUP_DIM)
        acc = ext_ref[halo:halo + tm, cols]
        for k in range(1, w):
            acc = acc + ext_ref[halo - k * bb:halo - k * bb + tm, cols]
        inv_cnt = 1.0 / jnp.minimum(w, pos + 1).astype(F32)
        diff = acc * inv_cnt - ext_ref[halo:halo + tm, cols]
        dst_ref[:, cols] += _dot(diff.astype(BF16), w_ref[gi]) * scale_ref[:, cols]
        yield


def _ffn_ple_pieces(rows_ref, h_in_ref, p_ref, w_gate_ref, w_up_ref, w_down_ref, g_ple, ple_in_ref, ple_gate_ref,
                    h_ref, pt_ref, act_ref, release_h_in, emit):
    d_ff = act_ref.shape[1]
    pt_ref[...] = _time_major(p_ref[...]).astype(BF16)
    for c0 in range(0, d_ff, FFN_COLS):
        cols = slice(c0, min(c0 + FFN_COLS, d_ff))
        gate = _dot(h_in_ref[...], w_gate_ref[:, cols])
        up = _dot(h_in_ref[...], w_up_ref[:, cols])
        act_ref[:, cols] = (jax.nn.silu(gate) * up).astype(BF16)
        if cols.stop == d_ff:
            release_h_in()
        yield
    for n0 in range(0, D_MODEL, OUT_COLS):
        cols = slice(n0, n0 + OUT_COLS)
        rows_ref[:, cols] += _dot(act_ref[...], w_down_ref[:, cols])
        yield
    h_ref[...] = _rmsnorm(rows_ref[...], g_ple).astype(BF16)
    for n0 in range(0, D_MODEL, OUT_COLS):
        cols = slice(n0, n0 + OUT_COLS)
        ple_gate = jax.nn.sigmoid(_dot(h_ref[...], ple_gate_ref[:, cols]))
        rows_ref[:, cols] += _dot(pt_ref[...], ple_in_ref[:, cols]) * ple_gate
        yield
    emit()
    yield


_DONE = object()


def _interleave(*stages):
    live = list(stages)
    while live:
        for stage in list(live):
            if next(stage, _DONE) is _DONE:
                live.remove(stage)


def _then(stage, last_piece):
    yield from stage
    last_piece()
    yield


def _run_halves(sched, g_ffn, mid_ref, hmid_ref, rows_ref, mixer, ffn):
    h_in_reads_issued = []

    def mixer_into(dst_ref):
        def norm_for_ffn():
            assert h_in_reads_issued or not sched.lag, "hmid_ref rewritten before the FFN half read it"
            hmid_ref[...] = _rmsnorm(dst_ref[...], g_ffn).astype(BF16)
        return _then(mixer(dst_ref), norm_for_ffn)

    def release_h_in():
        h_in_reads_issued.append(True)

    if sched.lag:
        rows_ref[...] = mid_ref[...]
        _interleave(ffn(hmid_ref, release_h_in), mixer_into(mid_ref))
    else:
        _interleave(mixer_into(rows_ref))
        _interleave(ffn(hmid_ref, release_h_in))


def _init_handoff(sched, mid_ref, hmid_ref):
    if sched.lag:
        mid_ref[...] = jnp.zeros(mid_ref.shape, mid_ref.dtype)
        hmid_ref[...] = jnp.zeros(hmid_ref.shape, hmid_ref.dtype)


def _ffn_scratch(tm, d_ff, ple_dim):
    return [
        pltpu.VMEM((tm, D_MODEL), BF16),
        pltpu.VMEM((tm, D_MODEL), BF16),
        pltpu.VMEM((tm, ple_dim), BF16),
        pltpu.VMEM((tm, d_ff), BF16),
    ]


def _layer0_kernel(x_ref, prev_ref, p_ref, g_mix_ref, pool_w_ref, pool_scale_ref,
                   g_ffn_ref, w_gate_ref, w_up_ref, w_down_ref, g_ple_ref, ple_in_ref, ple_gate_ref,
                   o_ref, state_ref, ext_ref, mid_ref, hmid_ref, h_ref, pt_ref, act_ref,
                   *, sched, start):
    tt, bb = sched.tt, sched.bb
    tm = tt * bb
    halo = POOL_BUF * bb
    i = pl.program_id(1)
    rows_ref = o_ref.reshape(tm, D_MODEL)

    @pl.when(i == 0)
    def _():
        ext_ref[0:halo, :] = _time_major(prev_ref[...])
        _init_handoff(sched, mid_ref, hmid_ref)

    @pl.when(i > 0)
    def _():
        ext_ref[0:halo, :] = ext_ref[tm:tm + halo, :]

    def mixer(dst_ref):
        return _pool_pieces(x_ref, dst_ref, ext_ref, pool_w_ref, pool_scale_ref, g_mix_ref[...],
                            tt=tt, bb=bb, first_pos=start + sched.mixer_block(i) * tt)

    def ffn(h_in_ref, release_h_in):
        return _ffn_ple_pieces(rows_ref, h_in_ref, p_ref, w_gate_ref, w_up_ref, w_down_ref,
                               g_ple_ref[...], ple_in_ref, ple_gate_ref, h_ref, pt_ref, act_ref,
                               release_h_in, lambda: None)

    _run_halves(sched, g_ffn_ref[...], mid_ref, hmid_ref, rows_ref, mixer, ffn)

    @pl.when(i == sched.nt - 1)
    def _():
        state_ref[...] = _batch_major(ext_ref[tm:tm + halo, :], POOL_BUF, bb)


def _ffn_weight_specs(prm, layer):
    return [
        _layer_spec((1, D_MODEL), layer),
        _layer_spec(prm['ffn_w_gate'].shape[1:], layer), _layer_spec(prm['ffn_w_up'].shape[1:], layer),
        _layer_spec(prm['ffn_w_down'].shape[1:], layer),
        _layer_spec((1, D_MODEL), layer),
        _layer_spec(prm['ple_w_in'].shape[1:], layer), _layer_spec(prm['ple_w_gate'].shape[1:], layer),
    ]


def _ffn_weights(prm):
    return (prm['g_ffn'], prm['ffn_w_gate'], prm['ffn_w_up'], prm['ffn_w_down'],
            prm['g_ple'], prm['ple_w_in'], prm['ple_w_gate'])


def _layer0(x, prev, p, prm, *, start):
    b_len, t_len, _ = x.shape
    sched = _Schedule(b_len, t_len, POOL_LAYER_ROWS)
    tt, bb = sched.tt, sched.bb
    tm = tt * bb
    state_spec = pl.BlockSpec((bb, POOL_BUF, D_MODEL), lambda j, i: (j, 0, 0))
    mid_rows = tm if sched.lag else 2 * V7X_SUBLANES
    return pl.pallas_call(
        functools.partial(_layer0_kernel, sched=sched, start=start),
        grid=sched.grid,
        in_specs=[
            pl.BlockSpec((bb, tt, D_MODEL), lambda j, i: (j, sched.mixer_block(i), 0)),
            state_spec,
            pl.BlockSpec((None, bb, tt, p.shape[-1]), lambda j, i: (0, j, sched.ffn_block(i), 0)),
            _layer_spec((1, D_MODEL), 0),
            _const_spec(prm['pool_w'].shape),
            _const_spec((1, D_MODEL)),
        ] + _ffn_weight_specs(prm, 0),
        out_specs=[pl.BlockSpec((tt, bb, D_MODEL), lambda j, i: (sched.ffn_block(i), j, 0)), state_spec],
        out_shape=[
            jax.ShapeDtypeStruct((t_len, b_len, D_MODEL), F32),
            jax.ShapeDtypeStruct((b_len, POOL_BUF, D_MODEL), F32),
        ],
        scratch_shapes=([pltpu.VMEM(((POOL_BUF + tt) * bb, D_MODEL), F32),
                         pltpu.VMEM((mid_rows, D_MODEL), F32)]
                        + _ffn_scratch(tm, prm['ffn_w_gate'].shape[-1], p.shape[-1])),
        compiler_params=_params(),
        name="layer0_pool_ffn",
    )(x, prev, p, prm['g_mix'], prm['pool_w'], prm['pool_scale'], *_ffn_weights(prm))


def _ssm_kernel(x_ref, h0_re_ref, h0_im_ref, g_ref, a2_re_ref, a2_im_ref, in_ref, out_ref, direct_ref,
                d_ref, wglu_ref, o_ref, s_re_ref, s_im_ref,
                st_re, st_im, u_ref, xj_ref, sbuf_ref, z_ref, *, pairs, bb):
    m = pairs * bb
    i = pl.program_id(1)

    @pl.when(i == 0)
    def _():
        st_re[...] = h0_re_ref[...]
        st_im[...] = h0_im_ref[...]

    for t in range(2):
        u_ref[t] = _rmsnorm(x_ref[:, t].reshape(m, D_MODEL), g_ref[...])

    def contributions(j):
        cols = slice(j * V7X_LANES, (j + 1) * V7X_LANES)
        both = jnp.concatenate([u_ref[0, :, cols], u_ref[1, :, cols]], axis=-1)
        xj_ref[j] = both.astype(BF16)
        sbuf_ref[:, j * SSM_PAIR_STATES:(j + 1) * SSM_PAIR_STATES] = _dot(xj_ref[j], in_ref[j])

    def scan(j):
        states = slice(j * SSM_LANE_STATES, (j + 1) * SSM_LANE_STATES)
        re = slice(j * SSM_PAIR_STATES, j * SSM_PAIR_STATES + SSM_LANE_STATES)
        im = slice(j * SSM_PAIR_STATES + SSM_LANE_STATES, (j + 1) * SSM_PAIR_STATES)
        ar = jnp.broadcast_to(a2_re_ref[:, states], (V7X_SUBLANES, SSM_LANE_STATES))
        ai = jnp.broadcast_to(a2_im_ref[:, states], (V7X_SUBLANES, SSM_LANE_STATES))
        for r0 in range(0, bb, V7X_SUBLANES):
            sr = st_re[r0:r0 + V7X_SUBLANES, states]
            si = st_im[r0:r0 + V7X_SUBLANES, states]
            for c in range(pairs):
                rows = slice(c * bb + r0, c * bb + r0 + V7X_SUBLANES)
                pr, pi = sbuf_ref[rows, re], sbuf_ref[rows, im]
                sbuf_ref[rows, re] = sr
                sbuf_ref[rows, im] = si
                sr, si = ar * sr - ai * si + pr, ar * si + ai * sr + pi
            st_re[r0:r0 + V7X_SUBLANES, states] = sr
            st_im[r0:r0 + V7X_SUBLANES, states] = si

    def outputs(j):
        cols = slice(j * V7X_LANES, (j + 1) * V7X_LANES)
        entering = sbuf_ref[:, j * SSM_PAIR_STATES:(j + 1) * SSM_PAIR_STATES].astype(BF16)
        y = _dot(entering, out_ref[j]) + _dot(xj_ref[j], direct_ref[j])
        for t in range(2):
            yt = y[:, t * V7X_LANES:(t + 1) * V7X_LANES] + d_ref[:, cols] * u_ref[t, :, cols]
            z_ref[t, :, cols] = jax.nn.gelu(yt).astype(BF16)

    for k in range(SSM_N_LANE_BLOCKS + 2):
        if 0 <= k - 2:
            outputs(k - 2)
        if k < SSM_N_LANE_BLOCKS:
            contributions(k)
        if 0 <= k - 1 < SSM_N_LANE_BLOCKS:
            scan(k - 1)

    for t in range(2):
        for n0 in range(0, D_MODEL, OUT_COLS):
            cols = slice(n0, n0 + OUT_COLS)
            a = _dot(z_ref[t], wglu_ref[:, cols])
            gate = _dot(z_ref[t], wglu_ref[:, D_MODEL + n0:D_MODEL + n0 + OUT_COLS])
            o_ref[:, t, :, cols] = x_ref[:, t, :, cols] + (a * jax.nn.sigmoid(gate)).reshape(pairs, bb, OUT_COLS)

    s_re_ref[...] = st_re[...]
    s_im_ref[...] = st_im[...]


def _ssm_layer(x, h0_re, h0_im, prm):
    t_len, b_len, _ = x.shape
    tt, bb = _tiles(b_len, t_len, SSM_ROWS)
    assert tt % 2 == 0
    pairs = tt // 2
    m = pairs * bb
    x_spec = pl.BlockSpec((pairs, 2, bb, D_MODEL), lambda j, i: (i, 0, j, 0))
    st_spec = pl.BlockSpec((bb, SSM_FLAT), lambda j, i: (j, 0))
    out, s_re, s_im = pl.pallas_call(
        functools.partial(_ssm_kernel, pairs=pairs, bb=bb),
        grid=(b_len // bb, t_len // tt),
        in_specs=[
            x_spec, st_spec, st_spec,
            _layer_spec((1, D_MODEL), 1),
            _const_spec((1, SSM_FLAT)), _const_spec((1, SSM_FLAT)),
            _const_spec(prm['ssm_in'].shape), _const_spec(prm['ssm_out'].shape),
            _const_spec(prm['ssm_direct'].shape),
            _const_spec((1, D_MODEL)),
            _const_spec(prm['ssm_w_glu'].shape),
        ],
        out_specs=[x_spec, st_spec, st_spec],
        out_shape=[
            jax.ShapeDtypeStruct((t_len // 2, 2, b_len, D_MODEL), F32),
            jax.ShapeDtypeStruct((b_len, SSM_FLAT), F32),
            jax.ShapeDtypeStruct((b_len, SSM_FLAT), F32),
        ],
        scratch_shapes=[
            pltpu.VMEM((bb, SSM_FLAT), F32), pltpu.VMEM((bb, SSM_FLAT), F32),
            pltpu.VMEM((2, m, D_MODEL), F32),
            pltpu.VMEM((SSM_N_LANE_BLOCKS, m, SSM_PAIR_COLS), BF16),
            pltpu.VMEM((m, SSM_N_LANE_BLOCKS * SSM_PAIR_STATES), F32),
            pltpu.VMEM((2, m, D_MODEL), BF16),
        ],
        compiler_params=_params(),
        name="ssm_mixer",
    )(x.reshape(t_len // 2, 2, b_len, D_MODEL), h0_re, h0_im, prm['g_mix'], prm['a2_re'], prm['a2_im'],
      prm['ssm_in'], prm['ssm_out'], prm['ssm_direct'], prm['ssm_d'], prm['ssm_w_glu'])
    return out.reshape(t_len, b_len, D_MODEL), s_re, s_im


def _ffn1_kernel(x_ref, p_ref, g_ffn_ref, w_gate_ref, w_up_ref, w_down_ref, g_ple_ref, ple_in_ref, ple_gate_ref,
                 g_final_ref, o_ref, rows_ref, h_in_ref, h_ref, pt_ref, act_ref, *, tt, bb):
    rows_ref[...] = x_ref[...].reshape(tt * bb, D_MODEL)
    h_in_ref[...] = _rmsnorm(rows_ref[...], g_ffn_ref[...]).astype(BF16)

    def emit():
        o_ref[...] = _batch_major(_rmsnorm(rows_ref[...], g_final_ref[...]), tt, bb)

    _interleave(_ffn_ple_pieces(rows_ref, h_in_ref, p_ref, w_gate_ref, w_up_ref, w_down_ref,
                                g_ple_ref[...], ple_in_ref, ple_gate_ref, h_ref, pt_ref, act_ref,
                                lambda: None, emit))


def _ffn1_layer(x, p, prm):
    t_len, b_len, _ = x.shape
    tt, bb = _tiles(b_len, t_len, FFN_ROWS)
    return pl.pallas_call(
        functools.partial(_ffn1_kernel, tt=tt, bb=bb),
        grid=(b_len // bb, t_len // tt),
        in_specs=[
            pl.BlockSpec((tt, bb, D_MODEL), lambda j, i: (i, j, 0)),
            pl.BlockSpec((None, bb, tt, p.shape[-1]), lambda j, i: (1, j, i, 0)),
        ] + _ffn_weight_specs(prm, 1) + [_const_spec((1, D_MODEL))],
        out_specs=pl.BlockSpec((bb, tt, D_MODEL), lambda j, i: (j, i, 0)),
        out_shape=jax.ShapeDtypeStruct((b_len, t_len, D_MODEL), F32),
        scratch_shapes=([pltpu.VMEM((tt * bb, D_MODEL), F32)]
                        + _ffn_scratch(tt * bb, prm['ffn_w_gate'].shape[-1], p.shape[-1])),
        compiler_params=_params(),
        name="layer1_ffn",
    )(x, p, *_ffn_weights(prm), prm['g_final'])


def _run_trunk(x, p, start, pool_state, ssm_re, ssm_im, prm):
    b_len = x.shape[0]
    xt, pool_new = _layer0(x, pool_state, p, prm, start=start)
    xt, s_re, s_im = _ssm_layer(xt, ssm_re.reshape(b_len, SSM_FLAT), ssm_im.reshape(b_len, SSM_FLAT), prm)
    y = _ffn1_layer(xt, p, prm)
    s_re = s_re.reshape(1, b_len, SSM_GROUPS, SSM_STATE)
    s_im = s_im.reshape(1, b_len, SSM_GROUPS, SSM_STATE)
    return y, pool_new[None], s_re, s_im


def kernel(x_prompt, x_sample, state_pool, state_ssm_re, state_ssm_im, p_prompt, p_sample, g_mix, g_ffn, g_ple, g_final, pool_w, pool_scale, ssm_lambda_re, ssm_lambda_im, ssm_log_dt, ssm_b_re, ssm_b_im, ssm_c_re, ssm_c_im, ssm_d, ssm_w_glu, ffn_w_gate, ffn_w_up, ffn_w_down, ple_w_in, ple_w_gate):
    depth = g_mix.shape[0]
    a2_re, a2_im, ssm_in, ssm_out, ssm_direct = _ssm_prep(
        ssm_lambda_re[0], ssm_lambda_im[0], ssm_log_dt[0], ssm_b_re[0], ssm_b_im[0], ssm_c_re[0], ssm_c_im[0])
    per_layer_vec = lambda a: a.reshape(depth, 1, D_MODEL)
    prm = dict(
        g_mix=per_layer_vec(g_mix), g_ffn=per_layer_vec(g_ffn), g_ple=per_layer_vec(g_ple),
        g_final=g_final.reshape(1, D_MODEL),
        pool_w=pool_w[0].astype(BF16), pool_scale=pool_scale,
        a2_re=a2_re, a2_im=a2_im,
        ssm_in=ssm_in, ssm_out=ssm_out, ssm_direct=ssm_direct,
        ssm_d=ssm_d, ssm_w_glu=ssm_w_glu[0].astype(BF16),
        ffn_w_gate=ffn_w_gate.astype(BF16), ffn_w_up=ffn_w_up.astype(BF16),
        ffn_w_down=ffn_w_down.astype(BF16),
        ple_w_in=ple_w_in.astype(BF16), ple_w_gate=ple_w_gate.astype(BF16),
    )
    b_p = x_prompt.shape[0]
    zeros_pool = jnp.zeros((b_p, POOL_BUF, D_MODEL), F32)
    zeros_ssm = jnp.zeros((b_p, SSM_GROUPS, SSM_STATE), F32)
    y_p, pool_p, re_p, im_p = _run_trunk(x_prompt, p_prompt, 0, zeros_pool, zeros_ssm, zeros_ssm, prm)
    y_s, pool_s, re_s, im_s = _run_trunk(x_sample, p_sample, PAST_LEN, state_pool[0],
                                         state_ssm_re[0], state_ssm_im[0], prm)
    return (y_p, y_s, pool_p, pool_s, re_p, im_p, re_s, im_s)
```

```python
import functools
import math

import jax
import jax.numpy as jnp
from jax import lax
from jax.experimental import pallas as pl
from jax.experimental.pallas import tpu as pltpu

D_MODEL = 1024
POOL_WINDOWS = (2, 4, 8, 16)
POOL_GROUP_DIM = D_MODEL // len(POOL_WINDOWS)
POOL_BUF = max(POOL_WINDOWS) - 1
SSM_GROUP_DIM = 16
SSM_GROUPS = D_MODEL // SSM_GROUP_DIM
SSM_STATE = 64
SSM_FLAT = SSM_GROUPS * SSM_STATE
PAST_LEN = 16384
EPS = 1e-6

V7X_SUBLANES = 8
V7X_LANES = 128
V7X_MXU_DIM = 256
V7X_VMEM_LIMIT_BYTES = 60 * 1024 * 1024

SSM_COL_BLOCK = V7X_MXU_DIM
SSM_GROUPS_PER_BLOCK = SSM_COL_BLOCK // SSM_GROUP_DIM
SSM_STATE_BLOCK = SSM_GROUPS_PER_BLOCK * SSM_STATE
SSM_N_BLOCKS = D_MODEL // SSM_COL_BLOCK
SSM_LANE_GROUPS = V7X_LANES // SSM_GROUP_DIM
SSM_LANE_STATES = SSM_LANE_GROUPS * SSM_STATE
SSM_N_LANE_BLOCKS = D_MODEL // V7X_LANES
SSM_PAIR_COLS = 2 * V7X_LANES
SSM_PAIR_STATES = 2 * SSM_LANE_STATES

POOL_LAYER_ROWS = 512
SSM_ROWS = 1024
FFN_ROWS = 1024
FFN_COLS = 2 * V7X_MXU_DIM
OUT_COLS = 2 * V7X_MXU_DIM

BF16 = jnp.bfloat16
F32 = jnp.float32


def _rmsnorm(x, g):
    return x * lax.rsqrt(jnp.mean(x * x, axis=-1, keepdims=True) + EPS) * g


def _dot(a, b):
    return jnp.dot(a, b, preferred_element_type=F32)


def _dot_nt_f32(a, b):
    return lax.dot_general(a, b, (((1,), (1,)), ((), ())), precision=lax.Precision.HIGHEST,
                           preferred_element_type=F32)


def _time_major(block):
    bb, tt, c = block.shape
    return jnp.transpose(block, (1, 0, 2)).reshape(tt * bb, c)


def _batch_major(rows, tt, bb):
    return jnp.transpose(rows.reshape(tt, bb, rows.shape[-1]), (1, 0, 2))


def _const_spec(shape):
    zeros = (0,) * len(shape)
    return pl.BlockSpec(shape, lambda *_: zeros, pipeline_mode=pl.Buffered(1))


def _layer_spec(shape, layer):
    tail = (0,) * len(shape)
    return pl.BlockSpec((None,) + tuple(shape), lambda *_: (layer,) + tail,
                        pipeline_mode=pl.Buffered(1))


def _params():
    return pltpu.CompilerParams(dimension_semantics=("arbitrary", "arbitrary"),
                                vmem_limit_bytes=V7X_VMEM_LIMIT_BYTES)


def _tiles(b_len, t_len, rows):
    bb = min(b_len, max(V7X_SUBLANES, rows // t_len))
    tt = min(t_len, rows // bb)
    assert t_len % tt == 0 and b_len % bb == 0 and bb % V7X_SUBLANES == 0
    return tt, bb


class _Schedule:
    def __init__(self, b_len, t_len, rows):
        self.tt, self.bb = _tiles(b_len, t_len, rows)
        self.nt = t_len // self.tt
        self.lag = 1 if self.nt > 1 else 0
        self.grid = (b_len // self.bb, self.nt + self.lag)

    def mixer_block(self, i):
        return jnp.minimum(i, self.nt - 1)

    def ffn_block(self, i):
        return jnp.maximum(i - self.lag, 0)


def _ssm_prep_kernel(lam_re_ref, lam_im_ref, log_dt_ref, b_re_ref, b_im_ref, c_re_ref, c_im_ref,
                     a2_re_ref, a2_im_ref, in_ref, out_ref, direct_ref,
                     lb_re_row, lb_im_row, wb_re, wb_im, wc_re, wc_im):
    lam_re = lam_re_ref[...]
    lam_im = lam_im_ref[...]
    dt = jnp.exp(log_dt_ref[...])
    mag = jnp.exp(lam_re * dt)
    lb_re = mag * jnp.cos(lam_im * dt)
    lb_im = mag * jnp.sin(lam_im * dt)
    den = lam_re * lam_re + lam_im * lam_im
    f_re = ((lb_re - 1.0) * lam_re + lb_im * lam_im) / den
    f_im = (lb_im * lam_re - (lb_re - 1.0) * lam_im) / den
    b_re = jnp.swapaxes(b_re_ref[...], 1, 2)
    b_im = jnp.swapaxes(b_im_ref[...], 1, 2)
    fr = f_re[:, None, :]
    fi = f_im[:, None, :]
    bb_re = fr * b_re - fi * b_im
    bb_im = fr * b_im + fi * b_re
    c_re = c_re_ref[...]
    c_im_neg = -c_im_ref[...]

    for ref in (wb_re, wb_im, wc_re, wc_im):
        ref[...] = jnp.zeros(ref.shape, ref.dtype)
    for g in range(SSM_GROUPS):
        cb, gl = divmod(g, SSM_GROUPS_PER_BLOCK)
        cols = slice(gl * SSM_GROUP_DIM, (gl + 1) * SSM_GROUP_DIM)
        states = slice(gl * SSM_STATE, (gl + 1) * SSM_STATE)
        wb_re[cb, cols, states] = bb_re[g]
        wb_im[cb, cols, states] = bb_im[g]
        wc_re[cb, cols, states] = c_re[g]
        wc_im[cb, cols, states] = c_im_neg[g]
        flat = slice(g * SSM_STATE, (g + 1) * SSM_STATE)
        lb_re_row[:, flat] = lb_re[g:g + 1, :]
        lb_im_row[:, flat] = lb_im[g:g + 1, :]

    lr_all = lb_re_row[...]
    li_all = lb_im_row[...]
    a2_re_ref[...] = lr_all * lr_all - li_all * li_all
    a2_im_ref[...] = 2.0 * lr_all * li_all

    halves = SSM_COL_BLOCK // V7X_LANES
    for cb in range(SSM_N_BLOCKS):
        flat = slice(cb * SSM_STATE_BLOCK, (cb + 1) * SSM_STATE_BLOCK)
        lr, li = lb_re_row[:, flat], lb_im_row[:, flat]
        l2r, l2i = a2_re_ref[:, flat], a2_im_ref[:, flat]
        w_r, w_i = wb_re[cb], wb_im[cb]
        wl_r, wl_i = w_r * lr - w_i * li, w_r * li + w_i * lr
        c_r, c_n = wc_re[cb], wc_im[cb]
        o1_r, o1_i = c_r * lr + c_n * li, c_n * lr - c_r * li
        o2_r, o2_i = c_r * l2r + c_n * l2i, c_n * l2r - c_r * l2i
        for half in range(halves):
            j = cb * halves + half
            r = slice(half * V7X_LANES, (half + 1) * V7X_LANES)
            s = slice(half * SSM_LANE_STATES, (half + 1) * SSM_LANE_STATES)
            t0, t1 = slice(0, V7X_LANES), slice(V7X_LANES, SSM_PAIR_COLS)
            re, im = slice(0, SSM_LANE_STATES), slice(SSM_LANE_STATES, SSM_PAIR_STATES)
            in_ref[j, t0, re] = wl_r[r, s].astype(BF16)
            in_ref[j, t0, im] = wl_i[r, s].astype(BF16)
            in_ref[j, t1, re] = w_r[r, s].astype(BF16)
            in_ref[j, t1, im] = w_i[r, s].astype(BF16)
            k0 = _dot_nt_f32(w_r[r, s], c_r[r, s]) + _dot_nt_f32(w_i[r, s], c_n[r, s])
            k1 = _dot_nt_f32(wl_r[r, s], c_r[r, s]) + _dot_nt_f32(wl_i[r, s], c_n[r, s])
            direct_ref[j, t0, t0] = k0.astype(BF16)
            direct_ref[j, t0, t1] = k1.astype(BF16)
            direct_ref[j, t1, t0] = jnp.zeros((V7X_LANES, V7X_LANES), BF16)
            direct_ref[j, t1, t1] = k0.astype(BF16)
            out_ref[j, re, t0] = o1_r[r, s].T.astype(BF16)
            out_ref[j, im, t0] = o1_i[r, s].T.astype(BF16)
            out_ref[j, re, t1] = o2_r[r, s].T.astype(BF16)
            out_ref[j, im, t1] = o2_i[r, s].T.astype(BF16)


def _ssm_prep(lam_re, lam_im, log_dt, b_re, b_im, c_re, c_im):
    g, p, _ = b_re.shape
    row = jax.ShapeDtypeStruct((1, SSM_FLAT), F32)
    n = SSM_N_LANE_BLOCKS
    block_diag = pltpu.VMEM((SSM_N_BLOCKS, SSM_COL_BLOCK, SSM_STATE_BLOCK), F32)
    return pl.pallas_call(
        _ssm_prep_kernel,
        out_shape=(row, row,
                   jax.ShapeDtypeStruct((n, SSM_PAIR_COLS, SSM_PAIR_STATES), BF16),
                   jax.ShapeDtypeStruct((n, SSM_PAIR_STATES, SSM_PAIR_COLS), BF16),
                   jax.ShapeDtypeStruct((n, SSM_PAIR_COLS, SSM_PAIR_COLS), BF16)),
        scratch_shapes=[pltpu.VMEM((1, SSM_FLAT), F32), pltpu.VMEM((1, SSM_FLAT), F32),
                        block_diag, block_diag, block_diag, block_diag],
        compiler_params=pltpu.CompilerParams(vmem_limit_bytes=V7X_VMEM_LIMIT_BYTES),
        name="ssm_prep",
    )(lam_re, lam_im, log_dt.reshape(g, 1), b_re, b_im, c_re, c_im)


def _pool_pieces(x_ref, dst_ref, ext_ref, w_ref, scale_ref, g, *, tt, bb, first_pos):
    tm = tt * bb
    halo = POOL_BUF * bb
    dst_ref[...] = _time_major(x_ref[...])
    ext_ref[halo:halo + tm, :] = _rmsnorm(dst_ref[...], g)
    yield
    row = lax.broadcasted_iota(jnp.int32, (tm, 1), 0)
    pos = first_pos + lax.shift_right_logical(row, int(math.log2(bb)))
    for gi, w in enumerate(POOL_WINDOWS):
        cols = slice(gi * POOL_GROUP_DIM, (gi + 1) * POOL_GROUP_DIM)
        acc = ext_ref[halo:halo + tm, cols]
        for k in range(1, w):
            acc = acc + ext_ref[halo - k * bb:halo - k * bb + tm, cols]
        inv_cnt = 1.0 / jnp.minimum(w, pos + 1).astype(F32)
        diff = acc * inv_cnt - ext_ref[halo:halo + tm, cols]
        dst_ref[:, cols] += _dot(diff.astype(BF16), w_ref[gi]) * scale_ref[:, cols]
        yield


def _ffn_ple_pieces(rows_ref, h_in_ref, p_ref, w_gate_ref, w_up_ref, w_down_ref, g_ple, ple_in_ref, ple_gate_ref,
                    h_ref, pt_ref, act_ref, release_h_in, emit):
    d_ff = act_ref.shape[1]
    pt_ref[...] = _time_major(p_ref[...]).astype(BF16)
    for c0 in range(0, d_ff, FFN_COLS):
        cols = slice(c0, min(c0 + FFN_COLS, d_ff))
        gate = _dot(h_in_ref[...], w_gate_ref[:, cols])
        up = _dot(h_in_ref[...], w_up_ref[:, cols])
        act_ref[:, cols] = (jax.nn.silu(gate) * up).astype(BF16)
        if cols.stop == d_ff:
            release_h_in()
        yield
    for n0 in range(0, D_MODEL, OUT_COLS):
        cols = slice(n0, n0 + OUT_COLS)
        rows_ref[:, cols] += _dot(act_ref[...], w_down_ref[:, cols])
        yield
    h_ref[...] = _rmsnorm(rows_ref[...], g_ple).astype(BF16)
    for n0 in range(0, D_MODEL, OUT_COLS):
        cols = slice(n0, n0 + OUT_COLS)
        ple_gate = jax.nn.sigmoid(_dot(h_ref[...], ple_gate_ref[:, cols]))
        rows_ref[:, cols] += _dot(pt_ref[...], ple_in_ref[:, cols]) * ple_gate
        yield
    emit()
    yield


_DONE = object()


def _interleave(*stages):
    live = list(stages)
    while live:
        for stage in list(live):
            if next(stage, _DONE) is _DONE:
                live.remove(stage)


def _then(stage, last_piece):
    yield from stage
    last_piece()
    yield


def _run_halves(sched, g_ffn, mid_ref, hmid_ref, rows_ref, mixer, ffn):
    h_in_reads_issued = []

    def mixer_into(dst_ref):
        def norm_for_ffn():
            assert h_in_reads_issued or not sched.lag, "hmid_ref rewritten before the FFN half read it"
            hmid_ref[...] = _rmsnorm(dst_ref[...], g_ffn).astype(BF16)
        return _then(mixer(dst_ref), norm_for_ffn)

    def release_h_in():
        h_in_reads_issued.append(True)

    if sched.lag:
        rows_ref[...] = mid_ref[...]
        _interleave(ffn(hmid_ref, release_h_in), mixer_into(mid_ref))
    else:
        _interleave(mixer_into(rows_ref))
        _interleave(ffn(hmid_ref, release_h_in))


def _init_handoff(sched, mid_ref, hmid_ref):
    if sched.lag:
        mid_ref[...] = jnp.zeros(mid_ref.shape, mid_ref.dtype)
        hmid_ref[...] = jnp.zeros(hmid_ref.shape, hmid_ref.dtype)


def _ffn_scratch(tm, d_ff, ple_dim):
    return [
        pltpu.VMEM((tm, D_MODEL), BF16),
        pltpu.VMEM((tm, D_MODEL), BF16),
        pltpu.VMEM((tm, ple_dim), BF16),
        pltpu.VMEM((tm, d_ff), BF16),
    ]


def _layer0_kernel(x_ref, prev_ref, p_ref, g_mix_ref, pool_w_ref, pool_scale_ref,
                   g_ffn_ref, w_gate_ref, w_up_ref, w_down_ref, g_ple_ref, ple_in_ref, ple_gate_ref,
                   o_ref, state_ref, ext_ref, mid_ref, hmid_ref, h_ref, pt_ref, act_ref,
                   *, sched, start):
    tt, bb = sched.tt, sched.bb
    tm = tt * bb
    halo = POOL_BUF * bb
    i = pl.program_id(1)
    rows_ref = o_ref.reshape(tm, D_MODEL)

    @pl.when(i == 0)
    def _():
        ext_ref[0:halo, :] = _time_major(prev_ref[...])
        _init_handoff(sched, mid_ref, hmid_ref)

    @pl.when(i > 0)
    def _():
        ext_ref[0:halo, :] = ext_ref[tm:tm + halo, :]

    def mixer(dst_ref):
        return _pool_pieces(x_ref, dst_ref, ext_ref, pool_w_ref, pool_scale_ref, g_mix_ref[...],
                            tt=tt, bb=bb, first_pos=start + sched.mixer_block(i) * tt)

    def ffn(h_in_ref, release_h_in):
        return _ffn_ple_pieces(rows_ref, h_in_ref, p_ref, w_gate_ref, w_up_ref, w_down_ref,
                               g_ple_ref[...], ple_in_ref, ple_gate_ref, h_ref, pt_ref, act_ref,
                               release_h_in, lambda: None)

    _run_halves(sched, g_ffn_ref[...], mid_ref, hmid_ref, rows_ref, mixer, ffn)

    @pl.when(i == sched.nt - 1)
    def _():
        state_ref[...] = _batch_major(ext_ref[tm:tm + halo, :], POOL_BUF, bb)


def _ffn_weight_specs(prm, layer):
    return [
        _layer_spec((1, D_MODEL), layer),
        _layer_spec(prm['ffn_w_gate'].shape[1:], layer), _layer_spec(prm['ffn_w_up'].shape[1:], layer),
        _layer_spec(prm['ffn_w_down'].shape[1:], layer),
        _layer_spec((1, D_MODEL), layer),
        _layer_spec(prm['ple_w_in'].shape[1:], layer), _layer_spec(prm['ple_w_gate'].shape[1:], layer),
    ]


def _ffn_weights(prm):
    return (prm['g_ffn'], prm['ffn_w_gate'], prm['ffn_w_up'], prm['ffn_w_down'],
            prm['g_ple'], prm['ple_w_in'], prm['ple_w_gate'])


def _layer0(x, prev, p, prm, *, start):
    b_len, t_len, _ = x.shape
    sched = _Schedule(b_len, t_len, POOL_LAYER_ROWS)
    tt, bb = sched.tt, sched.bb
    tm = tt * bb
    state_spec = pl.BlockSpec((bb, POOL_BUF, D_MODEL), lambda j, i: (j, 0, 0))
    mid_rows = tm if sched.lag else 2 * V7X_SUBLANES
    return pl.pallas_call(
        functools.partial(_layer0_kernel, sched=sched, start=start),
        grid=sched.grid,
        in_specs=[
            pl.BlockSpec((bb, tt, D_MODEL), lambda j, i: (j, sched.mixer_block(i), 0)),
            state_spec,
            pl.BlockSpec((None, bb, tt, p.shape[-1]), lambda j, i: (0, j, sched.ffn_block(i), 0)),
            _layer_spec((1, D_MODEL), 0),
            _const_spec(prm['pool_w'].shape),
            _const_spec((1, D_MODEL)),
        ] + _ffn_weight_specs(prm, 0),
        out_specs=[pl.BlockSpec((tt, bb, D_MODEL), lambda j, i: (sched.ffn_block(i), j, 0)), state_spec],
        out_shape=[
            jax.ShapeDtypeStruct((t_len, b_len, D_MODEL), F32),
            jax.ShapeDtypeStruct((b_len, POOL_BUF, D_MODEL), F32),
        ],
        scratch_shapes=([pltpu.VMEM(((POOL_BUF + tt) * bb, D_MODEL), F32),
                         pltpu.VMEM((mid_rows, D_MODEL), F32)]
                        + _ffn_scratch(tm, prm['ffn_w_gate'].shape[-1], p.shape[-1])),
        compiler_params=_params(),
        name="layer0_pool_ffn",
    )(x, prev, p, prm['g_mix'], prm['pool_w'], prm['pool_scale'], *_ffn_weights(prm))


def _ssm_kernel(x_ref, h0_re_ref, h0_im_ref, g_ref, a2_re_ref, a2_im_ref, in_ref, out_ref, direct_ref,
                d_ref, wglu_ref, o_ref, s_re_ref, s_im_ref,
                st_re, st_im, u_ref, xj_ref, sbuf_ref, z_ref, *, pairs, bb):
    m = pairs * bb
    i = pl.program_id(1)

    @pl.when(i == 0)
    def _():
        st_re[...] = h0_re_ref[...]
        st_im[...] = h0_im_ref[...]

    for t in range(2):
        u_ref[t] = _rmsnorm(x_ref[:, t].reshape(m, D_MODEL), g_ref[...])

    def contributions(j):
        cols = slice(j * V7X_LANES, (j + 1) * V7X_LANES)
        both = jnp.concatenate([u_ref[0, :, cols], u_ref[1, :, cols]], axis=-1)
        xj_ref[j] = both.astype(BF16)
        sbuf_ref[:, j * SSM_PAIR_STATES:(j + 1) * SSM_PAIR_STATES] = _dot(xj_ref[j], in_ref[j])

    def scan(j):
        states = slice(j * SSM_LANE_STATES, (j + 1) * SSM_LANE_STATES)
        re = slice(j * SSM_PAIR_STATES, j * SSM_PAIR_STATES + SSM_LANE_STATES)
        im = slice(j * SSM_PAIR_STATES + SSM_LANE_STATES, (j + 1) * SSM_PAIR_STATES)
        ar = jnp.broadcast_to(a2_re_ref[:, states], (V7X_SUBLANES, SSM_LANE_STATES))
        ai = jnp.broadcast_to(a2_im_ref[:, states], (V7X_SUBLANES, SSM_LANE_STATES))
        for r0 in range(0, bb, V7X_SUBLANES):
            sr = st_re[r0:r0 + V7X_SUBLANES, states]
            si = st_im[r0:r0 + V7X_SUBLANES, states]
            for c in range(pairs):
                rows = slice(c * bb + r0, c * bb + r0 + V7X_SUBLANES)
                pr, pi = sbuf_ref[rows, re], sbuf_ref[rows, im]
                sbuf_ref[rows, re] = sr
                sbuf_ref[rows, im] = si
                sr, si = ar * sr - ai * si + pr, ar * si + ai * sr + pi
            st_re[r0:r0 + V7X_SUBLANES, states] = sr
            st_im[r0:r0 + V7X_SUBLANES, states] = si

    def outputs(j):
        cols = slice(j * V7X_LANES, (j + 1) * V7X_LANES)
        entering = sbuf_ref[:, j * SSM_PAIR_STATES:(j + 1) * SSM_PAIR_STATES].astype(BF16)
        y = _dot(entering, out_ref[j]) + _dot(xj_ref[j], direct_ref[j])
        for t in range(2):
            yt = y[:, t * V7X_LANES:(t + 1) * V7X_LANES] + d_ref[:, cols] * u_ref[t, :, cols]
            z_ref[t, :, cols] = jax.nn.gelu(yt).astype(BF16)

    for k in range(SSM_N_LANE_BLOCKS + 2):
        if 0 <= k - 2:
            outputs(k - 2)
        if k < SSM_N_LANE_BLOCKS:
            contributions(k)
        if 0 <= k - 1 < SSM_N_LANE_BLOCKS:
            scan(k - 1)

    for t in range(2):
        for n0 in range(0, D_MODEL, OUT_COLS):
            cols = slice(n0, n0 + OUT_COLS)
            a = _dot(z_ref[t], wglu_ref[:, cols])
            gate = _dot(z_ref[t], wglu_ref[:, D_MODEL + n0:D_MODEL + n0 + OUT_COLS])
            o_ref[:, t, :, cols] = x_ref[:, t, :, cols] + (a * jax.nn.sigmoid(gate)).reshape(pairs, bb, OUT_COLS)

    s_re_ref[...] = st_re[...]
    s_im_ref[...] = st_im[...]


def _ssm_layer(x, h0_re, h0_im, prm):
    t_len, b_len, _ = x.shape
    tt, bb = _tiles(b_len, t_len, SSM_ROWS)
    assert tt % 2 == 0
    pairs = tt // 2
    m = pairs * bb
    x_spec = pl.BlockSpec((pairs, 2, bb, D_MODEL), lambda j, i: (i, 0, j, 0))
    st_spec = pl.BlockSpec((bb, SSM_FLAT), lambda j, i: (j, 0))
    out, s_re, s_im = pl.pallas_call(
        functools.partial(_ssm_kernel, pairs=pairs, bb=bb),
        grid=(b_len // bb, t_len // tt),
        in_specs=[
            x_spec, st_spec, st_spec,
            _layer_spec((1, D_MODEL), 1),
            _const_spec((1, SSM_FLAT)), _const_spec((1, SSM_FLAT)),
            _const_spec(prm['ssm_in'].shape), _const_spec(prm['ssm_out'].shape),
            _const_spec(prm['ssm_direct'].shape),
            _const_spec((1, D_MODEL)),
            _const_spec(prm['ssm_w_glu'].shape),
        ],
        out_specs=[x_spec, st_spec, st_spec],
        out_shape=[
            jax.ShapeDtypeStruct((t_len // 2, 2, b_len, D_MODEL), F32),
            jax.ShapeDtypeStruct((b_len, SSM_FLAT), F32),
            jax.ShapeDtypeStruct((b_len, SSM_FLAT), F32),
        ],
        scratch_shapes=[
            pltpu.VMEM((bb, SSM_FLAT), F32), pltpu.VMEM((bb, SSM_FLAT), F32),
            pltpu.VMEM((2, m, D_MODEL), F32),
            pltpu.VMEM((SSM_N_LANE_BLOCKS, m, SSM_PAIR_COLS), BF16),
            pltpu.VMEM((m, SSM_N_LANE_BLOCKS * SSM_PAIR_STATES), F32),
            pltpu.VMEM((2, m, D_MODEL), BF16),
        ],
        compiler_params=_params(),
        name="ssm_mixer",
    )(x.reshape(t_len // 2, 2, b_len, D_MODEL), h0_re, h0_im, prm['g_mix'], prm['a2_re'], prm['a2_im'],
      prm['ssm_in'], prm['ssm_out'], prm['ssm_direct'], prm['ssm_d'], prm['ssm_w_glu'])
    return out.reshape(t_len, b_len, D_MODEL), s_re, s_im


def _ffn1_kernel(x_ref, p_ref, g_ffn_ref, w_gate_ref, w_up_ref, w_down_ref, g_ple_ref, ple_in_ref, ple_gate_ref,
                 g_final_ref, o_ref, rows_ref, h_in_ref, h_ref, pt_ref, act_ref, *, tt, bb):
    rows_ref[...] = x_ref[...].reshape(tt * bb, D_MODEL)
    h_in_ref[...] = _rmsnorm(rows_ref[...], g_ffn_ref[...]).astype(BF16)

    def emit():
        o_ref[...] = _batch_major(_rmsnorm(rows_ref[...], g_final_ref[...]), tt, bb)

    _interleave(_ffn_ple_pieces(rows_ref, h_in_ref, p_ref, w_gate_ref, w_up_ref, w_down_ref,
                                g_ple_ref[...], ple_in_ref, ple_gate_ref, h_ref, pt_ref, act_ref,
                                lambda: None, emit))


def _ffn1_layer(x, p, prm):
    t_len, b_len, _ = x.shape
    tt, bb = _tiles(b_len, t_len, FFN_ROWS)
    return pl.pallas_call(
        functools.partial(_ffn1_kernel, tt=tt, bb=bb),
        grid=(b_len // bb, t_len // tt),
        in_specs=[
            pl.BlockSpec((tt, bb, D_MODEL), lambda j, i: (i, j, 0)),
            pl.BlockSpec((None, bb, tt, p.shape[-1]), lambda j, i: (1, j, i, 0)),
        ] + _ffn_weight_specs(prm, 1) + [_const_spec((1, D_MODEL))],
        out_specs=pl.BlockSpec((bb, tt, D_MODEL), lambda j, i: (j, i, 0)),
        out_shape=jax.ShapeDtypeStruct((b_len, t_len, D_MODEL), F32),
        scratch_shapes=([pltpu.VMEM((tt * bb, D_MODEL), F32)]
                        + _ffn_scratch(tt * bb, prm['ffn_w_gate'].shape[-1], p.shape[-1])),
        compiler_params=_params(),
        name="layer1_ffn",
    )(x, p, *_ffn_weights(prm), prm['g_final'])


def _run_trunk(x, p, start, pool_state, ssm_re, ssm_im, prm):
    b_len = x.shape[0]
    xt, pool_new = _layer0(x, pool_state, p, prm, start=start)
    xt, s_re, s_im = _ssm_layer(xt, ssm_re.reshape(b_len, SSM_FLAT), ssm_im.reshape(b_len, SSM_FLAT), prm)
    y = _ffn1_layer(xt, p, prm)
    s_re = s_re.reshape(1, b_len, SSM_GROUPS, SSM_STATE)
    s_im = s_im.reshape(1, b_len, SSM_GROUPS, SSM_STATE)
    return y, pool_new[None], s_re, s_im


def kernel(x_prompt, x_sample, state_pool, state_ssm_re, state_ssm_im, p_prompt, p_sample, g_mix, g_ffn, g_ple, g_final, pool_w, pool_scale, ssm_lambda_re, ssm_lambda_im, ssm_log_dt, ssm_b_re, ssm_b_im, ssm_c_re, ssm_c_im, ssm_d, ssm_w_glu, ffn_w_gate, ffn_w_up, ffn_w_down, ple_w_in, ple_w_gate):
    depth = g_mix.shape[0]
    a2_re, a2_im, ssm_in, ssm_out, ssm_direct = _ssm_prep(
        ssm_lambda_re[0], ssm_lambda_im[0], ssm_log_dt[0], ssm_b_re[0], ssm_b_im[0], ssm_c_re[0], ssm_c_im[0])
    per_layer_vec = lambda a: a.reshape(depth, 1, D_MODEL)
    prm = dict(
        g_mix=per_layer_vec(g_mix), g_ffn=per_layer_vec(g_ffn), g_ple=per_layer_vec(g_ple),
        g_final=g_final.reshape(1, D_MODEL),
        pool_w=pool_w[0].astype(BF16), pool_scale=pool_scale,
        a2_re=a2_re, a2_im=a2_im,
        ssm_in=ssm_in, ssm_out=ssm_out, ssm_direct=ssm_direct,
        ssm_d=ssm_d, ssm_w_glu=ssm_w_glu[0].astype(BF16),
        ffn_w_gate=ffn_w_gate.astype(BF16), ffn_w_up=ffn_w_up.astype(BF16),
        ffn_w_down=ffn_w_down.astype(BF16),
        ple_w_in=ple_w_in.astype(BF16), ple_w_gate=ple_w_gate.astype(BF16),
    )
    b_p = x_prompt.shape[0]
    zeros_pool = jnp.zeros((b_p, POOL_BUF, D_MODEL), F32)
    zeros_ssm = jnp.zeros((b_p, SSM_GROUPS, SSM_STATE), F32)
    y_p, pool_p, re_p, im_p = _run_trunk(x_prompt, p_prompt, 0, zeros_pool, zeros_ssm, zeros_ssm, prm)
    y_s, pool_s, re_s, im_s = _run_trunk(x_sample, p_sample, PAST_LEN, state_pool[0],
                                         state_ssm_re[0], state_ssm_im[0], prm)
    return (y_p, y_s, pool_p, pool_s, re_p, im_p, re_s, im_s)
```

```python
import functools
import math

import jax
import jax.numpy as jnp
from jax import lax
from jax.experimental import pallas as pl
from jax.experimental.pallas import tpu as pltpu

D_MODEL = 1024
POOL_WINDOWS = (2, 4, 8, 16)
POOL_GROUP_DIM = D_MODEL // len(POOL_WINDOWS)
POOL_BUF = max(POOL_WINDOWS) - 1
SSM_GROUP_DIM = 16
SSM_GROUPS = D_MODEL // SSM_GROUP_DIM
SSM_STATE = 64
SSM_FLAT = SSM_GROUPS * SSM_STATE
PAST_LEN = 16384
EPS = 1e-6

V7X_SUBLANES = 8
V7X_LANES = 128
V7X_MXU_DIM = 256
V7X_VMEM_LIMIT_BYTES = 60 * 1024 * 1024

SSM_COL_BLOCK = V7X_MXU_DIM
SSM_GROUPS_PER_BLOCK = SSM_COL_BLOCK // SSM_GROUP_DIM
SSM_STATE_BLOCK = SSM_GROUPS_PER_BLOCK * SSM_STATE
SSM_N_BLOCKS = D_MODEL // SSM_COL_BLOCK
SSM_LANE_GROUPS = V7X_LANES // SSM_GROUP_DIM
SSM_LANE_STATES = SSM_LANE_GROUPS * SSM_STATE
SSM_N_LANE_BLOCKS = D_MODEL // V7X_LANES
SSM_PAIR_COLS = 2 * V7X_LANES
SSM_PAIR_STATES = 2 * SSM_LANE_STATES

POOL_LAYER_ROWS = 512
SSM_ROWS = 1024
FFN_ROWS = 1024
FFN_COLS = 2 * V7X_MXU_DIM
OUT_COLS = 2 * V7X_MXU_DIM

BF16 = jnp.bfloat16
F32 = jnp.float32


def _rmsnorm(x, g):
    return x * lax.rsqrt(jnp.mean(x * x, axis=-1, keepdims=True) + EPS) * g


def _dot(a, b):
    return jnp.dot(a, b, preferred_element_type=F32)


def _dot_nt_f32(a, b):
    return lax.dot_general(a, b, (((1,), (1,)), ((), ())), precision=lax.Precision.HIGHEST,
                           preferred_element_type=F32)


def _time_major(block):
    bb, tt, c = block.shape
    return jnp.transpose(block, (1, 0, 2)).reshape(tt * bb, c)


def _batch_major(rows, tt, bb):
    return jnp.transpose(rows.reshape(tt, bb, rows.shape[-1]), (1, 0, 2))


def _const_spec(shape):
    zeros = (0,) * len(shape)
    return pl.BlockSpec(shape, lambda *_: zeros, pipeline_mode=pl.Buffered(1))


def _layer_spec(shape, layer):
    tail = (0,) * len(shape)
    return pl.BlockSpec((None,) + tuple(shape), lambda *_: (layer,) + tail,
                        pipeline_mode=pl.Buffered(1))


def _params():
    return pltpu.CompilerParams(dimension_semantics=("arbitrary", "arbitrary"),
                                vmem_limit_bytes=V7X_VMEM_LIMIT_BYTES)


def _tiles(b_len, t_len, rows):
    bb = min(b_len, max(V7X_SUBLANES, rows // t_len))
    tt = min(t_len, rows // bb)
    assert t_len % tt == 0 and b_len % bb == 0 and bb % V7X_SUBLANES == 0
    return tt, bb


class _Schedule:
    def __init__(self, b_len, t_len, rows):
        self.tt, self.bb = _tiles(b_len, t_len, rows)
        self.nt = t_len // self.tt
        self.lag = 1 if self.nt > 1 else 0
        self.grid = (b_len // self.bb, self.nt + self.lag)

    def mixer_block(self, i):
        return jnp.minimum(i, self.nt - 1)

    def ffn_block(self, i):
        return jnp.maximum(i - self.lag, 0)


def _ssm_prep_kernel(lam_re_ref, lam_im_ref, log_dt_ref, b_re_ref, b_im_ref, c_re_ref, c_im_ref,
                     a2_re_ref, a2_im_ref, in_ref, out_ref, direct_ref,
                     lb_re_row, lb_im_row, wb_re, wb_im, wc_re, wc_im):
    lam_re = lam_re_ref[...]
    lam_im = lam_im_ref[...]
    dt = jnp.exp(log_dt_ref[...])
    mag = jnp.exp(lam_re * dt)
    lb_re = mag * jnp.cos(lam_im * dt)
    lb_im = mag * jnp.sin(lam_im * dt)
    den = lam_re * lam_re + lam_im * lam_im
    f_re = ((lb_re - 1.0) * lam_re + lb_im * lam_im) / den
    f_im = (lb_im * lam_re - (lb_re - 1.0) * lam_im) / den
    b_re = jnp.swapaxes(b_re_ref[...], 1, 2)
    b_im = jnp.swapaxes(b_im_ref[...], 1, 2)
    fr = f_re[:, None, :]
    fi = f_im[:, None, :]
    bb_re = fr * b_re - fi * b_im
    bb_im = fr * b_im + fi * b_re
    c_re = c_re_ref[...]
    c_im_neg = -c_im_ref[...]

    for ref in (wb_re, wb_im, wc_re, wc_im):
        ref[...] = jnp.zeros(ref.shape, ref.dtype)
    for g in range(SSM_GROUPS):
        cb, gl = divmod(g, SSM_GROUPS_PER_BLOCK)
        cols = slice(gl * SSM_GROUP_DIM, (gl + 1) * SSM_GROUP_DIM)
        states = slice(gl * SSM_STATE, (gl + 1) * SSM_STATE)
        wb_re[cb, cols, states] = bb_re[g]
        wb_im[cb, cols, states] = bb_im[g]
        wc_re[cb, cols, states] = c_re[g]
        wc_im[cb, cols, states] = c_im_neg[g]
        flat = slice(g * SSM_STATE, (g + 1) * SSM_STATE)
        lb_re_row[:, flat] = lb_re[g:g + 1, :]
        lb_im_row[:, flat] = lb_im[g:g + 1, :]

    lr_all = lb_re_row[...]
    li_all = lb_im_row[...]
    a2_re_ref[...] = lr_all * lr_all - li_all * li_all
    a2_im_ref[...] = 2.0 * lr_all * li_all

    halves = SSM_COL_BLOCK // V7X_LANES
    for cb in range(SSM_N_BLOCKS):
        flat = slice(cb * SSM_STATE_BLOCK, (cb + 1) * SSM_STATE_BLOCK)
        lr, li = lb_re_row[:, flat], lb_im_row[:, flat]
        l2r, l2i = a2_re_ref[:, flat], a2_im_ref[:, flat]
        w_r, w_i = wb_re[cb], wb_im[cb]
        wl_r, wl_i = w_r * lr - w_i * li, w_r * li + w_i * lr
        c_r, c_n = wc_re[cb], wc_im[cb]
        o1_r, o1_i = c_r * lr + c_n * li, c_n * lr - c_r * li
        o2_r, o2_i = c_r * l2r + c_n * l2i, c_n * l2r - c_r * l2i
        for half in range(halves):
            j = cb * halves + half
            r = slice(half * V7X_LANES, (half + 1) * V7X_LANES)
            s = slice(half * SSM_LANE_STATES, (half + 1) * SSM_LANE_STATES)
            t0, t1 = slice(0, V7X_LANES), slice(V7X_LANES, SSM_PAIR_COLS)
            re, im = slice(0, SSM_LANE_STATES), slice(SSM_LANE_STATES, SSM_PAIR_STATES)
            in_ref[j, t0, re] = wl_r[r, s].astype(BF16)
            in_ref[j, t0, im] = wl_i[r, s].astype(BF16)
            in_ref[j, t1, re] = w_r[r, s].astype(BF16)
            in_ref[j, t1, im] = w_i[r, s].astype(BF16)
            k0 = _dot_nt_f32(w_r[r, s], c_r[r, s]) + _dot_nt_f32(w_i[r, s], c_n[r, s])
            k1 = _dot_nt_f32(wl_r[r, s], c_r[r, s]) + _dot_nt_f32(wl_i[r, s], c_n[r, s])
            direct_ref[j, t0, t0] = k0.astype(BF16)
            direct_ref[j, t0, t1] = k1.astype(BF16)
            direct_ref[j, t1, t0] = jnp.zeros((V7X_LANES, V7X_LANES), BF16)
            direct_ref[j, t1, t1] = k0.astype(BF16)
            out_ref[j, re, t0] = o1_r[r, s].T.astype(BF16)
            out_ref[j, im, t0] = o1_i[r, s].T.astype(BF16)
            out_ref[j, re, t1] = o2_r[r, s].T.astype(BF16)
            out_ref[j, im, t1] = o2_i[r, s].T.astype(BF16)


def _ssm_prep(lam_re, lam_im, log_dt, b_re, b_im, c_re, c_im):
    g, p, _ = b_re.shape
    row = jax.ShapeDtypeStruct((1, SSM_FLAT), F32)
    n = SSM_N_LANE_BLOCKS
    block_diag = pltpu.VMEM((SSM_N_BLOCKS, SSM_COL_BLOCK, SSM_STATE_BLOCK), F32)
    return pl.pallas_call(
        _ssm_prep_kernel,
        out_shape=(row, row,
                   jax.ShapeDtypeStruct((n, SSM_PAIR_COLS, SSM_PAIR_STATES), BF16),
                   jax.ShapeDtypeStruct((n, SSM_PAIR_STATES, SSM_PAIR_COLS), BF16),
                   jax.ShapeDtypeStruct((n, SSM_PAIR_COLS, SSM_PAIR_COLS), BF16)),
        scratch_shapes=[pltpu.VMEM((1, SSM_FLAT), F32), pltpu.VMEM((1, SSM_FLAT), F32),
                        block_diag, block_diag, block_diag, block_diag],
        compiler_params=pltpu.CompilerParams(vmem_limit_bytes=V7X_VMEM_LIMIT_BYTES),
        name="ssm_prep",
    )(lam_re, lam_im, log_dt.reshape(g, 1), b_re, b_im, c_re, c_im)


def _pool_pieces(x_ref, dst_ref, ext_ref, w_ref, scale_ref, g, *, tt, bb, first_pos):
    tm = tt * bb
    halo = POOL_BUF * bb
    dst_ref[...] = _time_major(x_ref[...])
    ext_ref[halo:halo + tm, :] = _rmsnorm(dst_ref[...], g)
    yield
    row = lax.broadcasted_iota(jnp.int32, (tm, 1), 0)
    pos = first_pos + lax.shift_right_logical(row, int(math.log2(bb)))
    for gi, w in enumerate(POOL_WINDOWS):
        cols = slice(gi * POOL_GROUP_DIM, (gi + 1) * POOL_GROUP_DIM)
        acc = ext_ref[halo:halo + tm, cols]
        for k in range(1, w):
            acc = acc + ext_ref[halo - k * bb:halo - k * bb + tm, cols]
        inv_cnt = 1.0 / jnp.minimum(w, pos + 1).astype(F32)
        diff = acc * inv_cnt - ext_ref[halo:halo + tm, cols]
        dst_ref[:, cols] += _dot(diff.astype(BF16), w_ref[gi]) * scale_ref[:, cols]
        yield


def _ffn_ple_pieces(x_ref, rows_ref, h_in_ref, p_ref, w_gate_ref, w_up_ref, w_down_ref, g_ple, ple_in_ref,
                    ple_gate_ref, h_ref, pt_ref, act_ref, release_h_in, emit):
    d_ff = act_ref.shape[1]
    pt_ref[...] = _time_major(p_ref[...]).astype(BF16)
    for c0 in range(0, d_ff, FFN_COLS):
        cols = slice(c0, min(c0 + FFN_COLS, d_ff))
        gate = _dot(h_in_ref[...], w_gate_ref[:, cols])
        up = _dot(h_in_ref[...], w_up_ref[:, cols])
        act_ref[:, cols] = (jax.nn.silu(gate) * up).astype(BF16)
        if cols.stop == d_ff:
            release_h_in()
        yield
    for n0 in range(0, D_MODEL, OUT_COLS):
        cols = slice(n0, n0 + OUT_COLS)
        rows_ref[:, cols] = x_ref[:, cols] + _dot(act_ref[...], w_down_ref[:, cols])
        yield
    h_ref[...] = _rmsnorm(rows_ref[...], g_ple).astype(BF16)
    for n0 in range(0, D_MODEL, OUT_COLS):
        cols = slice(n0, n0 + OUT_COLS)
        ple_gate = jax.nn.sigmoid(_dot(h_ref[...], ple_gate_ref[:, cols]))
        rows_ref[:, cols] += _dot(pt_ref[...], ple_in_ref[:, cols]) * ple_gate
        yield
    emit()
    yield


_DONE = object()


def _interleave(*stages):
    live = list(stages)
    while live:
        for stage in list(live):
            if next(stage, _DONE) is _DONE:
                live.remove(stage)


def _then(stage, last_piece):
    yield from stage
    last_piece()
    yield


def _run_halves(sched, g_ffn, mid_ref, hmid_ref, rows_ref, mixer, ffn):
    h_in_reads_issued = []

    def mixer_into(dst_ref):
        def norm_for_ffn():
            assert h_in_reads_issued or not sched.lag, "hmid_ref rewritten before the FFN half read it"
            hmid_ref[...] = _rmsnorm(dst_ref[...], g_ffn).astype(BF16)
        return _then(mixer(dst_ref), norm_for_ffn)

    def release_h_in():
        h_in_reads_issued.append(True)

    if sched.lag:
        rows_ref[...] = mid_ref[...]
        _interleave(ffn(hmid_ref, release_h_in), mixer_into(mid_ref))
    else:
        _interleave(mixer_into(rows_ref))
        _interleave(ffn(hmid_ref, release_h_in))


def _init_handoff(sched, mid_ref, hmid_ref):
    if sched.lag:
        mid_ref[...] = jnp.zeros(mid_ref.shape, mid_ref.dtype)
        hmid_ref[...] = jnp.zeros(hmid_ref.shape, hmid_ref.dtype)


def _ffn_scratch(tm, d_ff, ple_dim):
    return [
        pltpu.VMEM((tm, D_MODEL), BF16),
        pltpu.VMEM((tm, D_MODEL), BF16),
        pltpu.VMEM((tm, ple_dim), BF16),
        pltpu.VMEM((tm, d_ff), BF16),
    ]


def _layer0_kernel(x_ref, prev_ref, p_ref, g_mix_ref, pool_w_ref, pool_scale_ref,
                   g_ffn_ref, w_gate_ref, w_up_ref, w_down_ref, g_ple_ref, ple_in_ref, ple_gate_ref,
                   o_ref, state_ref, ext_ref, mid_ref, hmid_ref, h_ref, pt_ref, act_ref,
                   *, sched, start):
    tt, bb = sched.tt, sched.bb
    tm = tt * bb
    halo = POOL_BUF * bb
    i = pl.program_id(1)
    rows_ref = o_ref.reshape(tm, D_MODEL)

    @pl.when(i == 0)
    def _():
        ext_ref[0:halo, :] = _time_major(prev_ref[...])
        _init_handoff(sched, mid_ref, hmid_ref)

    @pl.when(i > 0)
    def _():
        ext_ref[0:halo, :] = ext_ref[tm:tm + halo, :]

    def mixer(dst_ref):
        return _pool_pieces(x_ref, dst_ref, ext_ref, pool_w_ref, pool_scale_ref, g_mix_ref[...],
                            tt=tt, bb=bb, first_pos=start + sched.mixer_block(i) * tt)

    def ffn(h_in_ref, release_h_in):
        return _ffn_ple_pieces(rows_ref, rows_ref, h_in_ref, p_ref, w_gate_ref, w_up_ref, w_down_ref,
                               g_ple_ref[...], ple_in_ref, ple_gate_ref, h_ref, pt_ref, act_ref,
                               release_h_in, lambda: None)

    _run_halves(sched, g_ffn_ref[...], mid_ref, hmid_ref, rows_ref, mixer, ffn)

    @pl.when(i == sched.nt - 1)
    def _():
        state_ref[...] = _batch_major(ext_ref[tm:tm + halo, :], POOL_BUF, bb)


def _ffn_weight_specs(prm, layer):
    return [
        _layer_spec((1, D_MODEL), layer),
        _layer_spec(prm['ffn_w_gate'].shape[1:], layer), _layer_spec(prm['ffn_w_up'].shape[1:], layer),
        _layer_spec(prm['ffn_w_down'].shape[1:], layer),
        _layer_spec((1, D_MODEL), layer),
        _layer_spec(prm['ple_w_in'].shape[1:], layer), _layer_spec(prm['ple_w_gate'].shape[1:], layer),
    ]


def _ffn_weights(prm):
    return (prm['g_ffn'], prm['ffn_w_gate'], prm['ffn_w_up'], prm['ffn_w_down'],
            prm['g_ple'], prm['ple_w_in'], prm['ple_w_gate'])


def _layer0(x, prev, p, prm, *, start):
    b_len, t_len, _ = x.shape
    sched = _Schedule(b_len, t_len, POOL_LAYER_ROWS)
    tt, bb = sched.tt, sched.bb
    tm = tt * bb
    state_spec = pl.BlockSpec((bb, POOL_BUF, D_MODEL), lambda j, i: (j, 0, 0))
    mid_rows = tm if sched.lag else 2 * V7X_SUBLANES
    return pl.pallas_call(
        functools.partial(_layer0_kernel, sched=sched, start=start),
        grid=sched.grid,
        in_specs=[
            pl.BlockSpec((bb, tt, D_MODEL), lambda j, i: (j, sched.mixer_block(i), 0)),
            state_spec,
            pl.BlockSpec((None, bb, tt, p.shape[-1]), lambda j, i: (0, j, sched.ffn_block(i), 0)),
            _layer_spec((1, D_MODEL), 0),
            _const_spec(prm['pool_w'].shape),
            _const_spec((1, D_MODEL)),
        ] + _ffn_weight_specs(prm, 0),
        out_specs=[pl.BlockSpec((tt, bb, D_MODEL), lambda j, i: (sched.ffn_block(i), j, 0)), state_spec],
        out_shape=[
            jax.ShapeDtypeStruct((t_len, b_len, D_MODEL), F32),
            jax.ShapeDtypeStruct((b_len, POOL_BUF, D_MODEL), F32),
        ],
        scratch_shapes=([pltpu.VMEM(((POOL_BUF + tt) * bb, D_MODEL), F32),
                         pltpu.VMEM((mid_rows, D_MODEL), F32)]
                        + _ffn_scratch(tm, prm['ffn_w_gate'].shape[-1], p.shape[-1])),
        compiler_params=_params(),
        name="layer0_pool_ffn",
    )(x, prev, p, prm['g_mix'], prm['pool_w'], prm['pool_scale'], *_ffn_weights(prm))


def _ssm_kernel(x_ref, h0_re_ref, h0_im_ref, g_ref, a2_re_ref, a2_im_ref, in_ref, out_ref, direct_ref,
                d_ref, wglu_ref, o_ref, s_re_ref, s_im_ref,
                st_re, st_im, u_ref, xj_ref, sbuf_ref, z_ref, *, pairs, bb):
    m = pairs * bb
    i = pl.program_id(1)

    @pl.when(i == 0)
    def _():
        st_re[...] = h0_re_ref[...]
        st_im[...] = h0_im_ref[...]

    for t in range(2):
        u_ref[t] = _rmsnorm(x_ref[:, t].reshape(m, D_MODEL), g_ref[...])

    def contributions(j):
        cols = slice(j * V7X_LANES, (j + 1) * V7X_LANES)
        both = jnp.concatenate([u_ref[0, :, cols], u_ref[1, :, cols]], axis=-1)
        xj_ref[j] = both.astype(BF16)
        sbuf_ref[:, j * SSM_PAIR_STATES:(j + 1) * SSM_PAIR_STATES] = _dot(xj_ref[j], in_ref[j])

    def scan(j):
        states = slice(j * SSM_LANE_STATES, (j + 1) * SSM_LANE_STATES)
        re = slice(j * SSM_PAIR_STATES, j * SSM_PAIR_STATES + SSM_LANE_STATES)
        im = slice(j * SSM_PAIR_STATES + SSM_LANE_STATES, (j + 1) * SSM_PAIR_STATES)
        ar = jnp.broadcast_to(a2_re_ref[:, states], (V7X_SUBLANES, SSM_LANE_STATES))
        ai = jnp.broadcast_to(a2_im_ref[:, states], (V7X_SUBLANES, SSM_LANE_STATES))
        for r0 in range(0, bb, V7X_SUBLANES):
            sr = st_re[r0:r0 + V7X_SUBLANES, states]
            si = st_im[r0:r0 + V7X_SUBLANES, states]
            for c in range(pairs):
                rows = slice(c * bb + r0, c * bb + r0 + V7X_SUBLANES)
                pr, pi = sbuf_ref[rows, re], sbuf_ref[rows, im]
                sbuf_ref[rows, re] = sr
                sbuf_ref[rows, im] = si
                sr, si = ar * sr - ai * si + pr, ar * si + ai * sr + pi
            st_re[r0:r0 + V7X_SUBLANES, states] = sr
            st_im[r0:r0 + V7X_SUBLANES, states] = si

    def outputs(j):
        cols = slice(j * V7X_LANES, (j + 1) * V7X_LANES)
        entering = sbuf_ref[:, j * SSM_PAIR_STATES:(j + 1) * SSM_PAIR_STATES].astype(BF16)
        y = _dot(entering, out_ref[j]) + _dot(xj_ref[j], direct_ref[j])
        for t in range(2):
            yt = y[:, t * V7X_LANES:(t + 1) * V7X_LANES] + d_ref[:, cols] * u_ref[t, :, cols]
            z_ref[t, :, cols] = jax.nn.gelu(yt).astype(BF16)

    for k in range(SSM_N_LANE_BLOCKS + 2):
        if 0 <= k - 2:
            outputs(k - 2)
        if k < SSM_N_LANE_BLOCKS:
            contributions(k)
        if 0 <= k - 1 < SSM_N_LANE_BLOCKS:
            scan(k - 1)

    for t in range(2):
        for n0 in range(0, D_MODEL, OUT_COLS):
            cols = slice(n0, n0 + OUT_COLS)
            a = _dot(z_ref[t], wglu_ref[:, cols])
            gate = _dot(z_ref[t], wglu_ref[:, D_MODEL + n0:D_MODEL + n0 + OUT_COLS])
            o_ref[:, t, :, cols] = x_ref[:, t, :, cols] + (a * jax.nn.sigmoid(gate)).reshape(pairs, bb, OUT_COLS)

    s_re_ref[...] = st_re[...]
    s_im_ref[...] = st_im[...]


def _ssm_layer(x, h0_re, h0_im, prm):
    t_len, b_len, _ = x.shape
    tt, bb = _tiles(b_len, t_len, SSM_ROWS)
    assert tt % 2 == 0
    pairs = tt // 2
    m = pairs * bb
    x_spec = pl.BlockSpec((pairs, 2, bb, D_MODEL), lambda j, i: (i, 0, j, 0))
    st_spec = pl.BlockSpec((bb, SSM_FLAT), lambda j, i: (j, 0))
    out, s_re, s_im = pl.pallas_call(
        functools.partial(_ssm_kernel, pairs=pairs, bb=bb),
        grid=(b_len // bb, t_len // tt),
        in_specs=[
            x_spec, st_spec, st_spec,
            _layer_spec((1, D_MODEL), 1),
            _const_spec((1, SSM_FLAT)), _const_spec((1, SSM_FLAT)),
            _const_spec(prm['ssm_in'].shape), _const_spec(prm['ssm_out'].shape),
            _const_spec(prm['ssm_direct'].shape),
            _const_spec((1, D_MODEL)),
            _const_spec(prm['ssm_w_glu'].shape),
        ],
        out_specs=[x_spec, st_spec, st_spec],
        out_shape=[
            jax.ShapeDtypeStruct((t_len // 2, 2, b_len, D_MODEL), F32),
            jax.ShapeDtypeStruct((b_len, SSM_FLAT), F32),
            jax.ShapeDtypeStruct((b_len, SSM_FLAT), F32),
        ],
        scratch_shapes=[
            pltpu.VMEM((bb, SSM_FLAT), F32), pltpu.VMEM((bb, SSM_FLAT), F32),
            pltpu.VMEM((2, m, D_MODEL), F32),
            pltpu.VMEM((SSM_N_LANE_BLOCKS, m, SSM_PAIR_COLS), BF16),
            pltpu.VMEM((m, SSM_N_LANE_BLOCKS * SSM_PAIR_STATES), F32),
            pltpu.VMEM((2, m, D_MODEL), BF16),
        ],
        compiler_params=_params(),
        name="ssm_mixer",
    )(x.reshape(t_len // 2, 2, b_len, D_MODEL), h0_re, h0_im, prm['g_mix'], prm['a2_re'], prm['a2_im'],
      prm['ssm_in'], prm['ssm_out'], prm['ssm_direct'], prm['ssm_d'], prm['ssm_w_glu'])
    return out.reshape(t_len, b_len, D_MODEL), s_re, s_im


def _ffn1_kernel(x_ref, p_ref, g_ffn_ref, w_gate_ref, w_up_ref, w_down_ref, g_ple_ref, ple_in_ref, ple_gate_ref,
                 g_final_ref, o_ref, rows_ref, h_in_ref, h_ref, pt_ref, act_ref, *, tt, bb):
    x_rows = x_ref.reshape(tt * bb, D_MODEL)
    h_in_ref[...] = _rmsnorm(x_rows[...], g_ffn_ref[...]).astype(BF16)

    def emit():
        o_ref[...] = _batch_major(_rmsnorm(rows_ref[...], g_final_ref[...]), tt, bb)

    _interleave(_ffn_ple_pieces(x_rows, rows_ref, h_in_ref, p_ref, w_gate_ref, w_up_ref, w_down_ref,
                                g_ple_ref[...], ple_in_ref, ple_gate_ref, h_ref, pt_ref, act_ref,
                                lambda: None, emit))


def _ffn1_layer(x, p, prm):
    t_len, b_len, _ = x.shape
    tt, bb = _tiles(b_len, t_len, FFN_ROWS)
    return pl.pallas_call(
        functools.partial(_ffn1_kernel, tt=tt, bb=bb),
        grid=(b_len // bb, t_len // tt),
        in_specs=[
            pl.BlockSpec((tt, bb, D_MODEL), lambda j, i: (i, j, 0)),
            pl.BlockSpec((None, bb, tt, p.shape[-1]), lambda j, i: (1, j, i, 0)),
        ] + _ffn_weight_specs(prm, 1) + [_const_spec((1, D_MODEL))],
        out_specs=pl.BlockSpec((bb, tt, D_MODEL), lambda j, i: (j, i, 0)),
        out_shape=jax.ShapeDtypeStruct((b_len, t_len, D_MODEL), F32),
        scratch_shapes=([pltpu.VMEM((tt * bb, D_MODEL), F32)]
                        + _ffn_scratch(tt * bb, prm['ffn_w_gate'].shape[-1], p.shape[-1])),
        compiler_params=_params(),
        name="layer1_ffn",
    )(x, p, *_ffn_weights(prm), prm['g_final'])


def _run_trunk(x, p, start, pool_state, ssm_re, ssm_im, prm):
    b_len = x.shape[0]
    xt, pool_new = _layer0(x, pool_state, p, prm, start=start)
    xt, s_re, s_im = _ssm_layer(xt, ssm_re.reshape(b_len, SSM_FLAT), ssm_im.reshape(b_len, SSM_FLAT), prm)
    y = _ffn1_layer(xt, p, prm)
    s_re = s_re.reshape(1, b_len, SSM_GROUPS, SSM_STATE)
    s_im = s_im.reshape(1, b_len, SSM_GROUPS, SSM_STATE)
    return y, pool_new[None], s_re, s_im


def kernel(x_prompt, x_sample, state_pool, state_ssm_re, state_ssm_im, p_prompt, p_sample, g_mix, g_ffn, g_ple, g_final, pool_w, pool_scale, ssm_lambda_re, ssm_lambda_im, ssm_log_dt, ssm_b_re, ssm_b_im, ssm_c_re, ssm_c_im, ssm_d, ssm_w_glu, ffn_w_gate, ffn_w_up, ffn_w_down, ple_w_in, ple_w_gate):
    depth = g_mix.shape[0]
    a2_re, a2_im, ssm_in, ssm_out, ssm_direct = _ssm_prep(
        ssm_lambda_re[0], ssm_lambda_im[0], ssm_log_dt[0], ssm_b_re[0], ssm_b_im[0], ssm_c_re[0], ssm_c_im[0])
    per_layer_vec = lambda a: a.reshape(depth, 1, D_MODEL)
    prm = dict(
        g_mix=per_layer_vec(g_mix), g_ffn=per_layer_vec(g_ffn), g_ple=per_layer_vec(g_ple),
        g_final=g_final.reshape(1, D_MODEL),
        pool_w=pool_w[0].astype(BF16), pool_scale=pool_scale,
        a2_re=a2_re, a2_im=a2_im,
        ssm_in=ssm_in, ssm_out=ssm_out, ssm_direct=ssm_direct,
        ssm_d=ssm_d, ssm_w_glu=ssm_w_glu[0].astype(BF16),
        ffn_w_gate=ffn_w_gate.astype(BF16), ffn_w_up=ffn_w_up.astype(BF16),
        ffn_w_down=ffn_w_down.astype(BF16),
        ple_w_in=ple_w_in.astype(BF16), ple_w_gate=ple_w_gate.astype(BF16),
    )
    b_p = x_prompt.shape[0]
    zeros_pool = jnp.zeros((b_p, POOL_BUF, D_MODEL), F32)
    zeros_ssm = jnp.zeros((b_p, SSM_GROUPS, SSM_STATE), F32)
    y_p, pool_p, re_p, im_p = _run_trunk(x_prompt, p_prompt, 0, zeros_pool, zeros_ssm, zeros_ssm, prm)
    y_s, pool_s, re_s, im_s = _run_trunk(x_sample, p_sample, PAST_LEN, state_pool[0],
                                         state_ssm_re[0], state_ssm_im[0], prm)
    return (y_p, y_s, pool_p, pool_s, re_p, im_p, re_s, im_s)
```

```python
import functools
import math

import jax
import jax.numpy as jnp
from jax import lax
from jax.experimental import pallas as pl
from jax.experimental.pallas import tpu as pltpu

D_MODEL = 1024
POOL_WINDOWS = (2, 4, 8, 16)
POOL_GROUP_DIM = D_MODEL // len(POOL_WINDOWS)
POOL_BUF = max(POOL_WINDOWS) - 1
SSM_GROUP_DIM = 16
SSM_GROUPS = D_MODEL // SSM_GROUP_DIM
SSM_STATE = 64
SSM_FLAT = SSM_GROUPS * SSM_STATE
PAST_LEN = 16384
EPS = 1e-6

V7X_SUBLANES = 8
V7X_LANES = 128
V7X_MXU_DIM = 256
V7X_VMEM_LIMIT_BYTES = 60 * 1024 * 1024

SSM_COL_BLOCK = V7X_MXU_DIM
SSM_GROUPS_PER_BLOCK = SSM_COL_BLOCK // SSM_GROUP_DIM
SSM_STATE_BLOCK = SSM_GROUPS_PER_BLOCK * SSM_STATE
SSM_N_BLOCKS = D_MODEL // SSM_COL_BLOCK
SSM_LANE_GROUPS = V7X_LANES // SSM_GROUP_DIM
SSM_LANE_STATES = SSM_LANE_GROUPS * SSM_STATE
SSM_N_LANE_BLOCKS = D_MODEL // V7X_LANES
SSM_PAIR_COLS = 2 * V7X_LANES
SSM_PAIR_STATES = 2 * SSM_LANE_STATES

POOL_LAYER_ROWS = 512
SSM_ROWS = 1024
FFN_ROWS = 1024
FFN_COLS = 2 * V7X_MXU_DIM
OUT_COLS = 2 * V7X_MXU_DIM

BF16 = jnp.bfloat16
F32 = jnp.float32


def _rmsnorm(x, g):
    return x * lax.rsqrt(jnp.mean(x * x, axis=-1, keepdims=True) + EPS) * g


def _dot(a, b):
    return jnp.dot(a, b, preferred_element_type=F32)


def _dot_nt_f32(a, b):
    return lax.dot_general(a, b, (((1,), (1,)), ((), ())), precision=lax.Precision.HIGHEST,
                           preferred_element_type=F32)


def _time_major(block):
    bb, tt, c = block.shape
    return jnp.transpose(block, (1, 0, 2)).reshape(tt * bb, c)


def _batch_major(rows, tt, bb):
    return jnp.transpose(rows.reshape(tt, bb, rows.shape[-1]), (1, 0, 2))


def _const_spec(shape):
    zeros = (0,) * len(shape)
    return pl.BlockSpec(shape, lambda *_: zeros, pipeline_mode=pl.Buffered(1))


def _layer_spec(shape, layer):
    tail = (0,) * len(shape)
    return pl.BlockSpec((None,) + tuple(shape), lambda *_: (layer,) + tail,
                        pipeline_mode=pl.Buffered(1))


def _params():
    return pltpu.CompilerParams(dimension_semantics=("arbitrary", "arbitrary"),
                                vmem_limit_bytes=V7X_VMEM_LIMIT_BYTES)


def _tiles(b_len, t_len, rows):
    bb = min(b_len, max(V7X_SUBLANES, rows // t_len))
    tt = min(t_len, rows // bb)
    assert t_len % tt == 0 and b_len % bb == 0 and bb % V7X_SUBLANES == 0
    return tt, bb


class _Schedule:
    def __init__(self, b_len, t_len, rows):
        self.tt, self.bb = _tiles(b_len, t_len, rows)
        self.nt = t_len // self.tt
        self.lag = 1 if self.nt > 1 else 0
        self.grid = (b_len // self.bb, self.nt + self.lag)

    def mixer_block(self, i):
        return jnp.minimum(i, self.nt - 1)

    def ffn_block(self, i):
        return jnp.maximum(i - self.lag, 0)


def _ssm_prep_kernel(lam_re_ref, lam_im_ref, log_dt_ref, b_re_ref, b_im_ref, c_re_ref, c_im_ref,
                     a2_re_ref, a2_im_ref, in_ref, out_ref, direct_ref,
                     lb_re_row, lb_im_row, wb_re, wb_im, wc_re, wc_im):
    lam_re = lam_re_ref[...]
    lam_im = lam_im_ref[...]
    dt = jnp.exp(log_dt_ref[...])
    mag = jnp.exp(lam_re * dt)
    lb_re = mag * jnp.cos(lam_im * dt)
    lb_im = mag * jnp.sin(lam_im * dt)
    den = lam_re * lam_re + lam_im * lam_im
    f_re = ((lb_re - 1.0) * lam_re + lb_im * lam_im) / den
    f_im = (lb_im * lam_re - (lb_re - 1.0) * lam_im) / den
    b_re = jnp.swapaxes(b_re_ref[...], 1, 2)
    b_im = jnp.swapaxes(b_im_ref[...], 1, 2)
    fr = f_re[:, None, :]
    fi = f_im[:, None, :]
    bb_re = fr * b_re - fi * b_im
    bb_im = fr * b_im + fi * b_re
    c_re = c_re_ref[...]
    c_im_neg = -c_im_ref[...]

    for ref in (wb_re, wb_im, wc_re, wc_im):
        ref[...] = jnp.zeros(ref.shape, ref.dtype)
    for g in range(SSM_GROUPS):
        cb, gl = divmod(g, SSM_GROUPS_PER_BLOCK)
        cols = slice(gl * SSM_GROUP_DIM, (gl + 1) * SSM_GROUP_DIM)
        states = slice(gl * SSM_STATE, (gl + 1) * SSM_STATE)
        wb_re[cb, cols, states] = bb_re[g]
        wb_im[cb, cols, states] = bb_im[g]
        wc_re[cb, cols, states] = c_re[g]
        wc_im[cb, cols, states] = c_im_neg[g]
        flat = slice(g * SSM_STATE, (g + 1) * SSM_STATE)
        lb_re_row[:, flat] = lb_re[g:g + 1, :]
        lb_im_row[:, flat] = lb_im[g:g + 1, :]

    lr_all = lb_re_row[...]
    li_all = lb_im_row[...]
    a2_re_ref[...] = lr_all * lr_all - li_all * li_all
    a2_im_ref[...] = 2.0 * lr_all * li_all

    halves = SSM_COL_BLOCK // V7X_LANES
    for cb in range(SSM_N_BLOCKS):
        flat = slice(cb * SSM_STATE_BLOCK, (cb + 1) * SSM_STATE_BLOCK)
        lr, li = lb_re_row[:, flat], lb_im_row[:, flat]
        l2r, l2i = a2_re_ref[:, flat], a2_im_ref[:, flat]
        w_r, w_i = wb_re[cb], wb_im[cb]
        wl_r, wl_i = w_r * lr - w_i * li, w_r * li + w_i * lr
        c_r, c_n = wc_re[cb], wc_im[cb]
        o1_r, o1_i = c_r * lr + c_n * li, c_n * lr - c_r * li
        o2_r, o2_i = c_r * l2r + c_n * l2i, c_n * l2r - c_r * l2i
        for half in range(halves):
            j = cb * halves + half
            r = slice(half * V7X_LANES, (half + 1) * V7X_LANES)
            s = slice(half * SSM_LANE_STATES, (half + 1) * SSM_LANE_STATES)
            t0, t1 = slice(0, V7X_LANES), slice(V7X_LANES, SSM_PAIR_COLS)
            re, im = slice(0, SSM_LANE_STATES), slice(SSM_LANE_STATES, SSM_PAIR_STATES)
            in_ref[j, t0, re] = wl_r[r, s].astype(BF16)
            in_ref[j, t0, im] = wl_i[r, s].astype(BF16)
            in_ref[j, t1, re] = w_r[r, s].astype(BF16)
            in_ref[j, t1, im] = w_i[r, s].astype(BF16)
            k0 = _dot_nt_f32(w_r[r, s], c_r[r, s]) + _dot_nt_f32(w_i[r, s], c_n[r, s])
            k1 = _dot_nt_f32(wl_r[r, s], c_r[r, s]) + _dot_nt_f32(wl_i[r, s], c_n[r, s])
            direct_ref[j, t0, t0] = k0.astype(BF16)
            direct_ref[j, t0, t1] = k1.astype(BF16)
            direct_ref[j, t1, t0] = jnp.zeros((V7X_LANES, V7X_LANES), BF16)
            direct_ref[j, t1, t1] = k0.astype(BF16)
            out_ref[j, re, t0] = o1_r[r, s].T.astype(BF16)
            out_ref[j, im, t0] = o1_i[r, s].T.astype(BF16)
            out_ref[j, re, t1] = o2_r[r, s].T.astype(BF16)
            out_ref[j, im, t1] = o2_i[r, s].T.astype(BF16)


def _ssm_prep(lam_re, lam_im, log_dt, b_re, b_im, c_re, c_im):
    g, p, _ = b_re.shape
    row = jax.ShapeDtypeStruct((1, SSM_FLAT), F32)
    n = SSM_N_LANE_BLOCKS
    block_diag = pltpu.VMEM((SSM_N_BLOCKS, SSM_COL_BLOCK, SSM_STATE_BLOCK), F32)
    return pl.pallas_call(
        _ssm_prep_kernel,
        out_shape=(row, row,
                   jax.ShapeDtypeStruct((n, SSM_PAIR_COLS, SSM_PAIR_STATES), BF16),
                   jax.ShapeDtypeStruct((n, SSM_PAIR_STATES, SSM_PAIR_COLS), BF16),
                   jax.ShapeDtypeStruct((n, SSM_PAIR_COLS, SSM_PAIR_COLS), BF16)),
        scratch_shapes=[pltpu.VMEM((1, SSM_FLAT), F32), pltpu.VMEM((1, SSM_FLAT), F32),
                        block_diag, block_diag, block_diag, block_diag],
        compiler_params=pltpu.CompilerParams(vmem_limit_bytes=V7X_VMEM_LIMIT_BYTES),
        name="ssm_prep",
    )(lam_re, lam_im, log_dt.reshape(g, 1), b_re, b_im, c_re, c_im)


def _pool_pieces(x_ref, dst_ref, ext_ref, w_ref, scale_ref, g, *, tt, bb, first_pos):
    tm = tt * bb
    halo = POOL_BUF * bb
    dst_ref[...] = _time_major(x_ref[...])
    ext_ref[halo:halo + tm, :] = _rmsnorm(dst_ref[...], g)
    yield
    row = lax.broadcasted_iota(jnp.int32, (tm, 1), 0)
    pos = first_pos + lax.shift_right_logical(row, int(math.log2(bb)))
    for gi, w in enumerate(POOL_WINDOWS):
        cols = slice(gi * POOL_GROUP_DIM, (gi + 1) * POOL_GROUP_DIM)
        acc = ext_ref[halo:halo + tm, cols]
        for k in range(1, w):
            acc = acc + ext_ref[halo - k * bb:halo - k * bb + tm, cols]
        inv_cnt = 1.0 / jnp.minimum(w, pos + 1).astype(F32)
        diff = acc * inv_cnt - ext_ref[halo:halo + tm, cols]
        dst_ref[:, cols] += _dot(diff.astype(BF16), w_ref[gi]) * scale_ref[:, cols]
        yield


def _ffn_ple_pieces(x_ref, rows_ref, h_in_ref, p_ref, w_gate_ref, w_up_ref, w_down_ref, g_ple, ple_in_ref,
                    ple_gate_ref, h_ref, pt_ref, act_ref, release_h_in, emit):
    d_ff = act_ref.shape[1]
    pt_ref[...] = _time_major(p_ref[...]).astype(BF16)
    for c0 in range(0, d_ff, FFN_COLS):
        cols = slice(c0, min(c0 + FFN_COLS, d_ff))
        gate = _dot(h_in_ref[...], w_gate_ref[:, cols])
        up = _dot(h_in_ref[...], w_up_ref[:, cols])
        act_ref[:, cols] = (jax.nn.silu(gate) * up).astype(BF16)
        if cols.stop == d_ff:
            release_h_in()
        yield
    for n0 in range(0, D_MODEL, OUT_COLS):
        cols = slice(n0, n0 + OUT_COLS)
        rows_ref[:, cols] = x_ref[:, cols] + _dot(act_ref[...], w_down_ref[:, cols])
        yield
    h_ref[...] = _rmsnorm(rows_ref[...], g_ple).astype(BF16)
    for n0 in range(0, D_MODEL, OUT_COLS):
        cols = slice(n0, n0 + OUT_COLS)
        ple_gate = jax.nn.sigmoid(_dot(h_ref[...], ple_gate_ref[:, cols]))
        rows_ref[:, cols] += _dot(pt_ref[...], ple_in_ref[:, cols]) * ple_gate
        yield
    emit()
    yield


_DONE = object()


def _interleave(*stages):
    live = list(stages)
    while live:
        for stage in list(live):
            if next(stage, _DONE) is _DONE:
                live.remove(stage)


def _then(stage, last_piece):
    yield from stage
    last_piece()
    yield


def _run_halves(sched, g_ffn, mid_ref, hmid_ref, rows_ref, mixer, ffn):
    h_in_reads_issued = []

    def mixer_into(dst_ref):
        def norm_for_ffn():
            assert h_in_reads_issued or not sched.lag, "hmid_ref rewritten before the FFN half read it"
            hmid_ref[...] = _rmsnorm(dst_ref[...], g_ffn).astype(BF16)
        return _then(mixer(dst_ref), norm_for_ffn)

    def release_h_in():
        h_in_reads_issued.append(True)

    if sched.lag:
        rows_ref[...] = mid_ref[...]
        _interleave(ffn(hmid_ref, release_h_in), mixer_into(mid_ref))
    else:
        _interleave(mixer_into(rows_ref))
        _interleave(ffn(hmid_ref, release_h_in))


def _init_handoff(sched, mid_ref, hmid_ref):
    if sched.lag:
        mid_ref[...] = jnp.zeros(mid_ref.shape, mid_ref.dtype)
        hmid_ref[...] = jnp.zeros(hmid_ref.shape, hmid_ref.dtype)


def _ffn_scratch(tm, d_ff, ple_dim):
    return [
        pltpu.VMEM((tm, D_MODEL), BF16),
        pltpu.VMEM((tm, D_MODEL), BF16),
        pltpu.VMEM((tm, ple_dim), BF16),
        pltpu.VMEM((tm, d_ff), BF16),
    ]


def _layer0_kernel(x_ref, prev_ref, p_ref, g_mix_ref, pool_w_ref, pool_scale_ref,
                   g_ffn_ref, w_gate_ref, w_up_ref, w_down_ref, g_ple_ref, ple_in_ref, ple_gate_ref,
                   o_ref, state_ref, ext_ref, mid_ref, hmid_ref, h_ref, pt_ref, act_ref,
                   *, sched, start):
    tt, bb = sched.tt, sched.bb
    tm = tt * bb
    halo = POOL_BUF * bb
    i = pl.program_id(1)
    rows_ref = o_ref.reshape(tm, D_MODEL)

    @pl.when(i == 0)
    def _():
        ext_ref[0:halo, :] = _time_major(prev_ref[...])
        _init_handoff(sched, mid_ref, hmid_ref)

    @pl.when(i > 0)
    def _():
        ext_ref[0:halo, :] = ext_ref[tm:tm + halo, :]

    def mixer(dst_ref):
        return _pool_pieces(x_ref, dst_ref, ext_ref, pool_w_ref, pool_scale_ref, g_mix_ref[...],
                            tt=tt, bb=bb, first_pos=start + sched.mixer_block(i) * tt)

    def ffn(h_in_ref, release_h_in):
        return _ffn_ple_pieces(rows_ref, rows_ref, h_in_ref, p_ref, w_gate_ref, w_up_ref, w_down_ref,
                               g_ple_ref[...], ple_in_ref, ple_gate_ref, h_ref, pt_ref, act_ref,
                               release_h_in, lambda: None)

    _run_halves(sched, g_ffn_ref[...], mid_ref, hmid_ref, rows_ref, mixer, ffn)

    @pl.when(i == sched.nt - 1)
    def _():
        state_ref[...] = _batch_major(ext_ref[tm:tm + halo, :], POOL_BUF, bb)


def _ffn_weight_specs(prm, layer):
    return [
        _layer_spec((1, D_MODEL), layer),
        _layer_spec(prm['ffn_w_gate'].shape[1:], layer), _layer_spec(prm['ffn_w_up'].shape[1:], layer),
        _layer_spec(prm['ffn_w_down'].shape[1:], layer),
        _layer_spec((1, D_MODEL), layer),
        _layer_spec(prm['ple_w_in'].shape[1:], layer), _layer_spec(prm['ple_w_gate'].shape[1:], layer),
    ]


def _ffn_weights(prm):
    return (prm['g_ffn'], prm['ffn_w_gate'], prm['ffn_w_up'], prm['ffn_w_down'],
            prm['g_ple'], prm['ple_w_in'], prm['ple_w_gate'])


def _layer0(x, prev, p, prm, *, start):
    b_len, t_len, _ = x.shape
    sched = _Schedule(b_len, t_len, POOL_LAYER_ROWS)
    tt, bb = sched.tt, sched.bb
    tm = tt * bb
    state_spec = pl.BlockSpec((bb, POOL_BUF, D_MODEL), lambda j, i: (j, 0, 0))
    mid_rows = tm if sched.lag else 2 * V7X_SUBLANES
    return pl.pallas_call(
        functools.partial(_layer0_kernel, sched=sched, start=start),
        grid=sched.grid,
        in_specs=[
            pl.BlockSpec((bb, tt, D_MODEL), lambda j, i: (j, sched.mixer_block(i), 0)),
            state_spec,
            pl.BlockSpec((None, bb, tt, p.shape[-1]), lambda j, i: (0, j, sched.ffn_block(i), 0)),
            _layer_spec((1, D_MODEL), 0),
            _const_spec(prm['pool_w'].shape),
            _const_spec((1, D_MODEL)),
        ] + _ffn_weight_specs(prm, 0),
        out_specs=[pl.BlockSpec((tt, bb, D_MODEL), lambda j, i: (sched.ffn_block(i), j, 0)), state_spec],
        out_shape=[
            jax.ShapeDtypeStruct((t_len, b_len, D_MODEL), F32),
            jax.ShapeDtypeStruct((b_len, POOL_BUF, D_MODEL), F32),
        ],
        scratch_shapes=([pltpu.VMEM(((POOL_BUF + tt) * bb, D_MODEL), F32),
                         pltpu.VMEM((mid_rows, D_MODEL), F32)]
                        + _ffn_scratch(tm, prm['ffn_w_gate'].shape[-1], p.shape[-1])),
        compiler_params=_params(),
        name="layer0_pool_ffn",
    )(x, prev, p, prm['g_mix'], prm['pool_w'], prm['pool_scale'], *_ffn_weights(prm))


def _ssm_kernel(x_ref, h0_re_ref, h0_im_ref, g_ref, a2_re_ref, a2_im_ref, in_ref, out_ref, direct_ref,
                d_ref, wglu_ref, o_ref, s_re_ref, s_im_ref,
                st_re, st_im, u_ref, xj_ref, sbuf_ref, z_ref, *, pairs, bb):
    m = pairs * bb
    i = pl.program_id(1)

    @pl.when(i == 0)
    def _():
        st_re[...] = h0_re_ref[...]
        st_im[...] = h0_im_ref[...]

    for t in range(2):
        u_ref[t] = _rmsnorm(x_ref[:, t].reshape(m, D_MODEL), g_ref[...])

    def contributions(j):
        cols = slice(j * V7X_LANES, (j + 1) * V7X_LANES)
        both = jnp.concatenate([u_ref[0, :, cols], u_ref[1, :, cols]], axis=-1)
        xj_ref[j] = both.astype(BF16)
        sbuf_ref[:, j * SSM_PAIR_STATES:(j + 1) * SSM_PAIR_STATES] = _dot(xj_ref[j], in_ref[j])

    def scan(j):
        states = slice(j * SSM_LANE_STATES, (j + 1) * SSM_LANE_STATES)
        re = slice(j * SSM_PAIR_STATES, j * SSM_PAIR_STATES + SSM_LANE_STATES)
        im = slice(j * SSM_PAIR_STATES + SSM_LANE_STATES, (j + 1) * SSM_PAIR_STATES)
        ar = jnp.broadcast_to(a2_re_ref[:, states], (V7X_SUBLANES, SSM_LANE_STATES))
        ai = jnp.broadcast_to(a2_im_ref[:, states], (V7X_SUBLANES, SSM_LANE_STATES))
        for r0 in range(0, bb, V7X_SUBLANES):
            sr = st_re[r0:r0 + V7X_SUBLANES, states]
            si = st_im[r0:r0 + V7X_SUBLANES, states]
            for c in range(pairs):
                rows = slice(c * bb + r0, c * bb + r0 + V7X_SUBLANES)
                pr, pi = sbuf_ref[rows, re], sbuf_ref[rows, im]
                sbuf_ref[rows, re] = sr
                sbuf_ref[rows, im] = si
                sr, si = ar * sr - ai * si + pr, ar * si + ai * sr + pi
            st_re[r0:r0 + V7X_SUBLANES, states] = sr
            st_im[r0:r0 + V7X_SUBLANES, states] = si

    def outputs(j):
        cols = slice(j * V7X_LANES, (j + 1) * V7X_LANES)
        entering = sbuf_ref[:, j * SSM_PAIR_STATES:(j + 1) * SSM_PAIR_STATES].astype(BF16)
        y = _dot(entering, out_ref[j]) + _dot(xj_ref[j], direct_ref[j])
        for t in range(2):
            yt = y[:, t * V7X_LANES:(t + 1) * V7X_LANES] + d_ref[:, cols] * u_ref[t, :, cols]
            z_ref[t, :, cols] = jax.nn.gelu(yt).astype(BF16)

    for k in range(SSM_N_LANE_BLOCKS + 2):
        if 0 <= k - 2:
            outputs(k - 2)
        if k < SSM_N_LANE_BLOCKS:
            contributions(k)
        if 0 <= k - 1 < SSM_N_LANE_BLOCKS:
            scan(k - 1)

    z_rows = z_ref.reshape(2 * m, D_MODEL)
    for n0 in range(0, D_MODEL, OUT_COLS):
        cols = slice(n0, n0 + OUT_COLS)
        a = _dot(z_rows[...], wglu_ref[:, cols])
        gate = _dot(z_rows[...], wglu_ref[:, D_MODEL + n0:D_MODEL + n0 + OUT_COLS])
        glu = a * jax.nn.sigmoid(gate)
        for t in range(2):
            o_ref[:, t, :, cols] = x_ref[:, t, :, cols] + glu[t * m:(t + 1) * m].reshape(pairs, bb, OUT_COLS)

    s_re_ref[...] = st_re[...]
    s_im_ref[...] = st_im[...]


def _ssm_layer(x, h0_re, h0_im, prm):
    t_len, b_len, _ = x.shape
    tt, bb = _tiles(b_len, t_len, SSM_ROWS)
    assert tt % 2 == 0
    pairs = tt // 2
    m = pairs * bb
    x_spec = pl.BlockSpec((pairs, 2, bb, D_MODEL), lambda j, i: (i, 0, j, 0))
    st_spec = pl.BlockSpec((bb, SSM_FLAT), lambda j, i: (j, 0))
    out, s_re, s_im = pl.pallas_call(
        functools.partial(_ssm_kernel, pairs=pairs, bb=bb),
        grid=(b_len // bb, t_len // tt),
        in_specs=[
            x_spec, st_spec, st_spec,
            _layer_spec((1, D_MODEL), 1),
            _const_spec((1, SSM_FLAT)), _const_spec((1, SSM_FLAT)),
            _const_spec(prm['ssm_in'].shape), _const_spec(prm['ssm_out'].shape),
            _const_spec(prm['ssm_direct'].shape),
            _const_spec((1, D_MODEL)),
            _const_spec(prm['ssm_w_glu'].shape),
        ],
        out_specs=[x_spec, st_spec, st_spec],
        out_shape=[
            jax.ShapeDtypeStruct((t_len // 2, 2, b_len, D_MODEL), F32),
            jax.ShapeDtypeStruct((b_len, SSM_FLAT), F32),
            jax.ShapeDtypeStruct((b_len, SSM_FLAT), F32),
        ],
        scratch_shapes=[
            pltpu.VMEM((bb, SSM_FLAT), F32), pltpu.VMEM((bb, SSM_FLAT), F32),
            pltpu.VMEM((2, m, D_MODEL), F32),
            pltpu.VMEM((SSM_N_LANE_BLOCKS, m, SSM_PAIR_COLS), BF16),
            pltpu.VMEM((m, SSM_N_LANE_BLOCKS * SSM_PAIR_STATES), F32),
            pltpu.VMEM((2, m, D_MODEL), BF16),
        ],
        compiler_params=_params(),
        name="ssm_mixer",
    )(x.reshape(t_len // 2, 2, b_len, D_MODEL), h0_re, h0_im, prm['g_mix'], prm['a2_re'], prm['a2_im'],
      prm['ssm_in'], prm['ssm_out'], prm['ssm_direct'], prm['ssm_d'], prm['ssm_w_glu'])
    return out.reshape(t_len, b_len, D_MODEL), s_re, s_im


def _ffn1_kernel(x_ref, p_ref, g_ffn_ref, w_gate_ref, w_up_ref, w_down_ref, g_ple_ref, ple_in_ref, ple_gate_ref,
                 g_final_ref, o_ref, rows_ref, h_in_ref, h_ref, pt_ref, act_ref, *, tt, bb):
    x_rows = x_ref.reshape(tt * bb, D_MODEL)
    h_in_ref[...] = _rmsnorm(x_rows[...], g_ffn_ref[...]).astype(BF16)

    def emit():
        o_ref[...] = _batch_major(_rmsnorm(rows_ref[...], g_final_ref[...]), tt, bb)

    _interleave(_ffn_ple_pieces(x_rows, rows_ref, h_in_ref, p_ref, w_gate_ref, w_up_ref, w_down_ref,
                                g_ple_ref[...], ple_in_ref, ple_gate_ref, h_ref, pt_ref, act_ref,
                                lambda: None, emit))


def _ffn1_layer(x, p, prm):
    t_len, b_len, _ = x.shape
    tt, bb = _tiles(b_len, t_len, FFN_ROWS)
    return pl.pallas_call(
        functools.partial(_ffn1_kernel, tt=tt, bb=bb),
        grid=(b_len // bb, t_len // tt),
        in_specs=[
            pl.BlockSpec((tt, bb, D_MODEL), lambda j, i: (i, j, 0)),
            pl.BlockSpec((None, bb, tt, p.shape[-1]), lambda j, i: (1, j, i, 0)),
        ] + _ffn_weight_specs(prm, 1) + [_const_spec((1, D_MODEL))],
        out_specs=pl.BlockSpec((bb, tt, D_MODEL), lambda j, i: (j, i, 0)),
        out_shape=jax.ShapeDtypeStruct((b_len, t_len, D_MODEL), F32),
        scratch_shapes=([pltpu.VMEM((tt * bb, D_MODEL), F32)]
                        + _ffn_scratch(tt * bb, prm['ffn_w_gate'].shape[-1], p.shape[-1])),
        compiler_params=_params(),
        name="layer1_ffn",
    )(x, p, *_ffn_weights(prm), prm['g_final'])


def _run_trunk(x, p, start, pool_state, ssm_re, ssm_im, prm):
    b_len = x.shape[0]
    xt, pool_new = _layer0(x, pool_state, p, prm, start=start)
    xt, s_re, s_im = _ssm_layer(xt, ssm_re.reshape(b_len, SSM_FLAT), ssm_im.reshape(b_len, SSM_FLAT), prm)
    y = _ffn1_layer(xt, p, prm)
    s_re = s_re.reshape(1, b_len, SSM_GROUPS, SSM_STATE)
    s_im = s_im.reshape(1, b_len, SSM_GROUPS, SSM_STATE)
    return y, pool_new[None], s_re, s_im


def kernel(x_prompt, x_sample, state_pool, state_ssm_re, state_ssm_im, p_prompt, p_sample, g_mix, g_ffn, g_ple, g_final, pool_w, pool_scale, ssm_lambda_re, ssm_lambda_im, ssm_log_dt, ssm_b_re, ssm_b_im, ssm_c_re, ssm_c_im, ssm_d, ssm_w_glu, ffn_w_gate, ffn_w_up, ffn_w_down, ple_w_in, ple_w_gate):
    depth = g_mix.shape[0]
    a2_re, a2_im, ssm_in, ssm_out, ssm_direct = _ssm_prep(
        ssm_lambda_re[0], ssm_lambda_im[0], ssm_log_dt[0], ssm_b_re[0], ssm_b_im[0], ssm_c_re[0], ssm_c_im[0])
    per_layer_vec = lambda a: a.reshape(depth, 1, D_MODEL)
    prm = dict(
        g_mix=per_layer_vec(g_mix), g_ffn=per_layer_vec(g_ffn), g_ple=per_layer_vec(g_ple),
        g_final=g_final.reshape(1, D_MODEL),
        pool_w=pool_w[0].astype(BF16), pool_scale=pool_scale,
        a2_re=a2_re, a2_im=a2_im,
        ssm_in=ssm_in, ssm_out=ssm_out, ssm_direct=ssm_direct,
        ssm_d=ssm_d, ssm_w_glu=ssm_w_glu[0].astype(BF16),
        ffn_w_gate=ffn_w_gate.astype(BF16), ffn_w_up=ffn_w_up.astype(BF16),
        ffn_w_down=ffn_w_down.astype(BF16),
        ple_w_in=ple_w_in.astype(BF16), ple_w_gate=ple_w_gate.astype(BF16),
    )
    b_p = x_prompt.shape[0]
    zeros_pool = jnp.zeros((b_p, POOL_BUF, D_MODEL), F32)
    zeros_ssm = jnp.zeros((b_p, SSM_GROUPS, SSM_STATE), F32)
    y_p, pool_p, re_p, im_p = _run_trunk(x_prompt, p_prompt, 0, zeros_pool, zeros_ssm, zeros_ssm, prm)
    y_s, pool_s, re_s, im_s = _run_trunk(x_sample, p_sample, PAST_LEN, state_pool[0],
                                         state_ssm_re[0], state_ssm_im[0], prm)
    return (y_p, y_s, pool_p, pool_s, re_p, im_p, re_s, im_s)
```
